```python
import jax, jax.numpy as jnp
from jax import lax
import numpy as np

D_MODEL = 1024
BATCH = 8
SEQ = 8192
DEPTH = 2

CHUNK = 64
N_A = DEPTH // 2
N_B = DEPTH - N_A

RWKV_HEAD = 64
RWKV_HEADS = D_MODEL // RWKV_HEAD
DECAY_LORA = 64
AAA_LORA = 64
GATE_LORA = 160
RWKV_LN_EPS = 64e-5

SB_HEAD = 64
SB_HEADS = D_MODEL // SB_HEAD
Q_BLOCK = 128

MEM_TOKENS = 256
MEM_HEADS = 4
MEM_HEAD = D_MODEL // MEM_HEADS

N_GROUPS = 4
EXPERTS_PER_GROUP = 8
N_EXPERTS = N_GROUPS * EXPERTS_PER_GROUP
TOP_K = 2
D_EXPERT = D_MODEL // 4
MOE_BLOCK = 128

NORM_EPS = 1e-6

kernel_name = 'rwkv7_stickbreak_yoco_hmoe'


def rmsnorm(x, g):
    xf = x.astype(jnp.float32)
    y = xf * lax.rsqrt(jnp.mean(xf * xf, axis=-1, keepdims=True) + NORM_EPS)
    return (y * g.astype(jnp.float32)).astype(x.dtype)


def rwkv7_scan(r, decay, k, v, a, b):
    B, S, H, N = r.shape

    def step(state, inp):
        r_t, w_t, k_t, v_t, a_t, b_t = inp
        sa = jnp.einsum('bhij,bhj->bhi', state, a_t)
        state = (state * w_t[:, :, None, :]
                 + sa[..., None] * b_t[:, :, None, :]
                 + v_t[..., None] * k_t[:, :, None, :])
        return state, jnp.einsum('bhij,bhj->bhi', state, r_t)

    tm = lambda t: jnp.moveaxis(t, 1, 0)
    s0 = jnp.zeros((B, H, N, N), jnp.float32)
    _, y = lax.scan(step, s0, (tm(r), tm(decay), tm(k), tm(v), tm(a), tm(b)))
    return jnp.moveaxis(y, 0, 1)


def rwkv7_time_mix(h, mu, w_rkv, w0, w1, w2, a0, a1, a2, g1, g2, k_k, k_a, r_k, ln_g, ln_b, w_o):
    B, S, D = h.shape
    H, N = RWKV_HEADS, RWKV_HEAD
    f32 = jnp.float32
    xx = jnp.pad(h, ((0, 0), (1, 0), (0, 0)))[:, :S] - h
    lerp = lambda i: h + xx * mu[i]
    rkv = jnp.einsum('nbsd,nde->nbse', jnp.stack([lerp(0), lerp(1), lerp(2)]), w_rkv).astype(f32)
    r, k, v = rkv[0], rkv[1], rkv[2]
    w = -jax.nn.softplus(-(w0 + jnp.tanh(lerp(3) @ w1) @ w2).astype(f32)) - 0.5
    decay = jnp.exp(-jnp.exp(w))
    a = jax.nn.sigmoid((a0 + (lerp(4) @ a1) @ a2).astype(f32))
    g = jax.nn.sigmoid(lerp(5) @ g1) @ g2
    heads = lambda t: t.reshape(B, S, H, N)
    kk = heads(k * k_k)
    kk = kk / jnp.maximum(jnp.sqrt(jnp.sum(kk * kk, axis=-1, keepdims=True)), 1e-12)
    k = k * (1.0 + (a - 1.0) * k_a)
    r, k, v, a, decay = heads(r), heads(k), heads(v), heads(a), heads(decay)
    y = rwkv7_scan(r, decay, k, v, -kk, kk * a)
    mean = jnp.mean(y, axis=-1, keepdims=True)
    var = jnp.mean(jnp.square(y - mean), axis=-1, keepdims=True)
    y = ((y - mean) * lax.rsqrt(var + RWKV_LN_EPS)).reshape(B, S, D) * ln_g + ln_b
    bonus = jnp.sum(r * k * r_k, axis=-1, keepdims=True) * v
    y = y + bonus.reshape(B, S, D)
    return (y.astype(h.dtype) * g) @ w_o


def stick_breaking_attention(q, k, v):
    B, S, H, Dh = q.shape
    nb = S // Q_BLOCK
    blk = lambda t: t.astype(jnp.float32).reshape(B, nb, Q_BLOCK, H, Dh).transpose(1, 0, 3, 2, 4)
    qb = blk(q) * (Dh ** -0.5)
    kb, vb = blk(k), blk(v)
    pos = jnp.arange(Q_BLOCK)

    def query_block(i):
        qi = lax.dynamic_index_in_dim(qb, i, 0, keepdims=False)

        def key_block(step, carry):
            acc, log_rest = carry
            j = i - step
            kj = lax.dynamic_index_in_dim(kb, j, 0, keepdims=False)
            vj = lax.dynamic_index_in_dim(vb, j, 0, keepdims=False)
            z = jnp.einsum('bhqd,bhkd->bhqk', qi, kj)
            causal = (j * Q_BLOCK + pos)[None, :] < (i * Q_BLOCK + pos)[:, None]
            log_keep = jnp.where(causal, jax.nn.log_sigmoid(-z), 0.0)
            between = lax.cumsum(log_keep, axis=3, reverse=True) - log_keep + log_rest[..., None]
            attn = jnp.where(causal, jnp.exp(jax.nn.log_sigmoid(z) + between), 0.0)
            acc = acc + jnp.einsum('bhqk,bhkd->bhqd', attn, vj)
            return acc, log_rest + jnp.sum(log_keep, axis=-1)

        init = (jnp.zeros((B, H, Q_BLOCK, Dh), jnp.float32), jnp.zeros((B, H, Q_BLOCK), jnp.float32))
        acc, _ = lax.fori_loop(0, i + 1, key_block, init)
        return acc

    out = lax.map(query_block, jnp.arange(nb))
    return out.transpose(1, 0, 3, 2, 4).reshape(B, S, H, Dh).astype(q.dtype)


def memory_cross_attention(h, mem, g_mem, w_q, w_kv, w_o):
    B, S, D = h.shape
    M = mem.shape[1]
    m = rmsnorm(mem, g_mem)
    q = (h @ w_q).reshape(B, S, MEM_HEADS, MEM_HEAD).astype(jnp.float32)
    kv = (m @ w_kv).reshape(B, M, 2, MEM_HEADS, MEM_HEAD).astype(jnp.float32)
    s = jnp.einsum('bshd,bmhd->bhsm', q, kv[:, :, 0]) * (MEM_HEAD ** -0.5)
    p = jax.nn.softmax(s, axis=-1)
    o = jnp.einsum('bhsm,bmhd->bshd', p, kv[:, :, 1]).astype(h.dtype).reshape(B, S, D)
    return o @ w_o


def hierarchical_moe(h, w_rg, b_rg, w_re, b_re, w_gate, w_up, w_down):
    B, S, D = h.shape
    T = B * S
    f32 = jnp.float32
    xt = h.reshape(T, D)
    g_prob = jax.nn.softmax((xt @ w_rg).astype(f32) + b_rg.astype(f32), axis=-1)
    grp = jnp.argmax(g_prob, axis=-1)
    g_w = jnp.take_along_axis(g_prob, grp[:, None], axis=1)
    e_logits = ((xt @ w_re).astype(f32) + b_re.astype(f32)).reshape(T, N_GROUPS, EXPERTS_PER_GROUP)
    e_logits = jnp.take_along_axis(e_logits, grp[:, None, None], axis=1)[:, 0]
    e_w, e_loc = lax.top_k(jax.nn.softmax(e_logits, axis=-1), TOP_K)
    e_w = e_w / jnp.sum(e_w, axis=-1, keepdims=True)
    gate = (g_w * e_w).reshape(-1)
    expert = (grp[:, None] * EXPERTS_PER_GROUP + e_loc).reshape(-1).astype(jnp.int32)
    token = jnp.repeat(jnp.arange(T, dtype=jnp.int32), TOP_K)
    A = T * TOP_K
    cap = A + N_EXPERTS * MOE_BLOCK
    n_blocks = cap // MOE_BLOCK
    order = jnp.argsort(expert)
    se = expert[order]
    counts = jnp.bincount(expert, length=N_EXPERTS)
    padded = (counts + MOE_BLOCK - 1) // MOE_BLOCK * MOE_BLOCK
    start = jnp.cumsum(counts) - counts
    pend = jnp.cumsum(padded)
    pstart = pend - padded
    dest = pstart[se] + jnp.arange(A, dtype=jnp.int32) - start[se]
    slot_tok = jnp.full((cap,), T, jnp.int32).at[dest].set(token[order])
    slot_gate = jnp.zeros((cap,), f32).at[dest].set(gate[order])
    block_expert = jnp.minimum(jnp.searchsorted(pend, jnp.arange(n_blocks) * MOE_BLOCK, side='right'), N_EXPERTS - 1)
    xpad = jnp.concatenate([xt, jnp.zeros((1, D), xt.dtype)], axis=0)
    xs = xpad[slot_tok].reshape(n_blocks, MOE_BLOCK, D)

    def expert_block(args):
        xb, e = args
        hid = jax.nn.silu(xb @ w_gate[e]) * (xb @ w_up[e])
        return hid @ w_down[e]

    ys = lax.map(expert_block, (xs, block_expert)).reshape(cap, D)
    y = jax.ops.segment_sum(ys * slot_gate[:, None].astype(ys.dtype), slot_tok, num_segments=T + 1)[:T]
    return y.reshape(B, S, D).astype(h.dtype)


def setup_inputs(seed: int = 0) -> dict:
    key = jax.random.key(seed)
    ks = iter(jax.random.split(key, 64))
    f32 = jnp.float32
    D, H, N = D_MODEL, RWKV_HEADS, RWKV_HEAD

    def nrm(shape, scale):
        return jax.random.normal(next(ks), shape, f32) * scale

    def gain(shape):
        return 1.0 + nrm(shape, 0.02)

    return {
        'x': nrm((BATCH, SEQ, D), 1.0),
        'mem': nrm((BATCH, MEM_TOKENS, D), 1.0),
        'norm_mix': gain((DEPTH, D)),
        'norm_mem': gain((DEPTH, D)),
        'norm_ffn': gain((DEPTH, D)),
        'norm_final': gain((D,)),
        'rwkv_mu': jax.random.uniform(next(ks), (N_A, 6, D), f32),
        'rwkv_w_rkv': nrm((N_A, 3, D, D), D ** -0.5),
        'rwkv_w0': jax.random.uniform(next(ks), (N_A, D), f32, -4.0, 0.0),
        'rwkv_w1': nrm((N_A, D, DECAY_LORA), D ** -0.5),
        'rwkv_w2': nrm((N_A, DECAY_LORA, D), 0.5 * DECAY_LORA ** -0.5),
        'rwkv_a0': nrm((N_A, D), 0.1),
        'rwkv_a1': nrm((N_A, D, AAA_LORA), D ** -0.5),
        'rwkv_a2': nrm((N_A, AAA_LORA, D), 0.5 * AAA_LORA ** -0.5),
        'rwkv_g1': nrm((N_A, D, GATE_LORA), D ** -0.5),
        'rwkv_g2': nrm((N_A, GATE_LORA, D), GATE_LORA ** -0.5),
        'rwkv_k_k': 0.85 + nrm((N_A, D), 0.05),
        'rwkv_k_a': 1.0 + nrm((N_A, D), 0.05),
        'rwkv_r_k': nrm((N_A, H, N), 0.1),
        'rwkv_ln_g': gain((N_A, D)),
        'rwkv_ln_b': nrm((N_A, D), 0.01),
        'rwkv_w_o': nrm((N_A, D, D), D ** -0.5),
        'kv_norm': gain((D,)),
        'w_kv_shared': nrm((D, 2 * D), D ** -0.5),
        'sb_w_q': nrm((N_B, D, D), D ** -0.5),
        'sb_w_o': nrm((N_B, D, D), D ** -0.5),
        'mem_norm_kv': gain((DEPTH, D)),
        'mem_w_q': nrm((DEPTH, D, D), D ** -0.5),
        'mem_w_kv': nrm((DEPTH, D, 2 * D), D ** -0.5),
        'mem_w_o': nrm((DEPTH, D, D), D ** -0.5),
        'moe_w_group': nrm((DEPTH, D, N_GROUPS), D ** -0.5),
        'moe_b_group': nrm((DEPTH, N_GROUPS), 0.01),
        'moe_w_expert': nrm((DEPTH, D, N_EXPERTS), D ** -0.5),
        'moe_b_expert': nrm((DEPTH, N_EXPERTS), 0.01),
        'moe_w_gate': nrm((DEPTH, N_EXPERTS, D, D_EXPERT), D ** -0.5),
        'moe_w_up': nrm((DEPTH, N_EXPERTS, D, D_EXPERT), D ** -0.5),
        'moe_w_down': nrm((DEPTH, N_EXPERTS, D_EXPERT, D), D_EXPERT ** -0.5),
    }


def reference(x, mem, norm_mix, norm_mem, norm_ffn, norm_final,
              rwkv_mu, rwkv_w_rkv, rwkv_w0, rwkv_w1, rwkv_w2, rwkv_a0, rwkv_a1, rwkv_a2,
              rwkv_g1, rwkv_g2, rwkv_k_k, rwkv_k_a, rwkv_r_k, rwkv_ln_g, rwkv_ln_b, rwkv_w_o,
              kv_norm, w_kv_shared, sb_w_q, sb_w_o,
              mem_norm_kv, mem_w_q, mem_w_kv, mem_w_o,
              moe_w_group, moe_b_group, moe_w_expert, moe_b_expert, moe_w_gate, moe_w_up, moe_w_down):
    B, S, D = x.shape
    k_sh = v_sh = None
    for l in range(DEPTH):
        h = rmsnorm(x, norm_mix[l])
        if l < N_A:
            a = l
            x = x + rwkv7_time_mix(h, rwkv_mu[a], rwkv_w_rkv[a], rwkv_w0[a], rwkv_w1[a], rwkv_w2[a],
                                   rwkv_a0[a], rwkv_a1[a], rwkv_a2[a], rwkv_g1[a], rwkv_g2[a],
                                   rwkv_k_k[a], rwkv_k_a[a], rwkv_r_k[a], rwkv_ln_g[a], rwkv_ln_b[a], rwkv_w_o[a])
        else:
            b = l - N_A
            q = (h @ sb_w_q[b]).reshape(B, S, SB_HEADS, SB_HEAD)
            o = stick_breaking_attention(q, k_sh, v_sh)
            x = x + o.reshape(B, S, D) @ sb_w_o[b]
        x = x + memory_cross_attention(rmsnorm(x, norm_mem[l]), mem, mem_norm_kv[l],
                                       mem_w_q[l], mem_w_kv[l], mem_w_o[l])
        x = x + hierarchical_moe(rmsnorm(x, norm_ffn[l]), moe_w_group[l], moe_b_group[l],
                                 moe_w_expert[l], moe_b_expert[l], moe_w_gate[l], moe_w_up[l], moe_w_down[l])
        if l == N_A - 1:
            kv = (rmsnorm(x, kv_norm) @ w_kv_shared).reshape(B, S, 2, SB_HEADS, SB_HEAD)
            k_sh, v_sh = kv[:, :, 0], kv[:, :, 1]
    return rmsnorm(x, norm_final)
```

```python
import functools

import jax
import jax.numpy as jnp
from jax import lax
from jax.experimental import pallas as pl
from jax.experimental.pallas import tpu as pltpu

F32 = jnp.float32
BF16 = jnp.bfloat16
I32 = jnp.int32
HIGHEST = lax.Precision.HIGHEST

HEAD = 64
MEM_HEADS = 4
N_GROUPS = 4
EXPERTS_PER_GROUP = 8
N_EXPERTS = N_GROUPS * EXPERTS_PER_GROUP
NORM_EPS = 1e-6
RWKV_LN_EPS = 64e-5

LANES = 128
CHUNK = 64
SB_BLOCK = 128
MOE_BLOCK = 256
ROW_TILE = 256
SCAN_ROWS = 512
VMEM_LIMIT = 56 * 1024 * 1024


def _params(*sem):
    return pltpu.CompilerParams(dimension_semantics=sem, vmem_limit_bytes=VMEM_LIMIT)


def _dot(a, b):
    return jnp.dot(a, b, preferred_element_type=F32)


def _dot_nt(a, b):
    return lax.dot_general(a, b, (((1,), (1,)), ((), ())), preferred_element_type=F32)


def _dot_tn(a, b):
    return lax.dot_general(a, b, (((0,), (0,)), ((), ())), preferred_element_type=F32)


def _rms(x, g):
    ms = jnp.mean(x * x, axis=-1, keepdims=True)
    return x * lax.rsqrt(ms + NORM_EPS) * g


def _sigmoid(x):
    return 1.0 / (1.0 + jnp.exp(-x))


def _softplus(x):
    return jnp.maximum(x, 0.0) + jnp.log(1.0 + jnp.exp(-jnp.abs(x)))


def _full(shape):
    return pl.BlockSpec(shape, lambda *_: (0,) * len(shape))


def _rwkv_proj_body(x_ref, xp_ref, gn_ref, mu_ref, wr_ref, wk_ref, wv_ref, w1_ref, a1_ref, g1_ref,
                    w2_ref, a2_ref, g2_ref, vec_ref,
                    r_o, lw_o, k_o, v_o, kk_o, as_o, g_o, *, tiles_per_seq):
    i = pl.program_id(0)
    gn = gn_ref[...]
    h = _rms(x_ref[...], gn)
    hp = _rms(xp_ref[...], gn)[7:8, :]
    hp = jnp.where(lax.rem(i, tiles_per_seq) == 0, 0.0, hp)
    row = lax.broadcasted_iota(I32, (h.shape[0], 1), 0)
    hprev = jnp.where(row == 0, hp, pltpu.roll(h, 1, 0))
    xx = hprev - h
    mu = mu_ref[...]

    def lerp(j):
        return (h + xx * mu[j:j + 1, :]).astype(BF16)

    vec = vec_ref[...]
    r = _dot(lerp(0), wr_ref[...])
    k = _dot(lerp(1), wk_ref[...])
    v = _dot(lerp(2), wv_ref[...])
    w = vec[0:1] + _dot(jnp.tanh(_dot(lerp(3), w1_ref[...])).astype(BF16), w2_ref[...])
    w = -_softplus(-w) - 0.5
    a = _sigmoid(vec[1:2] + _dot(_dot(lerp(4), a1_ref[...]).astype(BF16), a2_ref[...]))
    g = _dot(_sigmoid(_dot(lerp(5), g1_ref[...])).astype(BF16), g2_ref[...])
    r_o[...] = r.astype(BF16)
    lw_o[...] = -jnp.exp(w)
    k_o[...] = (k * (1.0 + (a - 1.0) * vec[3:4])).astype(BF16)
    v_o[...] = v.astype(BF16)
    kk_o[...] = (k * vec[2:3]).astype(BF16)
    as_o[...] = a.astype(BF16)
    g_o[...] = g.astype(BF16)


def _rwkv_proj(x, seq, gn, mu, w_rkv, w0, w1, w2, a0, a1, a2, g1, g2, k_k, k_a):
    t, d = x.shape
    tm = ROW_TILE
    pad8 = lambda rows: jnp.concatenate([rows, jnp.zeros((8 - rows.shape[0], d), F32)], axis=0)
    vec = pad8(jnp.stack([w0, a0, k_k, k_a]))
    mu8 = pad8(mu)
    wts = [w_rkv[0], w_rkv[1], w_rkv[2], w1, a1, g1, w2, a2, g2]
    wts = [w.astype(BF16) for w in wts]
    row_spec = pl.BlockSpec((tm, d), lambda i: (i, 0))
    prev_spec = pl.BlockSpec((8, d), lambda i: (jnp.maximum(i * (tm // 8) - 1, 0), 0))
    in_specs = [row_spec, prev_spec, _full((1, d)), _full((8, d))]
    in_specs += [_full(w.shape) for w in wts] + [_full((8, d))]
    outs = [jax.ShapeDtypeStruct((t, d), dt) for dt in (BF16, F32, BF16, BF16, BF16, BF16, BF16)]
    return pl.pallas_call(
        functools.partial(_rwkv_proj_body, tiles_per_seq=seq // tm),
        grid=(t // tm,), in_specs=in_specs, out_specs=[row_spec] * 7, out_shape=outs,
        compiler_params=_params("parallel"), name="rwkv_proj",
    )(x, x, gn.reshape(1, d), mu8, *wts, vec)


def _cumsum_rows(x):
    n = x.shape[0]
    row = lax.broadcasted_iota(I32, (n, 1), 0)
    s = 1
    while s < n:
        x = x + jnp.where(row >= s, pltpu.roll(x, s, 0), 0.0)
        s *= 2
    return x


def _rwkv_scan_body(r_ref, lw_ref, k_ref, v_ref, kk_ref, as_ref, rk_ref, lg_ref, lb_ref, y_ref, state):
    c = CHUNK
    n = HEAD

    @pl.when(pl.program_id(2) == 0)
    def _():
        state[...] = jnp.zeros_like(state)

    row = lax.broadcasted_iota(I32, (c, c), 0)
    col = lax.broadcasted_iota(I32, (c, c), 1)
    strict = row > col
    incl = row >= col
    eye = (row == col).astype(F32)
    rk = rk_ref[...]
    lng = lg_ref[...]
    lnb = lb_ref[...]

    def chunk(ci, carry):
        rows = pl.ds(pl.multiple_of(ci * c, c), c)
        r = r_ref[rows, :].astype(F32)
        lw = lw_ref[rows, :]
        k = k_ref[rows, :].astype(F32)
        v = v_ref[rows, :].astype(F32)
        kkr = kk_ref[rows, :].astype(F32)
        asg = as_ref[rows, :].astype(F32)
        cum = _cumsum_rows(lw)
        g_in = jnp.exp(cum)
        g_ex = jnp.exp(cum - lw)
        g_inv = jnp.exp(-cum)
        ys = []
        for hh in range(r.shape[1] // n):
            sl = slice(hh * n, (hh + 1) * n)
            kh = kkr[:, sl]
            kkn = kh * lax.rsqrt(jnp.maximum(jnp.sum(kh * kh, axis=-1, keepdims=True), 1e-24))
            a_t = -kkn * g_ex[:, sl]
            b_t = kkn * asg[:, sl] * g_inv[:, sl]
            k_t = k[:, sl] * g_inv[:, sl]
            r_t = r[:, sl] * g_in[:, sl]
            vh = v[:, sl]
            vb = vh.astype(BF16)
            lhs = jnp.concatenate([a_t, r_t], axis=0).astype(BF16)
            rhs = jnp.concatenate([b_t, k_t], axis=0).astype(BF16)
            p = _dot_nt(lhs, rhs)
            a_ab = jnp.where(strict, p[:c, :c], 0.0)
            a_ak = jnp.where(strict, p[:c, c:], 0.0)
            a_rb = jnp.where(incl, p[c:, :c], 0.0)
            a_rk = jnp.where(incl, p[c:, c:], 0.0)
            tinv = eye + a_ab
            lp = a_ab
            s = 2
            while s < c:
                lpb = lp.astype(BF16)
                lp = _dot(lpb, lpb)
                tinv = tinv + _dot(tinv.astype(BF16), lp.astype(BF16))
                s *= 2
            x0 = _dot(a_ak.astype(BF16), vb)
            wu = _dot(tinv.astype(BF16), jnp.concatenate([a_t, x0], axis=1).astype(BF16))
            sb = state[hh].astype(BF16)
            u = _dot_nt(wu[:, :n].astype(BF16), sb) + wu[:, n:]
            uv = jnp.concatenate([u, vh], axis=0).astype(BF16)
            o = _dot_nt(r_t.astype(BF16), sb) + _dot(jnp.concatenate([a_rb, a_rk], axis=1).astype(BF16), uv)
            state[hh] = (state[hh] + _dot_tn(uv, rhs)) * g_in[c - 1:c, sl]
            mean = jnp.mean(o, axis=-1, keepdims=True)
            var = jnp.mean(jnp.square(o - mean), axis=-1, keepdims=True)
            y = (o - mean) * lax.rsqrt(var + RWKV_LN_EPS) * lng[:, sl] + lnb[:, sl]
            bonus = jnp.sum(r[:, sl] * k[:, sl] * rk[:, sl], axis=-1, keepdims=True) * vh
            ys.append(y + bonus)
        y_ref[rows, :] = jnp.concatenate(ys, axis=1).astype(y_ref.dtype)
        return carry

    lax.fori_loop(0, r_ref.shape[0] // c, chunk, 0)


def _rwkv_scan(r, lw, k, v, kk, asg, r_k, ln_g, ln_b, batch, seq):
    t, d = r.shape
    rb = SCAN_ROWS
    nc = seq // rb
    blk = pl.BlockSpec((rb, LANES), lambda b, hp, ci: (b * nc + ci, hp))
    vec = pl.BlockSpec((1, LANES), lambda b, hp, ci: (0, hp))
    return pl.pallas_call(
        _rwkv_scan_body,
        grid=(batch, d // LANES, nc),
        in_specs=[blk] * 6 + [vec] * 3, out_specs=blk,
        out_shape=jax.ShapeDtypeStruct((t, d), BF16),
        scratch_shapes=[pltpu.VMEM((LANES // HEAD, HEAD, HEAD), F32)],
        compiler_params=_params("parallel", "parallel", "arbitrary"), name="rwkv_scan",
    )(r, lw, k, v, kk, asg, r_k.reshape(1, d), ln_g.reshape(1, d), ln_b.reshape(1, d))


def _linear_res_body(x_ref, a_ref, *rest, gated):
    if gated:
        g_ref, w_ref, o_ref = rest
        a = a_ref[...] * g_ref[...]
    else:
        w_ref, o_ref = rest
        a = a_ref[...]
    o_ref[...] = x_ref[...] + _dot(a, w_ref[...])


def _linear_res(x, a, w, gate=None):
    t, d = x.shape
    tm = ROW_TILE
    row = pl.BlockSpec((tm, d), lambda i: (i, 0))
    ins = [x, a] + ([gate] if gate is not None else []) + [w.astype(BF16)]
    return pl.pallas_call(
        functools.partial(_linear_res_body, gated=gate is not None),
        grid=(t // tm,), in_specs=[row] * (len(ins) - 1) + [_full(w.shape)], out_specs=row,
        out_shape=jax.ShapeDtypeStruct((t, d), F32),
        compiler_params=_params("parallel"), name="linear_res",
    )(*ins)


def _norm_linear_body(x_ref, g_ref, w_ref, o_ref):
    o_ref[...] = _dot(_rms(x_ref[...], g_ref[...]).astype(BF16), w_ref[...]).astype(o_ref.dtype)


def _norm_linear(x, g, w, out_dtype):
    t, d = x.shape
    tm = ROW_TILE
    return pl.pallas_call(
        _norm_linear_body, grid=(t // tm,),
        in_specs=[pl.BlockSpec((tm, d), lambda i: (i, 0)), _full((1, d)), _full(w.shape)],
        out_specs=pl.BlockSpec((tm, w.shape[1]), lambda i: (i, 0)),
        out_shape=jax.ShapeDtypeStruct((t, w.shape[1]), out_dtype),
        compiler_params=_params("parallel"), name="norm_linear",
    )(x, g.reshape(1, d), w.astype(BF16))


def _mem_attn_body(x_ref, kv_ref, gn_ref, wq_ref, wo_ref, gf_ref, wr_ref, br_ref, x_o, xn_o, lg_o):
    x = x_ref[...]
    d = x.shape[1]
    hd = d // MEM_HEADS
    q = _dot(_rms(x, gn_ref[...]).astype(BF16), wq_ref[...]) * (hd ** -0.5)
    outs = []
    for hh in range(MEM_HEADS):
        qh = q[:, hh * hd:(hh + 1) * hd].astype(BF16)
        kh = kv_ref[:, hh * hd:(hh + 1) * hd]
        vh = kv_ref[:, d + hh * hd:d + (hh + 1) * hd]
        s = _dot_nt(qh, kh)
        p = jnp.exp(s - jnp.max(s, axis=-1, keepdims=True))
        o = _dot(p.astype(BF16), vh) / jnp.sum(p, axis=-1, keepdims=True)
        outs.append(o.astype(BF16))
    xnew = x + _dot(jnp.concatenate(outs, axis=1), wo_ref[...])
    x_o[...] = xnew
    xn = _rms(xnew, gf_ref[...])
    xn_o[...] = xn
    lg_o[...] = jnp.dot(xn, wr_ref[...], precision=HIGHEST, preferred_element_type=F32) + br_ref[...]


def _mem_attn(x, kv, seq, mem_tokens, gn, w_q, w_o, g_ffn, w_router, b_router):
    t, d = x.shape
    tm = ROW_TILE
    per_seq = seq // tm
    row = pl.BlockSpec((tm, d), lambda i: (i, 0))
    return pl.pallas_call(
        _mem_attn_body, grid=(t // tm,),
        in_specs=[row, pl.BlockSpec((mem_tokens, 2 * d), lambda i: (i // per_seq, 0)),
                  _full((1, d)), _full((d, d)), _full((d, d)), _full((1, d)),
                  _full((d, LANES)), _full((1, LANES))],
        out_specs=[row, row, pl.BlockSpec((tm, LANES), lambda i: (i, 0))],
        out_shape=[jax.ShapeDtypeStruct((t, d), F32), jax.ShapeDtypeStruct((t, d), F32),
                   jax.ShapeDtypeStruct((t, LANES), F32)],
        compiler_params=_params("parallel"), name="mem_attn",
    )(x, kv, gn.reshape(1, d), w_q.astype(BF16), w_o.astype(BF16), g_ffn.reshape(1, d), w_router, b_router)


def _route_body(lg_ref, idx_o, gate_o, cnt_o, carry):
    @pl.when(pl.program_id(0) == 0)
    def _():
        carry[...] = jnp.zeros_like(carry)

    lg = lg_ref[...]
    tm = lg.shape[0]
    lane = lax.broadcasted_iota(I32, lg.shape, 1)
    lanef = lane.astype(F32)
    big = 1e9
    ninf = -jnp.inf
    gl = jnp.where(lane < N_GROUPS, lg, ninf)
    gmax = jnp.max(gl, axis=-1, keepdims=True)
    g_w = 1.0 / jnp.sum(jnp.exp(gl - gmax), axis=-1, keepdims=True)
    grp = jnp.min(jnp.where(gl == gmax, lanef, big), axis=-1, keepdims=True).astype(I32)
    in_grp = (lane >= N_GROUPS) & (lane < N_GROUPS + N_EXPERTS) & (((lane - N_GROUPS) >> 3) == grp)
    el = jnp.where(in_grp, lg, ninf)
    m1 = jnp.max(el, axis=-1, keepdims=True)
    i1 = jnp.min(jnp.where(el == m1, lanef, big), axis=-1, keepdims=True)
    el2 = jnp.where(lanef == i1, ninf, el)
    m2 = jnp.max(el2, axis=-1, keepdims=True)
    i2 = jnp.min(jnp.where(el2 == m2, lanef, big), axis=-1, keepdims=True)
    tt = jnp.exp(m2 - m1)
    w1 = 1.0 / (1.0 + tt)
    w2 = tt / (1.0 + tt)
    hit1 = lanef == i1
    hit2 = lanef == i2
    oh = jnp.where(hit1 | hit2, 1.0, 0.0)
    r_i = lax.broadcasted_iota(I32, (tm, tm), 0)
    c_i = lax.broadcasted_iota(I32, (tm, tm), 1)
    tri = jnp.where(r_i > c_i, 1.0, 0.0).astype(BF16)
    pos = _dot(tri, oh.astype(BF16)) + carry[...]
    pos1 = jnp.sum(jnp.where(hit1, pos, 0.0), axis=-1, keepdims=True)
    pos2 = jnp.sum(jnp.where(hit2, pos, 0.0), axis=-1, keepdims=True)
    carry[...] = carry[...] + jnp.sum(oh, axis=0, keepdims=True)
    cnt_o[...] = carry[...]
    idx = jnp.where(lane == 0, i1 - N_GROUPS, jnp.where(lane == 1, i2 - N_GROUPS,
          jnp.where(lane == 2, pos1, jnp.where(lane == 3, pos2, 0.0))))
    idx_o[...] = idx.astype(I32)
    gate_o[...] = jnp.where(lane == 0, g_w * w1, jnp.where(lane == 1, g_w * w2, 0.0))


def _route(logits):
    t = logits.shape[0]
    tm = ROW_TILE
    row = pl.BlockSpec((tm, LANES), lambda i: (i, 0))
    return pl.pallas_call(
        _route_body, grid=(t // tm,), in_specs=[row],
        out_specs=[row, row, _full((1, LANES))],
        out_shape=[jax.ShapeDtypeStruct((t, LANES), I32), jax.ShapeDtypeStruct((t, LANES), F32),
                   jax.ShapeDtypeStruct((1, LANES), F32)],
        scratch_shapes=[pltpu.VMEM((1, LANES), F32)],
        compiler_params=_params("arbitrary"), name="moe_route",
    )(logits)


def _row_copy(src, s, dst, d, sem):
    return pltpu.make_async_copy(src.at[pl.ds(s, 1)], dst.at[pl.ds(d, 1)], sem)


def _dispatch_body(dest_ref, xn_ref, xs_in, xs_out, sem):
    del xs_in
    tm = xn_ref.shape[0]

    def issue(r, c):
        _row_copy(xn_ref, r, xs_out, dest_ref[0, 0, r], sem).start()
        _row_copy(xn_ref, r, xs_out, dest_ref[0, 0, tm + r], sem).start()
        return c

    def drain(r, c):
        _row_copy(xn_ref, 0, xs_out, 0, sem).wait()
        return c

    lax.fori_loop(0, tm, issue, 0)
    lax.fori_loop(0, 2 * tm, drain, 0)


def _dispatch(xn, dest, cap):
    t, d = xn.shape
    tm = ROW_TILE
    return pl.pallas_call(
        _dispatch_body, grid=(t // tm,),
        in_specs=[pl.BlockSpec((1, 1, 2 * tm), lambda i: (i, 0, 0), memory_space=pltpu.SMEM),
                  pl.BlockSpec((tm, d), lambda i: (i, 0)),
                  pl.BlockSpec(memory_space=pl.ANY)],
        out_specs=pl.BlockSpec(memory_space=pl.ANY),
        out_shape=jax.ShapeDtypeStruct((cap, d), F32),
        scratch_shapes=[pltpu.SemaphoreType.DMA],
        input_output_aliases={2: 0},
        compiler_params=_params("arbitrary"), name="moe_dispatch",
    )(dest, xn, jnp.zeros((cap, d), F32))


def _experts_body(be_ref, nb_ref, xs_ref, wgu_ref, wd_ref, ys_ref):
    del be_ref
    live = pl.program_id(0) < nb_ref[0]

    @pl.when(live)
    def _():
        gu = _dot(xs_ref[...].astype(BF16), wgu_ref[0])
        f = gu.shape[1] // 2
        gt = gu[:, :f]
        hid = gt * _sigmoid(gt) * gu[:, f:]
        ys_ref[...] = _dot(hid.astype(BF16), wd_ref[0])

    @pl.when(jnp.logical_not(live))
    def _():
        ys_ref[...] = jnp.zeros_like(ys_ref)


def _experts(xs, block_expert, n_live, w_gate_up, w_down):
    cap, d = xs.shape
    f2 = w_gate_up.shape[2]
    blk = MOE_BLOCK
    grid_spec = pltpu.PrefetchScalarGridSpec(
        num_scalar_prefetch=2, grid=(cap // blk,),
        in_specs=[pl.BlockSpec((blk, d), lambda i, be, nb: (i, 0)),
                  pl.BlockSpec((1, d, f2), lambda i, be, nb: (be[i], 0, 0)),
                  pl.BlockSpec((1, f2 // 2, d), lambda i, be, nb: (be[i], 0, 0))],
        out_specs=pl.BlockSpec((blk, d), lambda i, be, nb: (i, 0)))
    return pl.pallas_call(
        _experts_body, grid_spec=grid_spec, out_shape=jax.ShapeDtypeStruct((cap, d), F32),
        compiler_params=_params("arbitrary"), name="moe_experts",
    )(block_expert, n_live, xs, w_gate_up, w_down)


def _combine_body(dest_ref, x_ref, gate_ref, ys_ref, *rest, n_proj, final_norm):
    norm_refs = rest[:n_proj + (1 if final_norm else 0)]
    w_refs = rest[len(norm_refs):len(norm_refs) + n_proj]
    outs = rest[len(norm_refs) + n_proj:-3]
    buf1, buf2, sem = rest[-3:]
    tm = x_ref.shape[0]

    def issue(r, c):
        _row_copy(ys_ref, dest_ref[0, 0, r], buf1, r, sem).start()
        _row_copy(ys_ref, dest_ref[0, 0, tm + r], buf2, r, sem).start()
        return c

    def drain(r, c):
        _row_copy(ys_ref, 0, buf1, 0, sem).wait()
        return c

    lax.fori_loop(0, tm, issue, 0)
    lax.fori_loop(0, 2 * tm, drain, 0)
    gate = gate_ref[...]
    x = x_ref[...] + gate[:, 0:1] * buf1[...] + gate[:, 1:2] * buf2[...]
    if final_norm:
        outs[0][...] = _rms(x, norm_refs[0][...])
        return
    outs[0][...] = x
    for j in range(n_proj):
        outs[1 + j][...] = _dot(_rms(x, norm_refs[j][...]).astype(BF16), w_refs[j][...]).astype(outs[1 + j].dtype)


def _combine(x, ys, dest, gates, norms, weights, final_norm=False):
    t, d = x.shape
    tm = ROW_TILE
    row = pl.BlockSpec((tm, d), lambda i: (i, 0))
    in_specs = [pl.BlockSpec((1, 1, 2 * tm), lambda i: (i, 0, 0), memory_space=pltpu.SMEM),
                row, pl.BlockSpec((tm, LANES), lambda i: (i, 0)), pl.BlockSpec(memory_space=pl.ANY)]
    in_specs += [_full((1, d))] * len(norms) + [_full(w.shape) for w in weights]
    out_specs = [row] + [pl.BlockSpec((tm, w.shape[1]), lambda i: (i, 0)) for w in weights]
    out_shape = [jax.ShapeDtypeStruct((t, d), F32)]
    out_shape += [jax.ShapeDtypeStruct((t, w.shape[1]), BF16) for w in weights]
    return pl.pallas_call(
        functools.partial(_combine_body, n_proj=len(weights), final_norm=final_norm),
        grid=(t // tm,), in_specs=in_specs, out_specs=out_specs, out_shape=out_shape,
        scratch_shapes=[pltpu.VMEM((tm, d), F32), pltpu.VMEM((tm, d), F32), pltpu.SemaphoreType.DMA],
        compiler_params=_params("arbitrary"), name="moe_combine",
    )(dest, x, gates, ys, *[g.reshape(1, d) for g in norms], *[w.astype(BF16) for w in weights])


def _moe(x, xn, logits, w_gate, w_up, w_down, norms, weights, final_norm=False):
    t, d = x.shape
    blk = MOE_BLOCK
    tm = ROW_TILE
    idx, gates, counts = _route(logits)
    counts = counts[0, N_GROUPS:N_GROUPS + N_EXPERTS].astype(I32)
    padded = (counts + blk - 1) // blk * blk
    pend = jnp.cumsum(padded)
    pstart = pend - padded
    cap = 2 * t + N_EXPERTS * blk
    n_blocks = cap // blk
    block_expert = jnp.minimum(
        jnp.searchsorted(pend, jnp.arange(n_blocks, dtype=I32) * blk, side="right"), N_EXPERTS - 1).astype(I32)
    n_live = (pend[-1:] // blk).astype(I32)
    dest = jnp.take(pstart, idx[:, 0:2], axis=0) + idx[:, 2:4]
    dest = dest.reshape(t // tm, tm, 2).transpose(0, 2, 1).reshape(t // tm, 1, 2 * tm)
    xs = _dispatch(xn, dest, cap)
    w_gu = jnp.concatenate([w_gate, w_up], axis=2).astype(BF16)
    ys = _experts(xs, block_expert, n_live, w_gu, w_down.astype(BF16))
    return _combine(x, ys, dest, gates, norms, weights, final_norm)


def _sb_attn_body(q_ref, k_ref, v_ref, o_ref):
    i = pl.program_id(2)
    blk = SB_BLOCK
    row = lax.broadcasted_iota(I32, (blk, blk), 0)
    col = lax.broadcasted_iota(I32, (blk, blk), 1)
    suffix = jnp.where(row >= col, 1.0, 0.0).astype(BF16)
    outs = []
    for hh in range(q_ref.shape[1] // HEAD):
        sl = slice(hh * HEAD, (hh + 1) * HEAD)
        qh = (q_ref[:, sl].astype(F32) * (HEAD ** -0.5)).astype(BF16)

        def key_block(step, carry, sl=sl, qh=qh):
            acc, rest = carry
            j = i - step
            rows = pl.ds(pl.multiple_of(j * blk, blk), blk)
            z = _dot_nt(qh, k_ref[rows, sl])
            causal = (j * blk + col) < (i * blk + row)
            lk = jnp.where(causal, -_softplus(z), 0.0)
            hi = lk.astype(BF16)
            lo = (lk - hi.astype(F32)).astype(BF16)
            sfx = _dot(hi, suffix) + _dot(lo, suffix)
            attn = jnp.where(causal, jnp.exp(z + sfx + rest), 0.0)
            acc = acc + _dot(attn.astype(BF16), v_ref[rows, sl])
            return acc, rest + sfx[:, 0:1]

        acc, _ = lax.fori_loop(0, i + 1, key_block, (jnp.zeros((blk, HEAD), F32), jnp.zeros((blk, 1), F32)))
        outs.append(acc)
    o_ref[...] = jnp.concatenate(outs, axis=1).astype(o_ref.dtype)


def _sb_attn(q, kv, batch, seq):
    t, d = q.shape
    nq = seq // SB_BLOCK
    hp = d // LANES
    return pl.pallas_call(
        _sb_attn_body, grid=(batch, hp, nq),
        in_specs=[pl.BlockSpec((SB_BLOCK, LANES), lambda b, h, i: (b * nq + i, h)),
                  pl.BlockSpec((seq, LANES), lambda b, h, i: (b, h)),
                  pl.BlockSpec((seq, LANES), lambda b, h, i: (b, hp + h))],
        out_specs=pl.BlockSpec((SB_BLOCK, LANES), lambda b, h, i: (b * nq + i, h)),
        out_shape=jax.ShapeDtypeStruct((t, d), BF16),
        compiler_params=_params("parallel", "parallel", "arbitrary"), name="sb_attn",
    )(q, kv, kv)


def _router_weights(w_group, b_group, w_expert, b_expert):
    d = w_group.shape[0]
    pad = LANES - N_GROUPS - N_EXPERTS
    w = jnp.concatenate([w_group, w_expert, jnp.zeros((d, pad), F32)], axis=1)
    b = jnp.concatenate([b_group, b_expert, jnp.zeros((pad,), F32)]).reshape(1, LANES)
    return w, b


def kernel(x, mem, norm_mix, norm_mem, norm_ffn, norm_final, rwkv_mu, rwkv_w_rkv, rwkv_w0, rwkv_w1, rwkv_w2, rwkv_a0, rwkv_a1, rwkv_a2, rwkv_g1, rwkv_g2, rwkv_k_k, rwkv_k_a, rwkv_r_k, rwkv_ln_g, rwkv_ln_b, rwkv_w_o, kv_norm, w_kv_shared, sb_w_q, sb_w_o, mem_norm_kv, mem_w_q, mem_w_kv, mem_w_o, moe_w_group, moe_b_group, moe_w_expert, moe_b_expert, moe_w_gate, moe_w_up, moe_w_down):
    batch, seq, d = x.shape
    depth = norm_mix.shape[0]
    n_a = rwkv_mu.shape[0]
    mem_tokens = mem.shape[1]
    xs = x.reshape(batch * seq, d)
    memf = mem.reshape(batch * mem_tokens, d)
    q = kv = None
    for l in range(depth):
        if l < n_a:
            r, lw, k, v, kk, asg, g = _rwkv_proj(
                xs, seq, norm_mix[l], rwkv_mu[l], rwkv_w_rkv[l], rwkv_w0[l], rwkv_w1[l], rwkv_w2[l],
                rwkv_a0[l], rwkv_a1[l], rwkv_a2[l], rwkv_g1[l], rwkv_g2[l], rwkv_k_k[l], rwkv_k_a[l])
            y = _rwkv_scan(r, lw, k, v, kk, asg, rwkv_r_k[l], rwkv_ln_g[l], rwkv_ln_b[l], batch, seq)
            xs = _linear_res(xs, y, rwkv_w_o[l], gate=g)
        else:
            o = _sb_attn(q, kv, batch, seq)
            xs = _linear_res(xs, o, sb_w_o[l - n_a])
        mkv = _norm_linear(memf, mem_norm_kv[l], mem_w_kv[l], BF16)
        w_router, b_router = _router_weights(moe_w_group[l], moe_b_group[l], moe_w_expert[l], moe_b_expert[l])
        xs, xn, logits = _mem_attn(xs, mkv, seq, mem_tokens, norm_mem[l], mem_w_q[l], mem_w_o[l],
                                   norm_ffn[l], w_router, b_router)
        last = l == depth - 1
        if last:
            norms, weights = [norm_final], []
        elif l + 1 < n_a:
            norms, weights = [], []
        else:
            norms, weights = [norm_mix[l + 1]], [sb_w_q[l + 1 - n_a]]
            if l == n_a - 1:
                norms, weights = norms + [kv_norm], weights + [w_kv_shared]
        res = _moe(xs, xn, logits, moe_w_gate[l], moe_w_up[l], moe_w_down[l], norms, weights, final_norm=last)
        if last:
            return res[0].reshape(batch, seq, d)
        xs = res[0]
        if weights:
            q = res[1]
            if l == n_a - 1:
                kv = res[2]
```

```python
import functools

import jax
import jax.numpy as jnp
from jax import lax
from jax.experimental import pallas as pl
from jax.experimental.pallas import tpu as pltpu

F32 = jnp.float32
BF16 = jnp.bfloat16
I32 = jnp.int32
HIGHEST = lax.Precision.HIGHEST

HEAD = 64
MEM_HEADS = 4
N_GROUPS = 4
EXPERTS_PER_GROUP = 8
N_EXPERTS = N_GROUPS * EXPERTS_PER_GROUP
NORM_EPS = 1e-6
RWKV_LN_EPS = 64e-5

LANES = 128
CHUNK = 64
SB_BLOCK = 128
SB_WINDOW = 256
SB_LANES = 256
SB_ZERO_LOG = -110.0
MOE_BLOCK = 256
ROW_TILE = 256
SCAN_ROWS = 512
SCAN_LANES = 256
SCAN_GROUP = 2
VMEM_LIMIT = 56 * 1024 * 1024


def _params(*sem):
    return pltpu.CompilerParams(dimension_semantics=sem, vmem_limit_bytes=VMEM_LIMIT)


def _dot(a, b):
    return jnp.dot(a, b, preferred_element_type=F32)


def _dot_nt(a, b):
    return lax.dot_general(a, b, (((1,), (1,)), ((), ())), preferred_element_type=F32)


def _dot_tn(a, b):
    return lax.dot_general(a, b, (((0,), (0,)), ((), ())), preferred_element_type=F32)


def _rms(x, g):
    ms = jnp.mean(x * x, axis=-1, keepdims=True)
    return x * lax.rsqrt(ms + NORM_EPS) * g


def _sigmoid(x):
    return 1.0 / (1.0 + jnp.exp(-x))


def _softplus(x):
    return jnp.maximum(x, 0.0) + jnp.log(1.0 + jnp.exp(-jnp.abs(x)))


def _full(shape):
    return pl.BlockSpec(shape, lambda *_: (0,) * len(shape))


def _rwkv_proj_body(x_ref, xp_ref, gn_ref, mu_ref, wr_ref, wk_ref, wv_ref, w1_ref, a1_ref, g1_ref,
                    w2_ref, a2_ref, g2_ref, vec_ref,
                    r_o, lw_o, k_o, v_o, kk_o, as_o, g_o, *, tiles_per_seq):
    i = pl.program_id(0)
    gn = gn_ref[...]
    h = _rms(x_ref[...], gn)
    hp = _rms(xp_ref[...], gn)[7:8, :]
    hp = jnp.where(lax.rem(i, tiles_per_seq) == 0, 0.0, hp)
    row = lax.broadcasted_iota(I32, (h.shape[0], 1), 0)
    hprev = jnp.where(row == 0, hp, pltpu.roll(h, 1, 0))
    xx = hprev - h
    mu = mu_ref[...]

    def lerp(j):
        return (h + xx * mu[j:j + 1, :]).astype(BF16)

    vec = vec_ref[...]
    r = _dot(lerp(0), wr_ref[...])
    k = _dot(lerp(1), wk_ref[...])
    v = _dot(lerp(2), wv_ref[...])
    w = vec[0:1] + _dot(jnp.tanh(_dot(lerp(3), w1_ref[...])).astype(BF16), w2_ref[...])
    w = -_softplus(-w) - 0.5
    a = _sigmoid(vec[1:2] + _dot(_dot(lerp(4), a1_ref[...]).astype(BF16), a2_ref[...]))
    g = _dot(_sigmoid(_dot(lerp(5), g1_ref[...])).astype(BF16), g2_ref[...])
    r_o[...] = r.astype(BF16)
    lw_o[...] = -jnp.exp(w)
    k_o[...] = (k * (1.0 + (a - 1.0) * vec[3:4])).astype(BF16)
    v_o[...] = v.astype(BF16)
    kk_o[...] = (k * vec[2:3]).astype(BF16)
    as_o[...] = a.astype(BF16)
    g_o[...] = g.astype(BF16)


def _rwkv_proj(x, seq, gn, mu, w_rkv, w0, w1, w2, a0, a1, a2, g1, g2, k_k, k_a):
    t, d = x.shape
    tm = ROW_TILE
    pad8 = lambda rows: jnp.concatenate([rows, jnp.zeros((8 - rows.shape[0], d), F32)], axis=0)
    vec = pad8(jnp.stack([w0, a0, k_k, k_a]))
    mu8 = pad8(mu)
    wts = [w_rkv[0], w_rkv[1], w_rkv[2], w1, a1, g1, w2, a2, g2]
    wts = [w.astype(BF16) for w in wts]
    row_spec = pl.BlockSpec((tm, d), lambda i: (i, 0))
    prev_spec = pl.BlockSpec((8, d), lambda i: (jnp.maximum(i * (tm // 8) - 1, 0), 0))
    in_specs = [row_spec, prev_spec, _full((1, d)), _full((8, d))]
    in_specs += [_full(w.shape) for w in wts] + [_full((8, d))]
    outs = [jax.ShapeDtypeStruct((t, d), dt) for dt in (BF16, F32, BF16, BF16, BF16, BF16, BF16)]
    return pl.pallas_call(
        functools.partial(_rwkv_proj_body, tiles_per_seq=seq // tm),
        grid=(t // tm,), in_specs=in_specs, out_specs=[row_spec] * 7, out_shape=outs,
        compiler_params=_params("parallel"), name="rwkv_proj",
    )(x, x, gn.reshape(1, d), mu8, *wts, vec)


def _cumsum_rows(x):
    n = x.shape[0]
    row = lax.broadcasted_iota(I32, (n, 1), 0)
    s = 1
    while s < n:
        x = x + jnp.where(row >= s, pltpu.roll(x, s, 0), 0.0)
        s *= 2
    return x


def _rwkv_scan_body(r_ref, lw_ref, k_ref, v_ref, kk_ref, as_ref, rk_ref, lg_ref, lb_ref, y_ref,
                    state, wr_s, u0_s, a_s, rhs_s, g_s):
    c = CHUNK
    n = HEAD
    nh = r_ref.shape[1] // n
    n_chunks = r_ref.shape[0] // c

    @pl.when(pl.program_id(2) == 0)
    def _():
        state[...] = jnp.zeros_like(state)

    row = lax.broadcasted_iota(I32, (c, c), 0)
    col = lax.broadcasted_iota(I32, (c, c), 1)
    strict = row > col
    eye = (row == col).astype(F32)
    row2 = lax.broadcasted_iota(I32, (c, 2 * c), 0)
    col2 = lax.broadcasted_iota(I32, (c, 2 * c), 1)
    incl = row2 >= (col2 & (c - 1))
    lanes = [slice(hh * n, (hh + 1) * n) for hh in range(nh)]

    def local(cj, carry):
        units, a_ts, r_ts, lhss, rhss, vrefs = [], [], [], [], [], []
        for sub in range(SCAN_GROUP):
            ci = cj * SCAN_GROUP + sub
            rows = pl.ds(pl.multiple_of(ci * c, c), c)
            r = r_ref[rows, :].astype(F32)
            lw = lw_ref[rows, :]
            k = k_ref[rows, :].astype(F32)
            kkr = kk_ref[rows, :].astype(F32)
            asg = as_ref[rows, :].astype(F32)
            cum = _cumsum_rows(lw)
            g_in = jnp.exp(cum)
            g_ex = jnp.exp(cum - lw)
            g_inv = jnp.exp(-cum)
            g_s[ci] = g_in[c - 1:c, :]
            for hh, sl in enumerate(lanes):
                kh = kkr[:, sl]
                kkn = kh * lax.rsqrt(jnp.maximum(jnp.sum(kh * kh, axis=-1, keepdims=True), 1e-24))
                a_t = -kkn * g_ex[:, sl]
                b_t = kkn * asg[:, sl] * g_inv[:, sl]
                k_t = k[:, sl] * g_inv[:, sl]
                r_t = (r[:, sl] * g_in[:, sl]).astype(BF16)
                units.append((ci, hh))
                a_ts.append(a_t)
                r_ts.append(r_t)
                lhss.append(jnp.concatenate([a_t.astype(BF16), r_t], axis=0))
                rhss.append(jnp.concatenate([b_t, k_t], axis=0).astype(BF16))
                vrefs.append((rows, sl))
        ps = [_dot_nt(lhs, rhs) for lhs, rhs in zip(lhss, rhss)]
        lps = [jnp.where(strict, p[:c, :c], 0.0) for p in ps]
        x0s = [_dot(jnp.where(strict, p[:c, c:], 0.0).astype(BF16), v_ref[rs, sl]) for p, (rs, sl) in zip(ps, vrefs)]
        for (ci, hh), p, rhs in zip(units, ps, rhss):
            a_s[ci, hh] = jnp.where(incl, p[c:, :], 0.0).astype(BF16)
            rhs_s[ci, hh] = rhs
        tinvs = [eye + lp for lp in lps]
        s = 2
        while s < c:
            lpbs = [lp.astype(BF16) for lp in lps]
            lps = [_dot(lpb, lpb) for lpb in lpbs]
            tinvs = [t + _dot(t.astype(BF16), lp.astype(BF16)) for t, lp in zip(tinvs, lps)]
            s *= 2
        wus = [_dot(t.astype(BF16), jnp.concatenate([a_t, x0], axis=1).astype(BF16))
               for t, a_t, x0 in zip(tinvs, a_ts, x0s)]
        for (ci, hh), wu, r_t in zip(units, wus, r_ts):
            wr_s[ci, hh] = jnp.concatenate([wu[:, :n].astype(BF16), r_t], axis=0)
            u0_s[ci, hh] = wu[:, n:]
        return carry

    lax.fori_loop(0, n_chunks // SCAN_GROUP, local, 0)

    rk = rk_ref[...]
    lng = lg_ref[...]
    lnb = lb_ref[...]

    def sweep(ci, carry):
        rows = pl.ds(pl.multiple_of(ci * c, c), c)
        r = r_ref[rows, :].astype(F32)
        k = k_ref[rows, :].astype(F32)
        v = v_ref[rows, :].astype(F32)
        g_last = g_s[ci]
        heads = range(nh)
        sbs = [state[hh].astype(BF16) for hh in heads]
        wss = [_dot_nt(wr_s[ci, hh], sbs[hh]) for hh in heads]
        uvs = [jnp.concatenate([wss[hh][:c] + u0_s[ci, hh], v[:, lanes[hh]]], axis=0).astype(BF16)
               for hh in heads]
        os_ = [wss[hh][c:] + _dot(a_s[ci, hh], uvs[hh]) for hh in heads]
        for hh in heads:
            state[hh] = (state[hh] + _dot_tn(uvs[hh], rhs_s[ci, hh])) * g_last[:, lanes[hh]]
        ys = []
        for hh in heads:
            sl = lanes[hh]
            o = os_[hh]
            mean = jnp.mean(o, axis=-1, keepdims=True)
            var = jnp.mean(jnp.square(o - mean), axis=-1, keepdims=True)
            y = (o - mean) * lax.rsqrt(var + RWKV_LN_EPS) * lng[:, sl] + lnb[:, sl]
            bonus = jnp.sum(r[:, sl] * k[:, sl] * rk[:, sl], axis=-1, keepdims=True) * v[:, sl]
            ys.append(y + bonus)
        y_ref[rows, :] = jnp.concatenate(ys, axis=1).astype(y_ref.dtype)
        return carry

    lax.fori_loop(0, n_chunks, sweep, 0)


def _rwkv_scan(r, lw, k, v, kk, asg, r_k, ln_g, ln_b, batch, seq):
    t, d = r.shape
    rb = SCAN_ROWS
    w = SCAN_LANES
    nc = seq // rb
    nh = w // HEAD
    n_chunks = rb // CHUNK
    blk = pl.BlockSpec((rb, w), lambda b, hp, ci: (b * nc + ci, hp))
    vec = pl.BlockSpec((1, w), lambda b, hp, ci: (0, hp))
    scratch = [pltpu.VMEM((nh, HEAD, HEAD), F32),
               pltpu.VMEM((n_chunks, nh, 2 * CHUNK, HEAD), BF16),
               pltpu.VMEM((n_chunks, nh, CHUNK, HEAD), F32),
               pltpu.VMEM((n_chunks, nh, CHUNK, 2 * CHUNK), BF16),
               pltpu.VMEM((n_chunks, nh, 2 * CHUNK, HEAD), BF16),
               pltpu.VMEM((n_chunks, 1, w), F32)]
    return pl.pallas_call(
        _rwkv_scan_body,
        grid=(batch, d // w, nc),
        in_specs=[blk] * 6 + [vec] * 3, out_specs=blk,
        out_shape=jax.ShapeDtypeStruct((t, d), BF16),
        scratch_shapes=scratch,
        compiler_params=_params("parallel", "parallel", "arbitrary"), name="rwkv_scan",
    )(r, lw, k, v, kk, asg, r_k.reshape(1, d), ln_g.reshape(1, d), ln_b.reshape(1, d))


def _linear_res_body(x_ref, a_ref, *rest, gated):
    if gated:
        g_ref, w_ref, o_ref = rest
        a = a_ref[...] * g_ref[...]
    else:
        w_ref, o_ref = rest
        a = a_ref[...]
    o_ref[...] = x_ref[...] + _dot(a, w_ref[...])


def _linear_res(x, a, w, gate=None):
    t, d = x.shape
    tm = ROW_TILE
    row = pl.BlockSpec((tm, d), lambda i: (i, 0))
    ins = [x, a] + ([gate] if gate is not None else []) + [w.astype(BF16)]
    return pl.pallas_call(
        functools.partial(_linear_res_body, gated=gate is not None),
        grid=(t // tm,), in_specs=[row] * (len(ins) - 1) + [_full(w.shape)], out_specs=row,
        out_shape=jax.ShapeDtypeStruct((t, d), F32),
        compiler_params=_params("parallel"), name="linear_res",
    )(*ins)


def _norm_linear_body(x_ref, g_ref, w_ref, o_ref):
    o_ref[...] = _dot(_rms(x_ref[...], g_ref[...]).astype(BF16), w_ref[...]).astype(o_ref.dtype)


def _norm_linear(x, g, w, out_dtype):
    t, d = x.shape
    tm = ROW_TILE
    return pl.pallas_call(
        _norm_linear_body, grid=(t // tm,),
        in_specs=[pl.BlockSpec((tm, d), lambda i: (i, 0)), _full((1, d)), _full(w.shape)],
        out_specs=pl.BlockSpec((tm, w.shape[1]), lambda i: (i, 0)),
        out_shape=jax.ShapeDtypeStruct((t, w.shape[1]), out_dtype),
        compiler_params=_params("parallel"), name="norm_linear",
    )(x, g.reshape(1, d), w.astype(BF16))


def _mem_attn_body(x_ref, kv_ref, gn_ref, wq_ref, wo_ref, gf_ref, wr_ref, br_ref, x_o, xn_o, lg_o):
    x = x_ref[...]
    d = x.shape[1]
    hd = d // MEM_HEADS
    q = _dot(_rms(x, gn_ref[...]).astype(BF16), wq_ref[...]) * (hd ** -0.5)
    outs = []
    for hh in range(MEM_HEADS):
        qh = q[:, hh * hd:(hh + 1) * hd].astype(BF16)
        kh = kv_ref[:, hh * hd:(hh + 1) * hd]
        vh = kv_ref[:, d + hh * hd:d + (hh + 1) * hd]
        s = _dot_nt(qh, kh)
        p = jnp.exp(s - jnp.max(s, axis=-1, keepdims=True))
        o = _dot(p.astype(BF16), vh) / jnp.sum(p, axis=-1, keepdims=True)
        outs.append(o.astype(BF16))
    xnew = x + _dot(jnp.concatenate(outs, axis=1), wo_ref[...])
    x_o[...] = xnew
    xn = _rms(xnew, gf_ref[...])
    xn_o[...] = xn
    lg_o[...] = jnp.dot(xn, wr_ref[...], precision=HIGHEST, preferred_element_type=F32) + br_ref[...]


def _mem_attn(x, kv, seq, mem_tokens, gn, w_q, w_o, g_ffn, w_router, b_router):
    t, d = x.shape
    tm = ROW_TILE
    per_seq = seq // tm
    row = pl.BlockSpec((tm, d), lambda i: (i, 0))
    return pl.pallas_call(
        _mem_attn_body, grid=(t // tm,),
        in_specs=[row, pl.BlockSpec((mem_tokens, 2 * d), lambda i: (i // per_seq, 0)),
                  _full((1, d)), _full((d, d)), _full((d, d)), _full((1, d)),
                  _full((d, LANES)), _full((1, LANES))],
        out_specs=[row, row, pl.BlockSpec((tm, LANES), lambda i: (i, 0))],
        out_shape=[jax.ShapeDtypeStruct((t, d), F32), jax.ShapeDtypeStruct((t, d), F32),
                   jax.ShapeDtypeStruct((t, LANES), F32)],
        compiler_params=_params("parallel"), name="mem_attn",
    )(x, kv, gn.reshape(1, d), w_q.astype(BF16), w_o.astype(BF16), g_ffn.reshape(1, d), w_router, b_router)


def _route_body(lg_ref, idx_o, gate_o, cnt_o, carry):
    @pl.when(pl.program_id(0) == 0)
    def _():
        carry[...] = jnp.zeros_like(carry)

    lg = lg_ref[...]
    tm = lg.shape[0]
    lane = lax.broadcasted_iota(I32, lg.shape, 1)
    lanef = lane.astype(F32)
    big = 1e9
    ninf = -jnp.inf
    gl = jnp.where(lane < N_GROUPS, lg, ninf)
    gmax = jnp.max(gl, axis=-1, keepdims=True)
    g_w = 1.0 / jnp.sum(jnp.exp(gl - gmax), axis=-1, keepdims=True)
    grp = jnp.min(jnp.where(gl == gmax, lanef, big), axis=-1, keepdims=True).astype(I32)
    in_grp = (lane >= N_GROUPS) & (lane < N_GROUPS + N_EXPERTS) & (((lane - N_GROUPS) >> 3) == grp)
    el = jnp.where(in_grp, lg, ninf)
    m1 = jnp.max(el, axis=-1, keepdims=True)
    i1 = jnp.min(jnp.where(el == m1, lanef, big), axis=-1, keepdims=True)
    el2 = jnp.where(lanef == i1, ninf, el)
    m2 = jnp.max(el2, axis=-1, keepdims=True)
    i2 = jnp.min(jnp.where(el2 == m2, lanef, big), axis=-1, keepdims=True)
    tt = jnp.exp(m2 - m1)
    w1 = 1.0 / (1.0 + tt)
    w2 = tt / (1.0 + tt)
    hit1 = lanef == i1
    hit2 = lanef == i2
    oh = jnp.where(hit1 | hit2, 1.0, 0.0)
    r_i = lax.broadcasted_iota(I32, (tm, tm), 0)
    c_i = lax.broadcasted_iota(I32, (tm, tm), 1)
    tri = jnp.where(r_i > c_i, 1.0, 0.0).astype(BF16)
    pos = _dot(tri, oh.astype(BF16)) + carry[...]
    pos1 = jnp.sum(jnp.where(hit1, pos, 0.0), axis=-1, keepdims=True)
    pos2 = jnp.sum(jnp.where(hit2, pos, 0.0), axis=-1, keepdims=True)
    carry[...] = carry[...] + jnp.sum(oh, axis=0, keepdims=True)
    cnt_o[...] = carry[...]
    idx = jnp.where(lane == 0, i1 - N_GROUPS, jnp.where(lane == 1, i2 - N_GROUPS,
          jnp.where(lane == 2, pos1, jnp.where(lane == 3, pos2, 0.0))))
    idx_o[...] = idx.astype(I32)
    gate_o[...] = jnp.where(lane == 0, g_w * w1, jnp.where(lane == 1, g_w * w2, 0.0))


def _route(logits):
    t = logits.shape[0]
    tm = ROW_TILE
    row = pl.BlockSpec((tm, LANES), lambda i: (i, 0))
    return pl.pallas_call(
        _route_body, grid=(t // tm,), in_specs=[row],
        out_specs=[row, row, _full((1, LANES))],
        out_shape=[jax.ShapeDtypeStruct((t, LANES), I32), jax.ShapeDtypeStruct((t, LANES), F32),
                   jax.ShapeDtypeStruct((1, LANES), F32)],
        scratch_shapes=[pltpu.VMEM((1, LANES), F32)],
        compiler_params=_params("arbitrary"), name="moe_route",
    )(logits)


def _row_copy(src, s, dst, d, sem):
    return pltpu.make_async_copy(src.at[pl.ds(s, 1)], dst.at[pl.ds(d, 1)], sem)


def _dispatch_body(dest_ref, xn_ref, xs_in, xs_out, sem):
    del xs_in
    tm = xn_ref.shape[0]

    def issue(r, c):
        _row_copy(xn_ref, r, xs_out, dest_ref[0, 0, r], sem).start()
        _row_copy(xn_ref, r, xs_out, dest_ref[0, 0, tm + r], sem).start()
        return c

    def drain(r, c):
        _row_copy(xn_ref, 0, xs_out, 0, sem).wait()
        return c

    lax.fori_loop(0, tm, issue, 0)
    lax.fori_loop(0, 2 * tm, drain, 0)


def _dispatch(xn, dest, cap):
    t, d = xn.shape
    tm = ROW_TILE
    return pl.pallas_call(
        _dispatch_body, grid=(t // tm,),
        in_specs=[pl.BlockSpec((1, 1, 2 * tm), lambda i: (i, 0, 0), memory_space=pltpu.SMEM),
                  pl.BlockSpec((tm, d), lambda i: (i, 0)),
                  pl.BlockSpec(memory_space=pl.ANY)],
        out_specs=pl.BlockSpec(memory_space=pl.ANY),
        out_shape=jax.ShapeDtypeStruct((cap, d), F32),
        scratch_shapes=[pltpu.SemaphoreType.DMA],
        input_output_aliases={2: 0},
        compiler_params=_params("arbitrary"), name="moe_dispatch",
    )(dest, xn, jnp.zeros((cap, d), F32))


def _experts_body(be_ref, nb_ref, xs_ref, wgu_ref, wd_ref, ys_ref):
    del be_ref
    live = pl.program_id(0) < nb_ref[0]

    @pl.when(live)
    def _():
        gu = _dot(xs_ref[...].astype(BF16), wgu_ref[0])
        f = gu.shape[1] // 2
        gt = gu[:, :f]
        hid = gt * _sigmoid(gt) * gu[:, f:]
        ys_ref[...] = _dot(hid.astype(BF16), wd_ref[0])

    @pl.when(jnp.logical_not(live))
    def _():
        ys_ref[...] = jnp.zeros_like(ys_ref)


def _experts(xs, block_expert, n_live, w_gate_up, w_down):
    cap, d = xs.shape
    f2 = w_gate_up.shape[2]
    blk = MOE_BLOCK
    grid_spec = pltpu.PrefetchScalarGridSpec(
        num_scalar_prefetch=2, grid=(cap // blk,),
        in_specs=[pl.BlockSpec((blk, d), lambda i, be, nb: (i, 0)),
                  pl.BlockSpec((1, d, f2), lambda i, be, nb: (be[i], 0, 0)),
                  pl.BlockSpec((1, f2 // 2, d), lambda i, be, nb: (be[i], 0, 0))],
        out_specs=pl.BlockSpec((blk, d), lambda i, be, nb: (i, 0)))
    return pl.pallas_call(
        _experts_body, grid_spec=grid_spec, out_shape=jax.ShapeDtypeStruct((cap, d), F32),
        compiler_params=_params("arbitrary"), name="moe_experts",
    )(block_expert, n_live, xs, w_gate_up, w_down)


def _combine_body(dest_ref, x_ref, gate_ref, ys_ref, *rest, n_proj, final_norm):
    norm_refs = rest[:n_proj + (1 if final_norm else 0)]
    w_refs = rest[len(norm_refs):len(norm_refs) + n_proj]
    outs = rest[len(norm_refs) + n_proj:-3]
    buf1, buf2, sem = rest[-3:]
    tm = x_ref.shape[0]

    def issue(r, c):
        _row_copy(ys_ref, dest_ref[0, 0, r], buf1, r, sem).start()
        _row_copy(ys_ref, dest_ref[0, 0, tm + r], buf2, r, sem).start()
        return c

    def drain(r, c):
        _row_copy(ys_ref, 0, buf1, 0, sem).wait()
        return c

    lax.fori_loop(0, tm, issue, 0)
    lax.fori_loop(0, 2 * tm, drain, 0)
    gate = gate_ref[...]
    x = x_ref[...] + gate[:, 0:1] * buf1[...] + gate[:, 1:2] * buf2[...]
    if final_norm:
        outs[0][...] = _rms(x, norm_refs[0][...])
        return
    outs[0][...] = x
    for j in range(n_proj):
        outs[1 + j][...] = _dot(_rms(x, norm_refs[j][...]).astype(BF16), w_refs[j][...]).astype(outs[1 + j].dtype)


def _combine(x, ys, dest, gates, norms, weights, final_norm=False):
    t, d = x.shape
    tm = ROW_TILE
    row = pl.BlockSpec((tm, d), lambda i: (i, 0))
    in_specs = [pl.BlockSpec((1, 1, 2 * tm), lambda i: (i, 0, 0), memory_space=pltpu.SMEM),
                row, pl.BlockSpec((tm, LANES), lambda i: (i, 0)), pl.BlockSpec(memory_space=pl.ANY)]
    in_specs += [_full((1, d))] * len(norms) + [_full(w.shape) for w in weights]
    out_specs = [row] + [pl.BlockSpec((tm, w.shape[1]), lambda i: (i, 0)) for w in weights]
    out_shape = [jax.ShapeDtypeStruct((t, d), F32)]
    out_shape += [jax.ShapeDtypeStruct((t, w.shape[1]), BF16) for w in weights]
    return pl.pallas_call(
        functools.partial(_combine_body, n_proj=len(weights), final_norm=final_norm),
        grid=(t // tm,), in_specs=in_specs, out_specs=out_specs, out_shape=out_shape,
        scratch_shapes=[pltpu.VMEM((tm, d), F32), pltpu.VMEM((tm, d), F32), pltpu.SemaphoreType.DMA],
        compiler_params=_params("arbitrary"), name="moe_combine",
    )(dest, x, gates, ys, *[g.reshape(1, d) for g in norms], *[w.astype(BF16) for w in weights])


def _moe(x, xn, logits, w_gate, w_up, w_down, norms, weights, final_norm=False):
    t, d = x.shape
    blk = MOE_BLOCK
    tm = ROW_TILE
    idx, gates, counts = _route(logits)
    counts = counts[0, N_GROUPS:N_GROUPS + N_EXPERTS].astype(I32)
    padded = (counts + blk - 1) // blk * blk
    pend = jnp.cumsum(padded)
    pstart = pend - padded
    cap = 2 * t + N_EXPERTS * blk
    n_blocks = cap // blk
    block_expert = jnp.minimum(
        jnp.searchsorted(pend, jnp.arange(n_blocks, dtype=I32) * blk, side="right"), N_EXPERTS - 1).astype(I32)
    n_live = (pend[-1:] // blk).astype(I32)
    dest = jnp.take(pstart, idx[:, 0:2], axis=0) + idx[:, 2:4]
    dest = dest.reshape(t // tm, tm, 2).transpose(0, 2, 1).reshape(t // tm, 1, 2 * tm)
    xs = _dispatch(xn, dest, cap)
    w_gu = jnp.concatenate([w_gate, w_up], axis=2).astype(BF16)
    ys = _experts(xs, block_expert, n_live, w_gu, w_down.astype(BF16))
    return _combine(x, ys, dest, gates, norms, weights, final_norm)


def _sb_attn_body(q_ref, k_ref, v_ref, o_ref):
    i = pl.program_id(2)
    blk, win = SB_BLOCK, SB_WINDOW
    nh = q_ref.shape[1] // HEAD
    qpos = i * blk + lax.broadcasted_iota(I32, (blk, win), 0)
    col = lax.broadcasted_iota(I32, (blk, win), 1)
    r2 = lax.broadcasted_iota(I32, (win, win), 0)
    c2 = lax.broadcasted_iota(I32, (win, win), 1)
    suffix = jnp.where(r2 >= c2, 1.0, 0.0).astype(BF16)
    lanes = [slice(h * HEAD, (h + 1) * HEAD) for h in range(nh)]
    qs = [(q_ref[:, sl].astype(F32) * (HEAD ** -0.5)).astype(BF16) for sl in lanes]

    def more(carry):
        hi, go = carry[0], carry[1]
        return jnp.logical_and(hi > 0, go)

    def window(carry):
        hi, _, accs, rests = carry
        lo = jnp.maximum(hi - win, 0)
        rows = pl.ds(pl.multiple_of(lo, blk), win)
        valid = (lo + col) < jnp.minimum(hi, qpos)
        zs, parts = [], []
        for h in range(nh):
            z = _dot_nt(qs[h], k_ref[rows, lanes[h]])
            lk = jnp.where(valid, -_softplus(z), 0.0)
            top = lk.astype(BF16)
            zs.append(z)
            parts += [top, (lk - top.astype(F32)).astype(BF16)]
        sums = _dot(jnp.concatenate(parts, axis=0), suffix)
        new_accs, new_rests = [], []
        for h in range(nh):
            sfx = sums[2 * h * blk:(2 * h + 1) * blk] + sums[(2 * h + 1) * blk:(2 * h + 2) * blk]
            attn = jnp.where(valid, jnp.exp(zs[h] + sfx + rests[h]), 0.0)
            new_accs.append(accs[h] + _dot(attn.astype(BF16), v_ref[rows, lanes[h]]))
            new_rests.append(rests[h] + sfx[:, 0:1])
        worst = functools.reduce(jnp.maximum, new_rests)
        return lo, jnp.max(worst) > SB_ZERO_LOG, tuple(new_accs), tuple(new_rests)

    init = ((i + 1) * blk, True, tuple(jnp.zeros((blk, HEAD), F32) for _ in range(nh)),
            tuple(jnp.zeros((blk, 1), F32) for _ in range(nh)))
    accs = lax.while_loop(more, window, init)[2]
    o_ref[...] = jnp.concatenate(accs, axis=1).astype(o_ref.dtype)


def _sb_attn(q, kv, batch, seq):
    t, d = q.shape
    nq = seq // SB_BLOCK
    w = SB_LANES
    hp = d // w
    return pl.pallas_call(
        _sb_attn_body, grid=(batch, hp, nq),
        in_specs=[pl.BlockSpec((SB_BLOCK, w), lambda b, h, i: (b * nq + i, h)),
                  pl.BlockSpec((seq, w), lambda b, h, i: (b, h)),
                  pl.BlockSpec((seq, w), lambda b, h, i: (b, hp + h))],
        out_specs=pl.BlockSpec((SB_BLOCK, w), lambda b, h, i: (b * nq + i, h)),
        out_shape=jax.ShapeDtypeStruct((t, d), BF16),
        compiler_params=_params("parallel", "parallel", "arbitrary"), name="sb_attn",
    )(q, kv, kv)


def _router_weights(w_group, b_group, w_expert, b_expert):
    d = w_group.shape[0]
    pad = LANES - N_GROUPS - N_EXPERTS
    w = jnp.concatenate([w_group, w_expert, jnp.zeros((d, pad), F32)], axis=1)
    b = jnp.concatenate([b_group, b_expert, jnp.zeros((pad,), F32)]).reshape(1, LANES)
    return w, b


def kernel(x, mem, norm_mix, norm_mem, norm_ffn, norm_final, rwkv_mu, rwkv_w_rkv, rwkv_w0, rwkv_w1, rwkv_w2, rwkv_a0, rwkv_a1, rwkv_a2, rwkv_g1, rwkv_g2, rwkv_k_k, rwkv_k_a, rwkv_r_k, rwkv_ln_g, rwkv_ln_b, rwkv_w_o, kv_norm, w_kv_shared, sb_w_q, sb_w_o, mem_norm_kv, mem_w_q, mem_w_kv, mem_w_o, moe_w_group, moe_b_group, moe_w_expert, moe_b_expert, moe_w_gate, moe_w_up, moe_w_down):
    batch, seq, d = x.shape
    depth = norm_mix.shape[0]
    n_a = rwkv_mu.shape[0]
    mem_tokens = mem.shape[1]
    xs = x.reshape(batch * seq, d)
    memf = mem.reshape(batch * mem_tokens, d)
    q = kv = None
    for l in range(depth):
        if l < n_a:
            r, lw, k, v, kk, asg, g = _rwkv_proj(
                xs, seq, norm_mix[l], rwkv_mu[l], rwkv_w_rkv[l], rwkv_w0[l], rwkv_w1[l], rwkv_w2[l],
                rwkv_a0[l], rwkv_a1[l], rwkv_a2[l], rwkv_g1[l], rwkv_g2[l], rwkv_k_k[l], rwkv_k_a[l])
            y = _rwkv_scan(r, lw, k, v, kk, asg, rwkv_r_k[l], rwkv_ln_g[l], rwkv_ln_b[l], batch, seq)
            xs = _linear_res(xs, y, rwkv_w_o[l], gate=g)
        else:
            o = _sb_attn(q, kv, batch, seq)
            xs = _linear_res(xs, o, sb_w_o[l - n_a])
        mkv = _norm_linear(memf, mem_norm_kv[l], mem_w_kv[l], BF16)
        w_router, b_router = _router_weights(moe_w_group[l], moe_b_group[l], moe_w_expert[l], moe_b_expert[l])
        xs, xn, logits = _mem_attn(xs, mkv, seq, mem_tokens, norm_mem[l], mem_w_q[l], mem_w_o[l],
                                   norm_ffn[l], w_router, b_router)
        last = l == depth - 1
        if last:
            norms, weights = [norm_final], []
        elif l + 1 < n_a:
            norms, weights = [], []
        else:
            norms, weights = [norm_mix[l + 1]], [sb_w_q[l + 1 - n_a]]
            if l == n_a - 1:
                norms, weights = norms + [kv_norm], weights + [w_kv_shared]
        res = _moe(xs, xn, logits, moe_w_gate[l], moe_w_up[l], moe_w_down[l], norms, weights, final_norm=last)
        if last:
            return res[0].reshape(batch, seq, d)
        xs = res[0]
        if weights:
            q = res[1]
            if l == n_a - 1:
                kv = res[2]
```

```python
import functools

import jax
import jax.numpy as jnp
from jax import lax
from jax.experimental import pallas as pl
from jax.experimental.pallas import tpu as pltpu

F32 = jnp.float32
BF16 = jnp.bfloat16
I32 = jnp.int32
HIGHEST = lax.Precision.HIGHEST

HEAD = 64
MEM_HEADS = 4
N_GROUPS = 4
EXPERTS_PER_GROUP = 8
N_EXPERTS = N_GROUPS * EXPERTS_PER_GROUP
NORM_EPS = 1e-6
RWKV_LN_EPS = 64e-5

LANES = 128
CHUNK = 64
SB_BLOCK = 128
SB_WINDOW = 256
SB_LANES = 256
SB_ZERO_LOG2 = -160.0
LOG2E = 1.4426950408889634
MOE_BLOCK = 256
DMA_UNROLL = 8
ROW_TILE = 256
SCAN_ROWS = 512
SCAN_LANES = 256
SCAN_GROUP = 4
VMEM_LIMIT = 56 * 1024 * 1024


def _params(*sem):
    return pltpu.CompilerParams(dimension_semantics=sem, vmem_limit_bytes=VMEM_LIMIT)


def _dot(a, b):
    return jnp.dot(a, b, preferred_element_type=F32)


def _dot_nt(a, b):
    return lax.dot_general(a, b, (((1,), (1,)), ((), ())), preferred_element_type=F32)


def _dot_tn(a, b):
    return lax.dot_general(a, b, (((0,), (0,)), ((), ())), preferred_element_type=F32)


def _rms(x, g):
    ms = jnp.mean(x * x, axis=-1, keepdims=True)
    return x * lax.rsqrt(ms + NORM_EPS) * g


def _sigmoid(x):
    return 1.0 / (1.0 + jnp.exp(-x))


def _softplus(x):
    return jnp.maximum(x, 0.0) + jnp.log(1.0 + jnp.exp(-jnp.abs(x)))


def _full(shape):
    return pl.BlockSpec(shape, lambda *_: (0,) * len(shape))


def _rwkv_proj_body(x_ref, xp_ref, gn_ref, mu_ref, wr_ref, wk_ref, wv_ref, w1_ref, a1_ref, g1_ref,
                    w2_ref, a2_ref, g2_ref, vec_ref,
                    r_o, lw_o, k_o, v_o, kk_o, as_o, g_o, *, tiles_per_seq):
    i = pl.program_id(0)
    gn = gn_ref[...]
    h = _rms(x_ref[...], gn)
    hp = _rms(xp_ref[...], gn)[7:8, :]
    hp = jnp.where(lax.rem(i, tiles_per_seq) == 0, 0.0, hp)
    row = lax.broadcasted_iota(I32, (h.shape[0], 1), 0)
    hprev = jnp.where(row == 0, hp, pltpu.roll(h, 1, 0))
    xx = hprev - h
    mu = mu_ref[...]

    def lerp(j):
        return (h + xx * mu[j:j + 1, :]).astype(BF16)

    vec = vec_ref[...]
    r = _dot(lerp(0), wr_ref[...])
    k = _dot(lerp(1), wk_ref[...])
    v = _dot(lerp(2), wv_ref[...])
    w = vec[0:1] + _dot(jnp.tanh(_dot(lerp(3), w1_ref[...])).astype(BF16), w2_ref[...])
    w = -_softplus(-w) - 0.5
    a = _sigmoid(vec[1:2] + _dot(_dot(lerp(4), a1_ref[...]).astype(BF16), a2_ref[...]))
    g = _dot(_sigmoid(_dot(lerp(5), g1_ref[...])).astype(BF16), g2_ref[...])
    r_o[...] = r.astype(BF16)
    lw_o[...] = -jnp.exp(w)
    k_o[...] = (k * (1.0 + (a - 1.0) * vec[3:4])).astype(BF16)
    v_o[...] = v.astype(BF16)
    kk_o[...] = (k * vec[2:3]).astype(BF16)
    as_o[...] = a.astype(BF16)
    g_o[...] = g.astype(BF16)


def _rwkv_proj(x, seq, gn, mu, w_rkv, w0, w1, w2, a0, a1, a2, g1, g2, k_k, k_a):
    t, d = x.shape
    tm = ROW_TILE
    pad8 = lambda rows: jnp.concatenate([rows, jnp.zeros((8 - rows.shape[0], d), F32)], axis=0)
    vec = pad8(jnp.stack([w0, a0, k_k, k_a]))
    mu8 = pad8(mu)
    wts = [w_rkv[0], w_rkv[1], w_rkv[2], w1, a1, g1, w2, a2, g2]
    wts = [w.astype(BF16) for w in wts]
    row_spec = pl.BlockSpec((tm, d), lambda i: (i, 0))
    prev_spec = pl.BlockSpec((8, d), lambda i: (jnp.maximum(i * (tm // 8) - 1, 0), 0))
    in_specs = [row_spec, prev_spec, _full((1, d)), _full((8, d))]
    in_specs += [_full(w.shape) for w in wts] + [_full((8, d))]
    outs = [jax.ShapeDtypeStruct((t, d), dt) for dt in (BF16, F32, BF16, BF16, BF16, BF16, BF16)]
    return pl.pallas_call(
        functools.partial(_rwkv_proj_body, tiles_per_seq=seq // tm),
        grid=(t // tm,), in_specs=in_specs, out_specs=[row_spec] * 7, out_shape=outs,
        compiler_params=_params("parallel"), name="rwkv_proj",
    )(x, x, gn.reshape(1, d), mu8, *wts, vec)


def _cumsum_rows(x):
    n = x.shape[0]
    row = lax.broadcasted_iota(I32, (n, 1), 0)
    s = 1
    while s < n:
        x = x + jnp.where(row >= s, pltpu.roll(x, s, 0), 0.0)
        s *= 2
    return x


def _rwkv_scan_body(r_ref, lw_ref, k_ref, v_ref, kk_ref, as_ref, rk_ref, lg_ref, lb_ref, y_ref,
                    state, wr_s, u0_s, u0t_s, vt_s, o0_s, arb_s, rhs_s, g_s):
    c = CHUNK
    n = HEAD
    nh = r_ref.shape[1] // n
    n_chunks = r_ref.shape[0] // c

    @pl.when(pl.program_id(2) == 0)
    def _():
        state[...] = jnp.zeros_like(state)

    row = lax.broadcasted_iota(I32, (c, c), 0)
    col = lax.broadcasted_iota(I32, (c, c), 1)
    strict = row > col
    incl = row >= col
    eye = (row == col).astype(F32)
    eye_b = eye.astype(BF16)
    lanes = [slice(hh * n, (hh + 1) * n) for hh in range(nh)]

    def local(cj, carry):
        units, a_ts, r_ts, lhss, rhss, vrefs = [], [], [], [], [], []
        for sub in range(SCAN_GROUP):
            ci = cj * SCAN_GROUP + sub
            rows = pl.ds(pl.multiple_of(ci * c, c), c)
            r = r_ref[rows, :].astype(F32)
            lw = lw_ref[rows, :]
            k = k_ref[rows, :].astype(F32)
            kkr = kk_ref[rows, :].astype(F32)
            asg = as_ref[rows, :].astype(F32)
            cum = _cumsum_rows(lw)
            g_in = jnp.exp(cum)
            g_ex = jnp.exp(cum - lw)
            g_inv = jnp.exp(-cum)
            g_s[ci] = g_in[c - 1:c, :]
            for hh, sl in enumerate(lanes):
                kh = kkr[:, sl]
                kkn = kh * lax.rsqrt(jnp.maximum(jnp.sum(kh * kh, axis=-1, keepdims=True), 1e-24))
                a_t = -kkn * g_ex[:, sl]
                b_t = kkn * asg[:, sl] * g_inv[:, sl]
                k_t = k[:, sl] * g_inv[:, sl]
                r_t = (r[:, sl] * g_in[:, sl]).astype(BF16)
                units.append((ci, hh))
                a_ts.append(a_t)
                r_ts.append(r_t)
                lhss.append(jnp.concatenate([a_t.astype(BF16), r_t], axis=0))
                rhss.append(jnp.concatenate([b_t, k_t], axis=0).astype(BF16))
                vrefs.append((rows, sl))
        ps = [_dot_nt(lhs, rhs) for lhs, rhs in zip(lhss, rhss)]
        lps = [jnp.where(strict, p[:c, :c], 0.0) for p in ps]
        xos = [_dot(jnp.concatenate([jnp.where(strict, p[:c, c:], 0.0), jnp.where(incl, p[c:, c:], 0.0)],
                                    axis=0).astype(BF16), v_ref[rs, sl]) for p, (rs, sl) in zip(ps, vrefs)]
        for (ci, hh), p, rhs, xo in zip(units, ps, rhss, xos):
            arb_s[ci, hh] = jnp.where(incl, p[c:, :c], 0.0).astype(BF16)
            rhs_s[ci, hh] = rhs
            o0_s[ci, hh] = xo[c:]
        tinvs = [eye + lp for lp in lps]
        s = 2
        while s < c:
            lpbs = [lp.astype(BF16) for lp in lps]
            lps = [_dot(lpb, lpb) for lpb in lpbs]
            tinvs = [t + _dot(t.astype(BF16), lp.astype(BF16)) for t, lp in zip(tinvs, lps)]
            s *= 2
        wus = [_dot(t.astype(BF16), jnp.concatenate([a_t, xo[:c]], axis=1).astype(BF16))
               for t, a_t, xo in zip(tinvs, a_ts, xos)]
        trs = []
        for wu, (rs, sl) in zip(wus, vrefs):
            u0 = wu[:, n:]
            top = u0.astype(BF16)
            low = (u0 - top.astype(F32)).astype(BF16)
            trs.append(_dot_tn(jnp.concatenate([top, low, v_ref[rs, sl]], axis=1), eye_b))
        for (ci, hh), wu, r_t, tr in zip(units, wus, r_ts, trs):
            wr_s[ci, hh] = jnp.concatenate([wu[:, :n].astype(BF16), r_t], axis=0)
            u0_s[ci, hh] = wu[:, n:]
            u0t_s[ci, hh] = tr[:n] + tr[n:2 * n]
            vt_s[ci, hh] = tr[2 * n:].astype(BF16)
        return carry

    lax.fori_loop(0, n_chunks // SCAN_GROUP, local, 0)

    rk = rk_ref[...]
    lng = lg_ref[...]
    lnb = lb_ref[...]
    heads = range(nh)

    def finish(ci, os_):
        rows = pl.ds(pl.multiple_of(ci * c, c), c)
        r = r_ref[rows, :].astype(F32)
        k = k_ref[rows, :].astype(F32)
        v = v_ref[rows, :].astype(F32)
        ys = []
        for hh in heads:
            sl = lanes[hh]
            o = os_[hh]
            mean = jnp.mean(o, axis=-1, keepdims=True)
            var = jnp.mean(jnp.square(o - mean), axis=-1, keepdims=True)
            y = (o - mean) * lax.rsqrt(var + RWKV_LN_EPS) * lng[:, sl] + lnb[:, sl]
            bonus = jnp.sum(r[:, sl] * k[:, sl] * rk[:, sl], axis=-1, keepdims=True) * v[:, sl]
            ys.append(y + bonus)
        y_ref[rows, :] = jnp.concatenate(ys, axis=1).astype(y_ref.dtype)

    def sweep(ci, os_prev):
        g_last = g_s[ci]
        sbs = [state[hh].astype(BF16) for hh in heads]
        wss = [_dot_nt(wr_s[ci, hh], sbs[hh]) for hh in heads]
        uts = [_dot_nt(sbs[hh], wr_s[ci, hh, 0:c, :]) for hh in heads]
        finish(jnp.maximum(ci - 1, 0), os_prev)
        ubs = [(wss[hh][:c] + u0_s[ci, hh]).astype(BF16) for hh in heads]
        uvts = [jnp.concatenate([(uts[hh] + u0t_s[ci, hh]).astype(BF16), vt_s[ci, hh]], axis=1) for hh in heads]
        os_ = [wss[hh][c:] + o0_s[ci, hh] + _dot(arb_s[ci, hh], ubs[hh]) for hh in heads]
        for hh in heads:
            state[hh] = (state[hh] + _dot(uvts[hh], rhs_s[ci, hh])) * g_last[:, lanes[hh]]
        return tuple(os_)

    zeros = tuple(jnp.zeros((c, n), F32) for _ in heads)
    finish(n_chunks - 1, lax.fori_loop(0, n_chunks, sweep, zeros))


def _rwkv_scan(r, lw, k, v, kk, asg, r_k, ln_g, ln_b, batch, seq):
    t, d = r.shape
    rb = SCAN_ROWS
    w = SCAN_LANES
    nc = seq // rb
    nh = w // HEAD
    n_chunks = rb // CHUNK
    blk = pl.BlockSpec((rb, w), lambda b, hp, ci: (b * nc + ci, hp))
    vec = pl.BlockSpec((1, w), lambda b, hp, ci: (0, hp))
    per_unit = lambda rows, cols, dt: pltpu.VMEM((n_chunks, nh, rows, cols), dt)
    scratch = [pltpu.VMEM((nh, HEAD, HEAD), F32),
               per_unit(2 * CHUNK, HEAD, BF16),
               per_unit(CHUNK, HEAD, F32),
               per_unit(HEAD, CHUNK, F32),
               per_unit(HEAD, CHUNK, BF16),
               per_unit(CHUNK, HEAD, F32),
               per_unit(CHUNK, CHUNK, BF16),
               per_unit(2 * CHUNK, HEAD, BF16),
               pltpu.VMEM((n_chunks, 1, w), F32)]
    return pl.pallas_call(
        _rwkv_scan_body,
        grid=(batch, d // w, nc),
        in_specs=[blk] * 6 + [vec] * 3, out_specs=blk,
        out_shape=jax.ShapeDtypeStruct((t, d), BF16),
        scratch_shapes=scratch,
        compiler_params=_params("parallel", "parallel", "arbitrary"), name="rwkv_scan",
    )(r, lw, k, v, kk, asg, r_k.reshape(1, d), ln_g.reshape(1, d), ln_b.reshape(1, d))


def _linear_res_body(x_ref, a_ref, *rest, gated):
    if gated:
        g_ref, w_ref, o_ref = rest
        a = a_ref[...] * g_ref[...]
    else:
        w_ref, o_ref = rest
        a = a_ref[...]
    o_ref[...] = x_ref[...] + _dot(a, w_ref[...])


def _linear_res(x, a, w, gate=None):
    t, d = x.shape
    tm = ROW_TILE
    row = pl.BlockSpec((tm, d), lambda i: (i, 0))
    ins = [x, a] + ([gate] if gate is not None else []) + [w.astype(BF16)]
    return pl.pallas_call(
        functools.partial(_linear_res_body, gated=gate is not None),
        grid=(t // tm,), in_specs=[row] * (len(ins) - 1) + [_full(w.shape)], out_specs=row,
        out_shape=jax.ShapeDtypeStruct((t, d), F32),
        compiler_params=_params("parallel"), name="linear_res",
    )(*ins)


def _norm_linear_body(x_ref, g_ref, w_ref, o_ref):
    o_ref[...] = _dot(_rms(x_ref[...], g_ref[...]).astype(BF16), w_ref[...]).astype(o_ref.dtype)


def _norm_linear(x, g, w, out_dtype):
    t, d = x.shape
    tm = ROW_TILE
    return pl.pallas_call(
        _norm_linear_body, grid=(t // tm,),
        in_specs=[pl.BlockSpec((tm, d), lambda i: (i, 0)), _full((1, d)), _full(w.shape)],
        out_specs=pl.BlockSpec((tm, w.shape[1]), lambda i: (i, 0)),
        out_shape=jax.ShapeDtypeStruct((t, w.shape[1]), out_dtype),
        compiler_params=_params("parallel"), name="norm_linear",
    )(x, g.reshape(1, d), w.astype(BF16))


def _mem_attn_body(x_ref, kv_ref, gn_ref, wq_ref, wo_ref, gf_ref, wrh_ref, wrl_ref, br_ref, x_o, xn_o, lg_o):
    x = x_ref[...]
    d = x.shape[1]
    hd = d // MEM_HEADS
    q = _dot(_rms(x, gn_ref[...]).astype(BF16), wq_ref[...])
    outs = []
    for hh in range(MEM_HEADS):
        qh = q[:, hh * hd:(hh + 1) * hd].astype(BF16)
        kh = kv_ref[:, hh * hd:(hh + 1) * hd]
        vh = kv_ref[:, d + hh * hd:d + (hh + 1) * hd]
        s = _dot_nt(qh, kh)
        p = jnp.exp(s - jnp.max(s, axis=-1, keepdims=True))
        o = _dot(p.astype(BF16), vh) / jnp.sum(p, axis=-1, keepdims=True)
        outs.append(o.astype(BF16))
    xnew = x + _dot(jnp.concatenate(outs, axis=1), wo_ref[...])
    x_o[...] = xnew
    xn = _rms(xnew, gf_ref[...])
    xn_o[...] = xn
    xh = xn.astype(BF16)
    xl = (xn - xh.astype(F32)).astype(BF16)
    tm = x.shape[0]
    hw = _dot(jnp.concatenate([xh, xl], axis=0), wrh_ref[...])
    lg_o[...] = hw[:tm] + hw[tm:] + _dot(xh, wrl_ref[...]) + br_ref[...]


def _mem_attn(x, kv, seq, mem_tokens, gn, w_q, w_o, g_ffn, w_router, b_router):
    t, d = x.shape
    tm = ROW_TILE
    per_seq = seq // tm
    row = pl.BlockSpec((tm, d), lambda i: (i, 0))
    w_hi = w_router.astype(BF16)
    return pl.pallas_call(
        _mem_attn_body, grid=(t // tm,),
        in_specs=[row, pl.BlockSpec((mem_tokens, 2 * d), lambda i: (i // per_seq, 0)),
                  _full((1, d)), _full((d, d)), _full((d, d)), _full((1, d)),
                  _full((d, LANES)), _full((d, LANES)), _full((1, LANES))],
        out_specs=[row, row, pl.BlockSpec((tm, LANES), lambda i: (i, 0))],
        out_shape=[jax.ShapeDtypeStruct((t, d), F32), jax.ShapeDtypeStruct((t, d), F32),
                   jax.ShapeDtypeStruct((t, LANES), F32)],
        compiler_params=_params("parallel"), name="mem_attn",
    )(x, kv, gn.reshape(1, d), (w_q * (d // MEM_HEADS) ** -0.5).astype(BF16), w_o.astype(BF16),
      g_ffn.reshape(1, d), w_hi, (w_router - w_hi.astype(F32)).astype(BF16), b_router)


def _route_body(lg_ref, idx_o, gate_o, cnt_o, carry):
    @pl.when(pl.program_id(0) == 0)
    def _():
        carry[...] = jnp.zeros_like(carry)

    lg = lg_ref[...]
    tm = lg.shape[0]
    lane = lax.broadcasted_iota(I32, lg.shape, 1)
    lanef = lane.astype(F32)
    big = 1e9
    ninf = -jnp.inf
    gl = jnp.where(lane < N_GROUPS, lg, ninf)
    gmax = jnp.max(gl, axis=-1, keepdims=True)
    g_w = 1.0 / jnp.sum(jnp.exp(gl - gmax), axis=-1, keepdims=True)
    grp = jnp.min(jnp.where(gl == gmax, lanef, big), axis=-1, keepdims=True).astype(I32)
    in_grp = (lane >= N_GROUPS) & (lane < N_GROUPS + N_EXPERTS) & (((lane - N_GROUPS) >> 3) == grp)
    el = jnp.where(in_grp, lg, ninf)
    m1 = jnp.max(el, axis=-1, keepdims=True)
    i1 = jnp.min(jnp.where(el == m1, lanef, big), axis=-1, keepdims=True)
    el2 = jnp.where(lanef == i1, ninf, el)
    m2 = jnp.max(el2, axis=-1, keepdims=True)
    i2 = jnp.min(jnp.where(el2 == m2, lanef, big), axis=-1, keepdims=True)
    tt = jnp.exp(m2 - m1)
    w1 = 1.0 / (1.0 + tt)
    w2 = tt / (1.0 + tt)
    hit1 = lanef == i1
    hit2 = lanef == i2
    oh = jnp.where(hit1 | hit2, 1.0, 0.0)
    r_i = lax.broadcasted_iota(I32, (tm, tm), 0)
    c_i = lax.broadcasted_iota(I32, (tm, tm), 1)
    tri = jnp.where(r_i > c_i, 1.0, 0.0).astype(BF16)
    pos = _dot(tri, oh.astype(BF16)) + carry[...]
    pos1 = jnp.sum(jnp.where(hit1, pos, 0.0), axis=-1, keepdims=True)
    pos2 = jnp.sum(jnp.where(hit2, pos, 0.0), axis=-1, keepdims=True)
    carry[...] = carry[...] + jnp.sum(oh, axis=0, keepdims=True)
    cnt_o[...] = carry[...]
    idx = jnp.where(lane == 0, i1 - N_GROUPS, jnp.where(lane == 1, i2 - N_GROUPS,
          jnp.where(lane == 2, pos1, jnp.where(lane == 3, pos2, 0.0))))
    idx_o[...] = idx.astype(I32)
    gate_o[...] = jnp.where(lane == 0, g_w * w1, jnp.where(lane == 1, g_w * w2, 0.0))


def _route(logits):
    t = logits.shape[0]
    tm = ROW_TILE
    row = pl.BlockSpec((tm, LANES), lambda i: (i, 0))
    return pl.pallas_call(
        _route_body, grid=(t // tm,), in_specs=[row],
        out_specs=[row, row, _full((1, LANES))],
        out_shape=[jax.ShapeDtypeStruct((t, LANES), I32), jax.ShapeDtypeStruct((t, LANES), F32),
                   jax.ShapeDtypeStruct((1, LANES), F32)],
        scratch_shapes=[pltpu.VMEM((1, LANES), F32)],
        compiler_params=_params("arbitrary"), name="moe_route",
    )(logits)


def _row_copy(src, s, dst, d, sem):
    return pltpu.make_async_copy(src.at[pl.ds(s, 1)], dst.at[pl.ds(d, 1)], sem)


def _dispatch_body(dest_ref, xn_ref, xs_in, xs_out, sem):
    del xs_in
    tm = xn_ref.shape[0]

    def issue(r, c):
        _row_copy(xn_ref, r, xs_out, dest_ref[0, 0, r], sem).start()
        _row_copy(xn_ref, r, xs_out, dest_ref[0, 0, tm + r], sem).start()
        return c

    lax.fori_loop(0, tm, issue, 0, unroll=DMA_UNROLL)
    for _ in range(2):
        pltpu.make_async_copy(xn_ref, xs_out.at[pl.ds(0, tm)], sem).wait()


def _dispatch(xn, dest, cap):
    t, d = xn.shape
    tm = ROW_TILE
    return pl.pallas_call(
        _dispatch_body, grid=(t // tm,),
        in_specs=[pl.BlockSpec((1, 1, 2 * tm), lambda i: (i, 0, 0), memory_space=pltpu.SMEM),
                  pl.BlockSpec((tm, d), lambda i: (i, 0)),
                  pl.BlockSpec(memory_space=pl.ANY)],
        out_specs=pl.BlockSpec(memory_space=pl.ANY),
        out_shape=jax.ShapeDtypeStruct((cap, d), F32),
        scratch_shapes=[pltpu.SemaphoreType.DMA],
        input_output_aliases={2: 0},
        compiler_params=_params("arbitrary"), name="moe_dispatch",
    )(dest, xn, jnp.zeros((cap, d), F32))


def _experts_body(be_ref, nb_ref, xs_ref, wgu_ref, wd_ref, ys_ref):
    del be_ref
    live = pl.program_id(0) < nb_ref[0]

    @pl.when(live)
    def _():
        gu = _dot(xs_ref[...].astype(BF16), wgu_ref[0])
        f = gu.shape[1] // 2
        gt = gu[:, :f]
        hid = gt * _sigmoid(gt) * gu[:, f:]
        ys_ref[...] = _dot(hid.astype(BF16), wd_ref[0])

    @pl.when(jnp.logical_not(live))
    def _():
        ys_ref[...] = jnp.zeros_like(ys_ref)


def _experts(xs, block_expert, n_live, w_gate_up, w_down):
    cap, d = xs.shape
    f2 = w_gate_up.shape[2]
    blk = MOE_BLOCK
    grid_spec = pltpu.PrefetchScalarGridSpec(
        num_scalar_prefetch=2, grid=(cap // blk,),
        in_specs=[pl.BlockSpec((blk, d), lambda i, be, nb: (i, 0)),
                  pl.BlockSpec((1, d, f2), lambda i, be, nb: (be[i], 0, 0)),
                  pl.BlockSpec((1, f2 // 2, d), lambda i, be, nb: (be[i], 0, 0))],
        out_specs=pl.BlockSpec((blk, d), lambda i, be, nb: (i, 0)))
    return pl.pallas_call(
        _experts_body, grid_spec=grid_spec, out_shape=jax.ShapeDtypeStruct((cap, d), F32),
        compiler_params=_params("arbitrary"), name="moe_experts",
    )(block_expert, n_live, xs, w_gate_up, w_down)


def _combine_body(dest_ref, x_ref, gate_ref, ys_ref, *rest, n_proj, final_norm):
    norm_refs = rest[:n_proj + (1 if final_norm else 0)]
    w_refs = rest[len(norm_refs):len(norm_refs) + n_proj]
    outs = rest[len(norm_refs) + n_proj:-3]
    buf1, buf2, sem = rest[-3:]
    tm = x_ref.shape[0]

    def issue(r, c):
        _row_copy(ys_ref, dest_ref[0, 0, r], buf1, r, sem).start()
        _row_copy(ys_ref, dest_ref[0, 0, tm + r], buf2, r, sem).start()
        return c

    lax.fori_loop(0, tm, issue, 0, unroll=DMA_UNROLL)
    for buf in (buf1, buf2):
        pltpu.make_async_copy(ys_ref.at[pl.ds(0, tm)], buf, sem).wait()
    gate = gate_ref[...]
    x = x_ref[...] + gate[:, 0:1] * buf1[...] + gate[:, 1:2] * buf2[...]
    if final_norm:
        outs[0][...] = _rms(x, norm_refs[0][...])
        return
    outs[0][...] = x
    for j in range(n_proj):
        outs[1 + j][...] = _dot(_rms(x, norm_refs[j][...]).astype(BF16), w_refs[j][...]).astype(outs[1 + j].dtype)


def _combine(x, ys, dest, gates, norms, weights, final_norm=False):
    t, d = x.shape
    tm = ROW_TILE
    row = pl.BlockSpec((tm, d), lambda i: (i, 0))
    in_specs = [pl.BlockSpec((1, 1, 2 * tm), lambda i: (i, 0, 0), memory_space=pltpu.SMEM),
                row, pl.BlockSpec((tm, LANES), lambda i: (i, 0)), pl.BlockSpec(memory_space=pl.ANY)]
    in_specs += [_full((1, d))] * len(norms) + [_full(w.shape) for w in weights]
    out_specs = [row] + [pl.BlockSpec((tm, w.shape[1]), lambda i: (i, 0)) for w in weights]
    out_shape = [jax.ShapeDtypeStruct((t, d), F32)]
    out_shape += [jax.ShapeDtypeStruct((t, w.shape[1]), BF16) for w in weights]
    return pl.pallas_call(
        functools.partial(_combine_body, n_proj=len(weights), final_norm=final_norm),
        grid=(t // tm,), in_specs=in_specs, out_specs=out_specs, out_shape=out_shape,
        scratch_shapes=[pltpu.VMEM((tm, d), F32), pltpu.VMEM((tm, d), F32), pltpu.SemaphoreType.DMA],
        compiler_params=_params("arbitrary"), name="moe_combine",
    )(dest, x, gates, ys, *[g.reshape(1, d) for g in norms], *[w.astype(BF16) for w in weights])


def _moe(x, xn, logits, w_gate, w_up, w_down, norms, weights, final_norm=False):
    t, d = x.shape
    blk = MOE_BLOCK
    tm = ROW_TILE
    idx, gates, counts = _route(logits)
    counts = counts[0, N_GROUPS:N_GROUPS + N_EXPERTS].astype(I32)
    padded = (counts + blk - 1) // blk * blk
    pend = jnp.cumsum(padded)
    pstart = pend - padded
    cap = 2 * t + N_EXPERTS * blk
    n_blocks = cap // blk
    starts = jnp.arange(n_blocks, dtype=I32) * blk
    block_expert = jnp.minimum(jnp.sum(pend[None, :] <= starts[:, None], axis=1), N_EXPERTS - 1).astype(I32)
    n_live = (pend[-1:] // blk).astype(I32)
    dest = jnp.take(pstart, idx[:, 0:2], axis=0) + idx[:, 2:4]
    dest = dest.reshape(t // tm, tm, 2).transpose(0, 2, 1).reshape(t // tm, 1, 2 * tm)
    xs = _dispatch(xn, dest, cap)
    w_gu = jnp.concatenate([w_gate, w_up], axis=2).astype(BF16)
    ys = _experts(xs, block_expert, n_live, w_gu, w_down.astype(BF16))
    return _combine(x, ys, dest, gates, norms, weights, final_norm)


def _sb_attn_body(q_ref, k_ref, v_ref, tri_ref, o_ref):
    i = pl.program_id(2)
    blk, win = SB_BLOCK, SB_WINDOW
    nh = q_ref.shape[1] // HEAD
    qpos = i * blk + lax.broadcasted_iota(I32, (blk, win), 0)
    col = lax.broadcasted_iota(I32, (blk, win), 1)
    neg_suffix = tri_ref[...]
    lanes = [slice(h * HEAD, (h + 1) * HEAD) for h in range(nh)]
    qs = [(q_ref[:, sl].astype(F32) * (HEAD ** -0.5 * LOG2E)).astype(BF16) for sl in lanes]
    u32 = jnp.uint32

    def more(carry):
        hi, go = carry[0], carry[1]
        return jnp.logical_and(hi > 0, go)

    def window(carry):
        hi, _, accs, rests = carry
        lo = jnp.maximum(hi - win, 0)
        rows = pl.ds(pl.multiple_of(lo, blk), win)
        valid = (lo + col) < jnp.minimum(hi, qpos)
        zs, parts = [], []
        for h in range(nh):
            z = _dot_nt(qs[h], k_ref[rows, lanes[h]])
            neg_abs = lax.bitcast_convert_type(lax.bitcast_convert_type(z, u32) | u32(0x80000000), F32)
            sp = jnp.maximum(z, 0.0) + jnp.log(1.0 + jnp.exp2(neg_abs)) * LOG2E
            sp = jnp.where(valid, sp, 0.0)
            top = lax.bitcast_convert_type(lax.bitcast_convert_type(sp, u32) & u32(0xFFFF0000), F32)
            zs.append(z)
            parts += [top.astype(BF16), (sp - top).astype(BF16)]
        sums = _dot(jnp.concatenate(parts, axis=0), neg_suffix)
        new_accs, new_rests = [], []
        for h in range(nh):
            sfx = sums[2 * h * blk:(2 * h + 1) * blk] + sums[(2 * h + 1) * blk:(2 * h + 2) * blk]
            attn = jnp.where(valid, jnp.exp2(zs[h] + sfx + rests[h]), 0.0)
            new_accs.append(accs[h] + _dot(attn.astype(BF16), v_ref[rows, lanes[h]]))
            new_rests.append(rests[h] + sfx[:, 0:1])
        worst = functools.reduce(jnp.maximum, new_rests)
        return lo, jnp.max(worst) > SB_ZERO_LOG2, tuple(new_accs), tuple(new_rests)

    init = ((i + 1) * blk, True, tuple(jnp.zeros((blk, HEAD), F32) for _ in range(nh)),
            tuple(jnp.zeros((blk, 1), F32) for _ in range(nh)))
    accs = lax.while_loop(more, window, init)[2]
    o_ref[...] = jnp.concatenate(accs, axis=1).astype(o_ref.dtype)


def _sb_attn(q, kv, batch, seq):
    t, d = q.shape
    nq = seq // SB_BLOCK
    w = SB_LANES
    hp = d // w
    win = SB_WINDOW
    tri = -jnp.tril(jnp.ones((win, win), F32)).astype(BF16)
    return pl.pallas_call(
        _sb_attn_body, grid=(batch, hp, nq),
        in_specs=[pl.BlockSpec((SB_BLOCK, w), lambda b, h, i: (b * nq + i, h)),
                  pl.BlockSpec((seq, w), lambda b, h, i: (b, h)),
                  pl.BlockSpec((seq, w), lambda b, h, i: (b, hp + h)),
                  pl.BlockSpec((win, win), lambda b, h, i: (0, 0))],
        out_specs=pl.BlockSpec((SB_BLOCK, w), lambda b, h, i: (b * nq + i, h)),
        out_shape=jax.ShapeDtypeStruct((t, d), BF16),
        compiler_params=_params("parallel", "parallel", "arbitrary"), name="sb_attn",
    )(q, kv, kv, tri)


def _router_weights(w_group, b_group, w_expert, b_expert):
    d = w_group.shape[0]
    pad = LANES - N_GROUPS - N_EXPERTS
    w = jnp.concatenate([w_group, w_expert, jnp.zeros((d, pad), F32)], axis=1)
    b = jnp.concatenate([b_group, b_expert, jnp.zeros((pad,), F32)]).reshape(1, LANES)
    return w, b


def kernel(x, mem, norm_mix, norm_mem, norm_ffn, norm_final, rwkv_mu, rwkv_w_rkv, rwkv_w0, rwkv_w1, rwkv_w2, rwkv_a0, rwkv_a1, rwkv_a2, rwkv_g1, rwkv_g2, rwkv_k_k, rwkv_k_a, rwkv_r_k, rwkv_ln_g, rwkv_ln_b, rwkv_w_o, kv_norm, w_kv_shared, sb_w_q, sb_w_o, mem_norm_kv, mem_w_q, mem_w_kv, mem_w_o, moe_w_group, moe_b_group, moe_w_expert, moe_b_expert, moe_w_gate, moe_w_up, moe_w_down):
    batch, seq, d = x.shape
    depth = norm_mix.shape[0]
    n_a = rwkv_mu.shape[0]
    mem_tokens = mem.shape[1]
    xs = x.reshape(batch * seq, d)
    memf = mem.reshape(batch * mem_tokens, d)
    q = kv = None
    for l in range(depth):
        if l < n_a:
            r, lw, k, v, kk, asg, g = _rwkv_proj(
                xs, seq, norm_mix[l], rwkv_mu[l], rwkv_w_rkv[l], rwkv_w0[l], rwkv_w1[l], rwkv_w2[l],
                rwkv_a0[l], rwkv_a1[l], rwkv_a2[l], rwkv_g1[l], rwkv_g2[l], rwkv_k_k[l], rwkv_k_a[l])
            y = _rwkv_scan(r, lw, k, v, kk, asg, rwkv_r_k[l], rwkv_ln_g[l], rwkv_ln_b[l], batch, seq)
            xs = _linear_res(xs, y, rwkv_w_o[l], gate=g)
        else:
            o = _sb_attn(q, kv, batch, seq)
            xs = _linear_res(xs, o, sb_w_o[l - n_a])
        mkv = _norm_linear(memf, mem_norm_kv[l], mem_w_kv[l], BF16)
        w_router, b_router = _router_weights(moe_w_group[l], moe_b_group[l], moe_w_expert[l], moe_b_expert[l])
        xs, xn, logits = _mem_attn(xs, mkv, seq, mem_tokens, norm_mem[l], mem_w_q[l], mem_w_o[l],
                                   norm_ffn[l], w_router, b_router)
        last = l == depth - 1
        if last:
            norms, weights = [norm_final], []
        elif l + 1 < n_a:
            norms, weights = [], []
        else:
            norms, weights = [norm_mix[l + 1]], [sb_w_q[l + 1 - n_a]]
            if l == n_a - 1:
                norms, weights = norms + [kv_norm], weights + [w_kv_shared]
        res = _moe(xs, xn, logits, moe_w_gate[l], moe_w_up[l], moe_w_down[l], norms, weights, final_norm=last)
        if last:
            return res[0].reshape(batch, seq, d)
        xs = res[0]
        if weights:
            q = res[1]
            if l == n_a - 1:
                kv = res[2]
```

```python
import functools

import jax
import jax.numpy as jnp
from jax import lax
from jax.experimental import pallas as pl
from jax.experimental.pallas import tpu as pltpu

F32 = jnp.float32
BF16 = jnp.bfloat16
I32 = jnp.int32
HIGHEST = lax.Precision.HIGHEST

HEAD = 64
MEM_HEADS = 4
N_GROUPS = 4
EXPERTS_PER_GROUP = 8
N_EXPERTS = N_GROUPS * EXPERTS_PER_GROUP
NORM_EPS = 1e-6
RWKV_LN_EPS = 64e-5

LANES = 128
CHUNK = 64
SB_BLOCK = 128
SB_WINDOW = 256
SB_LANES = 512
SB_ZERO_LOG2 = -160.0
LOG2E = 1.4426950408889634
MOE_BLOCK = 512
DMA_UNROLL = 8
ROW_TILE = 512
PROJ_TILE = 256
SCAN_ROWS = 512
SCAN_LANES = 256
SCAN_GROUP = 8
VMEM_LIMIT = 56 * 1024 * 1024


def _params(*sem):
    return pltpu.CompilerParams(dimension_semantics=sem, vmem_limit_bytes=VMEM_LIMIT)


def _dot(a, b):
    return jnp.dot(a, b, preferred_element_type=F32)


def _dot_nt(a, b):
    return lax.dot_general(a, b, (((1,), (1,)), ((), ())), preferred_element_type=F32)


def _dot_tn(a, b):
    return lax.dot_general(a, b, (((0,), (0,)), ((), ())), preferred_element_type=F32)


def _rms(x, g):
    ms = jnp.mean(x * x, axis=-1, keepdims=True)
    return x * lax.rsqrt(ms + NORM_EPS) * g


def _sigmoid(x):
    return 1.0 / (1.0 + jnp.exp(-x))


def _softplus(x):
    return jnp.maximum(x, 0.0) + jnp.log(1.0 + jnp.exp(-jnp.abs(x)))


def _full(shape):
    return pl.BlockSpec(shape, lambda *_: (0,) * len(shape))


def _rwkv_proj_body(x_ref, xp_ref, gn_ref, mu_ref, wr_ref, wk_ref, wv_ref, w1_ref, a1_ref, g1_ref,
                    w2_ref, a2_ref, g2_ref, vec_ref,
                    r_o, lw_o, k_o, v_o, kk_o, as_o, g_o, *, tiles_per_seq):
    i = pl.program_id(0)
    gn = gn_ref[...]
    h = _rms(x_ref[...], gn)
    hp = _rms(xp_ref[...], gn)[7:8, :]
    hp = jnp.where(lax.rem(i, tiles_per_seq) == 0, 0.0, hp)
    row = lax.broadcasted_iota(I32, (h.shape[0], 1), 0)
    hprev = jnp.where(row == 0, hp, pltpu.roll(h, 1, 0))
    xx = hprev - h
    mu = mu_ref[...]

    def lerp(j):
        return (h + xx * mu[j:j + 1, :]).astype(BF16)

    vec = vec_ref[...]
    r = _dot(lerp(0), wr_ref[...])
    k = _dot(lerp(1), wk_ref[...])
    v = _dot(lerp(2), wv_ref[...])
    w = vec[0:1] + _dot(jnp.tanh(_dot(lerp(3), w1_ref[...])).astype(BF16), w2_ref[...])
    w = -_softplus(-w) - 0.5
    a = _sigmoid(vec[1:2] + _dot(_dot(lerp(4), a1_ref[...]).astype(BF16), a2_ref[...]))
    g = _dot(_sigmoid(_dot(lerp(5), g1_ref[...])).astype(BF16), g2_ref[...])
    r_o[...] = r.astype(BF16)
    lw_o[...] = -jnp.exp(w)
    k_o[...] = (k * (1.0 + (a - 1.0) * vec[3:4])).astype(BF16)
    v_o[...] = v.astype(BF16)
    kk_o[...] = (k * vec[2:3]).astype(BF16)
    as_o[...] = a.astype(BF16)
    g_o[...] = g.astype(BF16)


def _rwkv_proj(x, seq, gn, mu, w_rkv, w0, w1, w2, a0, a1, a2, g1, g2, k_k, k_a):
    t, d = x.shape
    tm = PROJ_TILE
    pad8 = lambda rows: jnp.concatenate([rows, jnp.zeros((8 - rows.shape[0], d), F32)], axis=0)
    vec = pad8(jnp.stack([w0, a0, k_k, k_a]))
    mu8 = pad8(mu)
    wts = [w_rkv[0], w_rkv[1], w_rkv[2], w1, a1, g1, w2, a2, g2]
    wts = [w.astype(BF16) for w in wts]
    row_spec = pl.BlockSpec((tm, d), lambda i: (i, 0))
    prev_spec = pl.BlockSpec((8, d), lambda i: (jnp.maximum(i * (tm // 8) - 1, 0), 0))
    in_specs = [row_spec, prev_spec, _full((1, d)), _full((8, d))]
    in_specs += [_full(w.shape) for w in wts] + [_full((8, d))]
    outs = [jax.ShapeDtypeStruct((t, d), dt) for dt in (BF16, F32, BF16, BF16, BF16, BF16, BF16)]
    return pl.pallas_call(
        functools.partial(_rwkv_proj_body, tiles_per_seq=seq // tm),
        grid=(t // tm,), in_specs=in_specs, out_specs=[row_spec] * 7, out_shape=outs,
        compiler_params=_params("parallel"), name="rwkv_proj",
    )(x, x, gn.reshape(1, d), mu8, *wts, vec)


def _cumsum_rows(x):
    n = x.shape[0]
    row = lax.broadcasted_iota(I32, (n, 1), 0)
    s = 1
    while s < n:
        x = x + jnp.where(row >= s, pltpu.roll(x, s, 0), 0.0)
        s *= 2
    return x


def _rwkv_scan_body(r_ref, lw_ref, k_ref, v_ref, kk_ref, as_ref, rk_ref, lg_ref, lb_ref, y_ref,
                    state, wr_s, u0_s, u0t_s, vt_s, o0_s, arb_s, rhs_s, g_s):
    c = CHUNK
    n = HEAD
    nh = r_ref.shape[1] // n
    n_chunks = r_ref.shape[0] // c

    @pl.when(pl.program_id(2) == 0)
    def _():
        state[...] = jnp.zeros_like(state)

    row = lax.broadcasted_iota(I32, (c, c), 0)
    col = lax.broadcasted_iota(I32, (c, c), 1)
    strict = row > col
    incl = row >= col
    eye = (row == col).astype(F32)
    eye_b = eye.astype(BF16)
    lanes = [slice(hh * n, (hh + 1) * n) for hh in range(nh)]

    def local(cj, carry):
        units, a_ts, r_ts, lhss, rhss, vrefs = [], [], [], [], [], []
        for sub in range(SCAN_GROUP):
            ci = cj * SCAN_GROUP + sub
            rows = pl.ds(pl.multiple_of(ci * c, c), c)
            r = r_ref[rows, :].astype(F32)
            lw = lw_ref[rows, :]
            k = k_ref[rows, :].astype(F32)
            kkr = kk_ref[rows, :].astype(F32)
            asg = as_ref[rows, :].astype(F32)
            cum = _cumsum_rows(lw)
            g_in = jnp.exp(cum)
            g_ex = jnp.exp(cum - lw)
            g_inv = jnp.exp(-cum)
            g_s[ci] = g_in[c - 1:c, :]
            for hh, sl in enumerate(lanes):
                kh = kkr[:, sl]
                kkn = kh * lax.rsqrt(jnp.maximum(jnp.sum(kh * kh, axis=-1, keepdims=True), 1e-24))
                a_t = -kkn * g_ex[:, sl]
                b_t = kkn * asg[:, sl] * g_inv[:, sl]
                k_t = k[:, sl] * g_inv[:, sl]
                r_t = (r[:, sl] * g_in[:, sl]).astype(BF16)
                units.append((ci, hh))
                a_ts.append(a_t)
                r_ts.append(r_t)
                lhss.append(jnp.concatenate([a_t.astype(BF16), r_t], axis=0))
                rhss.append(jnp.concatenate([b_t, k_t], axis=0).astype(BF16))
                vrefs.append((rows, sl))
        ps = [_dot_nt(lhs, rhs) for lhs, rhs in zip(lhss, rhss)]
        lps = [jnp.where(strict, p[:c, :c], 0.0) for p in ps]
        xos = [_dot(jnp.concatenate([jnp.where(strict, p[:c, c:], 0.0), jnp.where(incl, p[c:, c:], 0.0)],
                                    axis=0).astype(BF16), v_ref[rs, sl]) for p, (rs, sl) in zip(ps, vrefs)]
        for (ci, hh), p, rhs, xo in zip(units, ps, rhss, xos):
            arb_s[ci, hh] = jnp.where(incl, p[c:, :c], 0.0).astype(BF16)
            rhs_s[ci, hh] = rhs
            o0_s[ci, hh] = xo[c:]
        tinvs = [eye + lp for lp in lps]
        s = 2
        while s < c:
            lpbs = [lp.astype(BF16) for lp in lps]
            lps = [_dot(lpb, lpb) for lpb in lpbs]
            tinvs = [t + _dot(t.astype(BF16), lp.astype(BF16)) for t, lp in zip(tinvs, lps)]
            s *= 2
        wus = [_dot(t.astype(BF16), jnp.concatenate([a_t, xo[:c]], axis=1).astype(BF16))
               for t, a_t, xo in zip(tinvs, a_ts, xos)]
        trs = []
        for wu, (rs, sl) in zip(wus, vrefs):
            u0 = wu[:, n:]
            top = u0.astype(BF16)
            low = (u0 - top.astype(F32)).astype(BF16)
            trs.append(_dot_tn(jnp.concatenate([top, low, v_ref[rs, sl]], axis=1), eye_b))
        for (ci, hh), wu, r_t, tr in zip(units, wus, r_ts, trs):
            wr_s[ci, hh] = jnp.concatenate([wu[:, :n].astype(BF16), r_t], axis=0)
            u0_s[ci, hh] = wu[:, n:]
            u0t_s[ci, hh] = tr[:n] + tr[n:2 * n]
            vt_s[ci, hh] = tr[2 * n:].astype(BF16)
        return carry

    lax.fori_loop(0, n_chunks // SCAN_GROUP, local, 0)

    rk = rk_ref[...]
    lng = lg_ref[...]
    lnb = lb_ref[...]
    heads = range(nh)

    def finish(ci, os_):
        rows = pl.ds(pl.multiple_of(ci * c, c), c)
        r = r_ref[rows, :].astype(F32)
        k = k_ref[rows, :].astype(F32)
        v = v_ref[rows, :].astype(F32)
        ys = []
        for hh in heads:
            sl = lanes[hh]
            o = os_[hh]
            mean = jnp.mean(o, axis=-1, keepdims=True)
            var = jnp.mean(jnp.square(o - mean), axis=-1, keepdims=True)
            y = (o - mean) * lax.rsqrt(var + RWKV_LN_EPS) * lng[:, sl] + lnb[:, sl]
            bonus = jnp.sum(r[:, sl] * k[:, sl] * rk[:, sl], axis=-1, keepdims=True) * v[:, sl]
            ys.append(y + bonus)
        y_ref[rows, :] = jnp.concatenate(ys, axis=1).astype(y_ref.dtype)

    def sweep(ci, os_prev):
        g_last = g_s[ci]
        sbs = [state[hh].astype(BF16) for hh in heads]
        wss = [_dot_nt(wr_s[ci, hh], sbs[hh]) for hh in heads]
        uts = [_dot_nt(sbs[hh], wr_s[ci, hh, 0:c, :]) for hh in heads]
        finish(jnp.maximum(ci - 1, 0), os_prev)
        ubs = [(wss[hh][:c] + u0_s[ci, hh]).astype(BF16) for hh in heads]
        uvts = [jnp.concatenate([(uts[hh] + u0t_s[ci, hh]).astype(BF16), vt_s[ci, hh]], axis=1) for hh in heads]
        os_ = [wss[hh][c:] + o0_s[ci, hh] + _dot(arb_s[ci, hh], ubs[hh]) for hh in heads]
        for hh in heads:
            state[hh] = (state[hh] + _dot(uvts[hh], rhs_s[ci, hh])) * g_last[:, lanes[hh]]
        return tuple(os_)

    zeros = tuple(jnp.zeros((c, n), F32) for _ in heads)
    finish(n_chunks - 1, lax.fori_loop(0, n_chunks, sweep, zeros))


def _rwkv_scan(r, lw, k, v, kk, asg, r_k, ln_g, ln_b, batch, seq):
    t, d = r.shape
    rb = SCAN_ROWS
    w = SCAN_LANES
    nc = seq // rb
    nh = w // HEAD
    n_chunks = rb // CHUNK
    assert d % w == 0 and seq % rb == 0 and n_chunks % SCAN_GROUP == 0
    blk = pl.BlockSpec((rb, w), lambda b, hp, ci: (b * nc + ci, hp))
    vec = pl.BlockSpec((1, w), lambda b, hp, ci: (0, hp))
    per_unit = lambda rows, cols, dt: pltpu.VMEM((n_chunks, nh, rows, cols), dt)
    scratch = [pltpu.VMEM((nh, HEAD, HEAD), F32),
               per_unit(2 * CHUNK, HEAD, BF16),
               per_unit(CHUNK, HEAD, F32),
               per_unit(HEAD, CHUNK, F32),
               per_unit(HEAD, CHUNK, BF16),
               per_unit(CHUNK, HEAD, F32),
               per_unit(CHUNK, CHUNK, BF16),
               per_unit(2 * CHUNK, HEAD, BF16),
               pltpu.VMEM((n_chunks, 1, w), F32)]
    return pl.pallas_call(
        _rwkv_scan_body,
        grid=(batch, d // w, nc),
        in_specs=[blk] * 6 + [vec] * 3, out_specs=blk,
        out_shape=jax.ShapeDtypeStruct((t, d), BF16),
        scratch_shapes=scratch,
        compiler_params=_params("parallel", "parallel", "arbitrary"), name="rwkv_scan",
    )(r, lw, k, v, kk, asg, r_k.reshape(1, d), ln_g.reshape(1, d), ln_b.reshape(1, d))


def _linear_res_body(x_ref, a_ref, *rest, gated):
    if gated:
        g_ref, w_ref, o_ref = rest
        a = a_ref[...] * g_ref[...]
    else:
        w_ref, o_ref = rest
        a = a_ref[...]
    o_ref[...] = x_ref[...] + _dot(a, w_ref[...])


def _linear_res(x, a, w, gate=None):
    t, d = x.shape
    tm = ROW_TILE
    row = pl.BlockSpec((tm, d), lambda i: (i, 0))
    ins = [x, a] + ([gate] if gate is not None else []) + [w.astype(BF16)]
    return pl.pallas_call(
        functools.partial(_linear_res_body, gated=gate is not None),
        grid=(t // tm,), in_specs=[row] * (len(ins) - 1) + [_full(w.shape)], out_specs=row,
        out_shape=jax.ShapeDtypeStruct((t, d), F32),
        compiler_params=_params("parallel"), name="linear_res",
    )(*ins)


def _norm_linear_body(x_ref, g_ref, w_ref, o_ref):
    o_ref[...] = _dot(_rms(x_ref[...], g_ref[...]).astype(BF16), w_ref[...]).astype(o_ref.dtype)


def _norm_linear(x, g, w, out_dtype):
    t, d = x.shape
    tm = ROW_TILE
    return pl.pallas_call(
        _norm_linear_body, grid=(t // tm,),
        in_specs=[pl.BlockSpec((tm, d), lambda i: (i, 0)), _full((1, d)), _full(w.shape)],
        out_specs=pl.BlockSpec((tm, w.shape[1]), lambda i: (i, 0)),
        out_shape=jax.ShapeDtypeStruct((t, w.shape[1]), out_dtype),
        compiler_params=_params("parallel"), name="norm_linear",
    )(x, g.reshape(1, d), w.astype(BF16))


def _mem_attn_body(x_ref, kv_ref, gn_ref, wq_ref, wo_ref, gf_ref, wrh_ref, wrl_ref, br_ref, x_o, xn_o, lg_o):
    x = x_ref[...]
    d = x.shape[1]
    hd = d // MEM_HEADS
    q = _dot(_rms(x, gn_ref[...]).astype(BF16), wq_ref[...])
    outs = []
    for hh in range(MEM_HEADS):
        qh = q[:, hh * hd:(hh + 1) * hd].astype(BF16)
        kh = kv_ref[:, hh * hd:(hh + 1) * hd]
        vh = kv_ref[:, d + hh * hd:d + (hh + 1) * hd]
        s = _dot_nt(qh, kh)
        p = jnp.exp(s - jnp.max(s, axis=-1, keepdims=True))
        o = _dot(p.astype(BF16), vh) / jnp.sum(p, axis=-1, keepdims=True)
        outs.append(o.astype(BF16))
    xnew = x + _dot(jnp.concatenate(outs, axis=1), wo_ref[...])
    x_o[...] = xnew
    xn = _rms(xnew, gf_ref[...])
    xn_o[...] = xn
    xh = xn.astype(BF16)
    xl = (xn - xh.astype(F32)).astype(BF16)
    tm = x.shape[0]
    hw = _dot(jnp.concatenate([xh, xl], axis=0), wrh_ref[...])
    lg_o[...] = hw[:tm] + hw[tm:] + _dot(xh, wrl_ref[...]) + br_ref[...]


def _mem_attn(x, kv, seq, mem_tokens, gn, w_q, w_o, g_ffn, w_router, b_router):
    t, d = x.shape
    tm = ROW_TILE
    per_seq = seq // tm
    row = pl.BlockSpec((tm, d), lambda i: (i, 0))
    w_hi = w_router.astype(BF16)
    return pl.pallas_call(
        _mem_attn_body, grid=(t // tm,),
        in_specs=[row, pl.BlockSpec((mem_tokens, 2 * d), lambda i: (i // per_seq, 0)),
                  _full((1, d)), _full((d, d)), _full((d, d)), _full((1, d)),
                  _full((d, LANES)), _full((d, LANES)), _full((1, LANES))],
        out_specs=[row, row, pl.BlockSpec((tm, LANES), lambda i: (i, 0))],
        out_shape=[jax.ShapeDtypeStruct((t, d), F32), jax.ShapeDtypeStruct((t, d), F32),
                   jax.ShapeDtypeStruct((t, LANES), F32)],
        compiler_params=_params("parallel"), name="mem_attn",
    )(x, kv, gn.reshape(1, d), (w_q * (d // MEM_HEADS) ** -0.5).astype(BF16), w_o.astype(BF16),
      g_ffn.reshape(1, d), w_hi, (w_router - w_hi.astype(F32)).astype(BF16), b_router)


def _route_body(lg_ref, idx_o, gate_o, cnt_o, carry):
    @pl.when(pl.program_id(0) == 0)
    def _():
        carry[...] = jnp.zeros_like(carry)

    lg = lg_ref[...]
    tm = lg.shape[0]
    lane = lax.broadcasted_iota(I32, lg.shape, 1)
    lanef = lane.astype(F32)
    big = 1e9
    ninf = -jnp.inf
    gl = jnp.where(lane < N_GROUPS, lg, ninf)
    gmax = jnp.max(gl, axis=-1, keepdims=True)
    g_w = 1.0 / jnp.sum(jnp.exp(gl - gmax), axis=-1, keepdims=True)
    grp = jnp.min(jnp.where(gl == gmax, lanef, big), axis=-1, keepdims=True).astype(I32)
    in_grp = (lane >= N_GROUPS) & (lane < N_GROUPS + N_EXPERTS) & (((lane - N_GROUPS) >> 3) == grp)
    el = jnp.where(in_grp, lg, ninf)
    m1 = jnp.max(el, axis=-1, keepdims=True)
    i1 = jnp.min(jnp.where(el == m1, lanef, big), axis=-1, keepdims=True)
    el2 = jnp.where(lanef == i1, ninf, el)
    m2 = jnp.max(el2, axis=-1, keepdims=True)
    i2 = jnp.min(jnp.where(el2 == m2, lanef, big), axis=-1, keepdims=True)
    tt = jnp.exp(m2 - m1)
    w1 = 1.0 / (1.0 + tt)
    w2 = tt / (1.0 + tt)
    hit1 = lanef == i1
    hit2 = lanef == i2
    oh = jnp.where(hit1 | hit2, 1.0, 0.0)
    r_i = lax.broadcasted_iota(I32, (tm, tm), 0)
    c_i = lax.broadcasted_iota(I32, (tm, tm), 1)
    tri = jnp.where(r_i > c_i, 1.0, 0.0).astype(BF16)
    pos = _dot(tri, oh.astype(BF16)) + carry[...]
    pos1 = jnp.sum(jnp.where(hit1, pos, 0.0), axis=-1, keepdims=True)
    pos2 = jnp.sum(jnp.where(hit2, pos, 0.0), axis=-1, keepdims=True)
    carry[...] = carry[...] + jnp.sum(oh, axis=0, keepdims=True)
    cnt_o[...] = carry[...]
    idx = jnp.where(lane == 0, i1 - N_GROUPS, jnp.where(lane == 1, i2 - N_GROUPS,
          jnp.where(lane == 2, pos1, jnp.where(lane == 3, pos2, 0.0))))
    idx_o[...] = idx.astype(I32)
    gate_o[...] = jnp.where(lane == 0, g_w * w1, jnp.where(lane == 1, g_w * w2, 0.0))


def _route(logits):
    t = logits.shape[0]
    tm = ROW_TILE
    row = pl.BlockSpec((tm, LANES), lambda i: (i, 0))
    return pl.pallas_call(
        _route_body, grid=(t // tm,), in_specs=[row],
        out_specs=[row, row, _full((1, LANES))],
        out_shape=[jax.ShapeDtypeStruct((t, LANES), I32), jax.ShapeDtypeStruct((t, LANES), F32),
                   jax.ShapeDtypeStruct((1, LANES), F32)],
        scratch_shapes=[pltpu.VMEM((1, LANES), F32)],
        compiler_params=_params("arbitrary"), name="moe_route",
    )(logits)


def _row_copy(src, s, dst, d, sem):
    return pltpu.make_async_copy(src.at[pl.ds(s, 1)], dst.at[pl.ds(d, 1)], sem)


def _dispatch_body(dest_ref, xn_ref, xs_in, xs_out, sem):
    del xs_in
    tm = xn_ref.shape[0]

    def issue(r, c):
        _row_copy(xn_ref, r, xs_out, dest_ref[0, 0, r], sem).start()
        _row_copy(xn_ref, r, xs_out, dest_ref[0, 0, tm + r], sem).start()
        return c

    lax.fori_loop(0, tm, issue, 0, unroll=DMA_UNROLL)
    for _ in range(2):
        pltpu.make_async_copy(xn_ref, xs_out.at[pl.ds(0, tm)], sem).wait()


def _dispatch(xn, dest, cap):
    t, d = xn.shape
    tm = ROW_TILE
    return pl.pallas_call(
        _dispatch_body, grid=(t // tm,),
        in_specs=[pl.BlockSpec((1, 1, 2 * tm), lambda i: (i, 0, 0), memory_space=pltpu.SMEM),
                  pl.BlockSpec((tm, d), lambda i: (i, 0)),
                  pl.BlockSpec(memory_space=pl.ANY)],
        out_specs=pl.BlockSpec(memory_space=pl.ANY),
        out_shape=jax.ShapeDtypeStruct((cap, d), F32),
        scratch_shapes=[pltpu.SemaphoreType.DMA],
        input_output_aliases={2: 0},
        compiler_params=_params("arbitrary"), name="moe_dispatch",
    )(dest, xn, jnp.zeros((cap, d), F32))


def _experts_body(be_ref, nb_ref, xs_ref, wgu_ref, wd_ref, ys_ref):
    del be_ref
    live = pl.program_id(0) < nb_ref[0]

    @pl.when(live)
    def _():
        gu = _dot(xs_ref[...].astype(BF16), wgu_ref[0])
        f = gu.shape[1] // 2
        gt = gu[:, :f]
        hid = gt * _sigmoid(gt) * gu[:, f:]
        ys_ref[...] = _dot(hid.astype(BF16), wd_ref[0])

    @pl.when(jnp.logical_not(live))
    def _():
        ys_ref[...] = jnp.zeros_like(ys_ref)


def _experts(xs, block_expert, n_live, w_gate_up, w_down):
    cap, d = xs.shape
    f2 = w_gate_up.shape[2]
    blk = MOE_BLOCK
    grid_spec = pltpu.PrefetchScalarGridSpec(
        num_scalar_prefetch=2, grid=(cap // blk,),
        in_specs=[pl.BlockSpec((blk, d), lambda i, be, nb: (i, 0)),
                  pl.BlockSpec((1, d, f2), lambda i, be, nb: (be[i], 0, 0)),
                  pl.BlockSpec((1, f2 // 2, d), lambda i, be, nb: (be[i], 0, 0))],
        out_specs=pl.BlockSpec((blk, d), lambda i, be, nb: (i, 0)))
    return pl.pallas_call(
        _experts_body, grid_spec=grid_spec, out_shape=jax.ShapeDtypeStruct((cap, d), F32),
        compiler_params=_params("arbitrary"), name="moe_experts",
    )(block_expert, n_live, xs, w_gate_up, w_down)


def _combine_body(dest_ref, x_ref, gate_ref, ys_ref, *rest, n_proj, final_norm):
    norm_refs = rest[:n_proj + (1 if final_norm else 0)]
    w_refs = rest[len(norm_refs):len(norm_refs) + n_proj]
    outs = rest[len(norm_refs) + n_proj:-3]
    buf1, buf2, sem = rest[-3:]
    tm = x_ref.shape[0]

    def issue(r, c):
        _row_copy(ys_ref, dest_ref[0, 0, r], buf1, r, sem).start()
        _row_copy(ys_ref, dest_ref[0, 0, tm + r], buf2, r, sem).start()
        return c

    lax.fori_loop(0, tm, issue, 0, unroll=DMA_UNROLL)
    for buf in (buf1, buf2):
        pltpu.make_async_copy(ys_ref.at[pl.ds(0, tm)], buf, sem).wait()
    gate = gate_ref[...]
    x = x_ref[...] + gate[:, 0:1] * buf1[...] + gate[:, 1:2] * buf2[...]
    if final_norm:
        outs[0][...] = _rms(x, norm_refs[0][...])
        return
    outs[0][...] = x
    for j in range(n_proj):
        outs[1 + j][...] = _dot(_rms(x, norm_refs[j][...]).astype(BF16), w_refs[j][...]).astype(outs[1 + j].dtype)


def _combine(x, ys, dest, gates, norms, weights, final_norm=False):
    t, d = x.shape
    tm = ROW_TILE
    row = pl.BlockSpec((tm, d), lambda i: (i, 0))
    in_specs = [pl.BlockSpec((1, 1, 2 * tm), lambda i: (i, 0, 0), memory_space=pltpu.SMEM),
                row, pl.BlockSpec((tm, LANES), lambda i: (i, 0)), pl.BlockSpec(memory_space=pl.ANY)]
    in_specs += [_full((1, d))] * len(norms) + [_full(w.shape) for w in weights]
    out_specs = [row] + [pl.BlockSpec((tm, w.shape[1]), lambda i: (i, 0)) for w in weights]
    out_shape = [jax.ShapeDtypeStruct((t, d), F32)]
    out_shape += [jax.ShapeDtypeStruct((t, w.shape[1]), BF16) for w in weights]
    return pl.pallas_call(
        functools.partial(_combine_body, n_proj=len(weights), final_norm=final_norm),
        grid=(t // tm,), in_specs=in_specs, out_specs=out_specs, out_shape=out_shape,
        scratch_shapes=[pltpu.VMEM((tm, d), F32), pltpu.VMEM((tm, d), F32), pltpu.SemaphoreType.DMA],
        compiler_params=_params("arbitrary"), name="moe_combine",
    )(dest, x, gates, ys, *[g.reshape(1, d) for g in norms], *[w.astype(BF16) for w in weights])


def _moe(x, xn, logits, w_gate, w_up, w_down, norms, weights, final_norm=False):
    t, d = x.shape
    blk = MOE_BLOCK
    tm = ROW_TILE
    idx, gates, counts = _route(logits)
    counts = counts[0, N_GROUPS:N_GROUPS + N_EXPERTS].astype(I32)
    padded = (counts + blk - 1) // blk * blk
    pend = jnp.cumsum(padded)
    pstart = pend - padded
    cap = 2 * t + N_EXPERTS * blk
    n_blocks = cap // blk
    starts = jnp.arange(n_blocks, dtype=I32) * blk
    block_expert = jnp.minimum(jnp.sum(pend[None, :] <= starts[:, None], axis=1), N_EXPERTS - 1).astype(I32)
    n_live = (pend[-1:] // blk).astype(I32)
    dest = jnp.take(pstart, idx[:, 0:2], axis=0) + idx[:, 2:4]
    dest = dest.reshape(t // tm, tm, 2).transpose(0, 2, 1).reshape(t // tm, 1, 2 * tm)
    xs = _dispatch(xn, dest, cap)
    w_gu = jnp.concatenate([w_gate, w_up], axis=2).astype(BF16)
    ys = _experts(xs, block_expert, n_live, w_gu, w_down.astype(BF16))
    return _combine(x, ys, dest, gates, norms, weights, final_norm)


def _sb_attn_body(q_ref, k_ref, v_ref, tri_ref, o_ref):
    i = pl.program_id(2)
    blk, win = SB_BLOCK, SB_WINDOW
    nh = q_ref.shape[1] // HEAD
    qpos = i * blk + lax.broadcasted_iota(I32, (blk, win), 0)
    col = lax.broadcasted_iota(I32, (blk, win), 1)
    neg_suffix = tri_ref[...]
    lanes = [slice(h * HEAD, (h + 1) * HEAD) for h in range(nh)]
    qs = [(q_ref[:, sl].astype(F32) * (HEAD ** -0.5 * LOG2E)).astype(BF16) for sl in lanes]
    u32 = jnp.uint32

    def more(carry):
        hi, go = carry[0], carry[1]
        return jnp.logical_and(hi > 0, go)

    def window(carry):
        hi, _, accs, rests = carry
        lo = jnp.maximum(hi - win, 0)
        rows = pl.ds(pl.multiple_of(lo, blk), win)
        valid = (lo + col) < jnp.minimum(hi, qpos)
        zs, parts = [], []
        for h in range(nh):
            z = _dot_nt(qs[h], k_ref[rows, lanes[h]])
            neg_abs = lax.bitcast_convert_type(lax.bitcast_convert_type(z, u32) | u32(0x80000000), F32)
            sp = jnp.maximum(z, 0.0) + jnp.log(1.0 + jnp.exp2(neg_abs)) * LOG2E
            sp = jnp.where(valid, sp, 0.0)
            top = lax.bitcast_convert_type(lax.bitcast_convert_type(sp, u32) & u32(0xFFFF0000), F32)
            zs.append(z)
            parts += [top.astype(BF16), (sp - top).astype(BF16)]
        sums = _dot(jnp.concatenate(parts, axis=0), neg_suffix)
        new_accs, new_rests = [], []
        for h in range(nh):
            sfx = sums[2 * h * blk:(2 * h + 1) * blk] + sums[(2 * h + 1) * blk:(2 * h + 2) * blk]
            attn = jnp.where(valid, jnp.exp2(zs[h] + sfx + rests[h]), 0.0)
            new_accs.append(accs[h] + _dot(attn.astype(BF16), v_ref[rows, lanes[h]]))
            new_rests.append(rests[h] + sfx[:, 0:1])
        worst = functools.reduce(jnp.maximum, new_rests)
        return lo, jnp.max(worst) > SB_ZERO_LOG2, tuple(new_accs), tuple(new_rests)

    init = ((i + 1) * blk, True, tuple(jnp.zeros((blk, HEAD), F32) for _ in range(nh)),
            tuple(jnp.zeros((blk, 1), F32) for _ in range(nh)))
    accs = lax.while_loop(more, window, init)[2]
    o_ref[...] = jnp.concatenate(accs, axis=1).astype(o_ref.dtype)


def _sb_attn(q, kv, batch, seq):
    t, d = q.shape
    nq = seq // SB_BLOCK
    w = SB_LANES
    hp = d // w
    win = SB_WINDOW
    assert d % w == 0 and seq % SB_BLOCK == 0 and seq >= win
    tri = -jnp.tril(jnp.ones((win, win), F32)).astype(BF16)
    return pl.pallas_call(
        _sb_attn_body, grid=(batch, hp, nq),
        in_specs=[pl.BlockSpec((SB_BLOCK, w), lambda b, h, i: (b * nq + i, h)),
                  pl.BlockSpec((seq, w), lambda b, h, i: (b, h)),
                  pl.BlockSpec((seq, w), lambda b, h, i: (b, hp + h)),
                  pl.BlockSpec((win, win), lambda b, h, i: (0, 0))],
        out_specs=pl.BlockSpec((SB_BLOCK, w), lambda b, h, i: (b * nq + i, h)),
        out_shape=jax.ShapeDtypeStruct((t, d), BF16),
        compiler_params=_params("parallel", "parallel", "arbitrary"), name="sb_attn",
    )(q, kv, kv, tri)


def _router_weights(w_group, b_group, w_expert, b_expert):
    d = w_group.shape[0]
    pad = LANES - N_GROUPS - N_EXPERTS
    w = jnp.concatenate([w_group, w_expert, jnp.zeros((d, pad), F32)], axis=1)
    b = jnp.concatenate([b_group, b_expert, jnp.zeros((pad,), F32)]).reshape(1, LANES)
    return w, b


def kernel(x, mem, norm_mix, norm_mem, norm_ffn, norm_final, rwkv_mu, rwkv_w_rkv, rwkv_w0, rwkv_w1, rwkv_w2, rwkv_a0, rwkv_a1, rwkv_a2, rwkv_g1, rwkv_g2, rwkv_k_k, rwkv_k_a, rwkv_r_k, rwkv_ln_g, rwkv_ln_b, rwkv_w_o, kv_norm, w_kv_shared, sb_w_q, sb_w_o, mem_norm_kv, mem_w_q, mem_w_kv, mem_w_o, moe_w_group, moe_b_group, moe_w_expert, moe_b_expert, moe_w_gate, moe_w_up, moe_w_down):
    batch, seq, d = x.shape
    depth = norm_mix.shape[0]
    n_a = rwkv_mu.shape[0]
    mem_tokens = mem.shape[1]
    xs = x.reshape(batch * seq, d)
    memf = mem.reshape(batch * mem_tokens, d)
    q = kv = None
    for l in range(depth):
        if l < n_a:
            r, lw, k, v, kk, asg, g = _rwkv_proj(
                xs, seq, norm_mix[l], rwkv_mu[l], rwkv_w_rkv[l], rwkv_w0[l], rwkv_w1[l], rwkv_w2[l],
                rwkv_a0[l], rwkv_a1[l], rwkv_a2[l], rwkv_g1[l], rwkv_g2[l], rwkv_k_k[l], rwkv_k_a[l])
            y = _rwkv_scan(r, lw, k, v, kk, asg, rwkv_r_k[l], rwkv_ln_g[l], rwkv_ln_b[l], batch, seq)
            xs = _linear_res(xs, y, rwkv_w_o[l], gate=g)
        else:
            o = _sb_attn(q, kv, batch, seq)
            xs = _linear_res(xs, o, sb_w_o[l - n_a])
        mkv = _norm_linear(memf, mem_norm_kv[l], mem_w_kv[l], BF16)
        w_router, b_router = _router_weights(moe_w_group[l], moe_b_group[l], moe_w_expert[l], moe_b_expert[l])
        xs, xn, logits = _mem_attn(xs, mkv, seq, mem_tokens, norm_mem[l], mem_w_q[l], mem_w_o[l],
                                   norm_ffn[l], w_router, b_router)
        last = l == depth - 1
        if last:
            norms, weights = [norm_final], []
        elif l + 1 < n_a:
            norms, weights = [], []
        else:
            norms, weights = [norm_mix[l + 1]], [sb_w_q[l + 1 - n_a]]
            if l == n_a - 1:
                norms, weights = norms + [kv_norm], weights + [w_kv_shared]
        res = _moe(xs, xn, logits, moe_w_gate[l], moe_w_up[l], moe_w_down[l], norms, weights, final_norm=last)
        if last:
            return res[0].reshape(batch, seq, d)
        xs = res[0]
        if weights:
            q = res[1]
            if l == n_a - 1:
                kv = res[2]
```

```python
import functools

import jax
import jax.numpy as jnp
from jax import lax
from jax.experimental import pallas as pl
from jax.experimental.pallas import tpu as pltpu

F32 = jnp.float32
BF16 = jnp.bfloat16
I32 = jnp.int32
HIGHEST = lax.Precision.HIGHEST

HEAD = 64
MEM_HEADS = 4
N_GROUPS = 4
EXPERTS_PER_GROUP = 8
N_EXPERTS = N_GROUPS * EXPERTS_PER_GROUP
NORM_EPS = 1e-6
RWKV_LN_EPS = 64e-5

LANES = 128
CHUNK = 64
SB_BLOCK = 128
SB_WINDOW = 256
SB_LANES = 512
SB_ZERO_LOG2 = -160.0
LOG2E = 1.4426950408889634
MOE_BLOCK = 512
DMA_UNROLL = 8
ROW_TILE = 512
PROJ_TILE = 256
SCAN_ROWS = 512
SCAN_LANES = 256
SCAN_GROUP = 8
VMEM_LIMIT = 56 * 1024 * 1024


def _params(*sem):
    return pltpu.CompilerParams(dimension_semantics=sem, vmem_limit_bytes=VMEM_LIMIT)


def _dot(a, b):
    return jnp.dot(a, b, preferred_element_type=F32)


def _dot_nt(a, b):
    return lax.dot_general(a, b, (((1,), (1,)), ((), ())), preferred_element_type=F32)


def _dot_tn(a, b):
    return lax.dot_general(a, b, (((0,), (0,)), ((), ())), preferred_element_type=F32)


def _rms(x, g):
    ms = jnp.mean(x * x, axis=-1, keepdims=True)
    return x * lax.rsqrt(ms + NORM_EPS) * g


def _sigmoid(x):
    return 1.0 / (1.0 + jnp.exp(-x))


def _softplus(x):
    return jnp.maximum(x, 0.0) + jnp.log(1.0 + jnp.exp(-jnp.abs(x)))


def _full(shape):
    return pl.BlockSpec(shape, lambda *_: (0,) * len(shape))


def _rwkv_proj_body(x_ref, xp_ref, gn_ref, mu_ref, wr_ref, wk_ref, wv_ref, w1_ref, a1_ref, g1_ref,
                    w2_ref, a2_ref, g2_ref, vec_ref,
                    r_o, lw_o, k_o, v_o, kk_o, as_o, g_o, *, tiles_per_seq):
    i = pl.program_id(0)
    gn = gn_ref[...]
    h = _rms(x_ref[...], gn)
    hp = _rms(xp_ref[...], gn)[7:8, :]
    hp = jnp.where(lax.rem(i, tiles_per_seq) == 0, 0.0, hp)
    row = lax.broadcasted_iota(I32, (h.shape[0], 1), 0)
    hprev = jnp.where(row == 0, hp, pltpu.roll(h, 1, 0))
    xx = hprev - h
    mu = mu_ref[...]

    def lerp(j):
        return (h + xx * mu[j:j + 1, :]).astype(BF16)

    vec = vec_ref[...]
    r = _dot(lerp(0), wr_ref[...])
    k = _dot(lerp(1), wk_ref[...])
    v = _dot(lerp(2), wv_ref[...])
    w = vec[0:1] + _dot(jnp.tanh(_dot(lerp(3), w1_ref[...])).astype(BF16), w2_ref[...])
    w = -_softplus(-w) - 0.5
    a = _sigmoid(vec[1:2] + _dot(_dot(lerp(4), a1_ref[...]).astype(BF16), a2_ref[...]))
    g = _dot(_sigmoid(_dot(lerp(5), g1_ref[...])).astype(BF16), g2_ref[...])
    r_o[...] = r.astype(BF16)
    lw_o[...] = -jnp.exp(w)
    k_o[...] = (k * (1.0 + (a - 1.0) * vec[3:4])).astype(BF16)
    v_o[...] = v.astype(BF16)
    kk_o[...] = (k * vec[2:3]).astype(BF16)
    as_o[...] = a.astype(BF16)
    g_o[...] = g.astype(BF16)


def _rwkv_proj(x, seq, gn, mu, w_rkv, w0, w1, w2, a0, a1, a2, g1, g2, k_k, k_a):
    t, d = x.shape
    tm = PROJ_TILE
    pad8 = lambda rows: jnp.concatenate([rows, jnp.zeros((8 - rows.shape[0], d), F32)], axis=0)
    vec = pad8(jnp.stack([w0, a0, k_k, k_a]))
    mu8 = pad8(mu)
    wts = [w_rkv[0], w_rkv[1], w_rkv[2], w1, a1, g1, w2, a2, g2]
    wts = [w.astype(BF16) for w in wts]
    row_spec = pl.BlockSpec((tm, d), lambda i: (i, 0))
    prev_spec = pl.BlockSpec((8, d), lambda i: (jnp.maximum(i * (tm // 8) - 1, 0), 0))
    in_specs = [row_spec, prev_spec, _full((1, d)), _full((8, d))]
    in_specs += [_full(w.shape) for w in wts] + [_full((8, d))]
    outs = [jax.ShapeDtypeStruct((t, d), dt) for dt in (BF16, F32, BF16, BF16, BF16, BF16, BF16)]
    return pl.pallas_call(
        functools.partial(_rwkv_proj_body, tiles_per_seq=seq // tm),
        grid=(t // tm,), in_specs=in_specs, out_specs=[row_spec] * 7, out_shape=outs,
        compiler_params=_params("parallel"), name="rwkv_proj",
    )(x, x, gn.reshape(1, d), mu8, *wts, vec)


def _cumsum_rows(x):
    n = x.shape[0]
    row = lax.broadcasted_iota(I32, (n, 1), 0)
    s = 1
    while s < n:
        x = x + jnp.where(row >= s, pltpu.roll(x, s, 0), 0.0)
        s *= 2
    return x


def _rwkv_scan_body(r_ref, lw_ref, k_ref, v_ref, kk_ref, as_ref, rk_ref, lg_ref, lb_ref, y_ref,
                    state, wr_s, u0_s, u0t_s, vt_s, o0_s, arb_s, rhs_s, g_s):
    c = CHUNK
    n = HEAD
    nh = r_ref.shape[1] // n
    n_chunks = r_ref.shape[0] // c

    @pl.when(pl.program_id(2) == 0)
    def _():
        state[...] = jnp.zeros_like(state)

    row = lax.broadcasted_iota(I32, (c, c), 0)
    col = lax.broadcasted_iota(I32, (c, c), 1)
    strict = row > col
    incl = row >= col
    eye = (row == col).astype(F32)
    eye_b = eye.astype(BF16)
    lanes = [slice(hh * n, (hh + 1) * n) for hh in range(nh)]

    def local(cj, carry):
        units, a_ts, r_ts, lhss, rhss, vrefs = [], [], [], [], [], []
        for sub in range(SCAN_GROUP):
            ci = cj * SCAN_GROUP + sub
            rows = pl.ds(pl.multiple_of(ci * c, c), c)
            r = r_ref[rows, :].astype(F32)
            lw = lw_ref[rows, :]
            k = k_ref[rows, :].astype(F32)
            kkr = kk_ref[rows, :].astype(F32)
            asg = as_ref[rows, :].astype(F32)
            cum = _cumsum_rows(lw)
            g_in = jnp.exp(cum)
            g_ex = jnp.exp(cum - lw)
            g_inv = jnp.exp(-cum)
            g_s[ci] = g_in[c - 1:c, :]
            for hh, sl in enumerate(lanes):
                kh = kkr[:, sl]
                kkn = kh * lax.rsqrt(jnp.maximum(jnp.sum(kh * kh, axis=-1, keepdims=True), 1e-24))
                a_t = -kkn * g_ex[:, sl]
                b_t = kkn * asg[:, sl] * g_inv[:, sl]
                k_t = k[:, sl] * g_inv[:, sl]
                r_t = (r[:, sl] * g_in[:, sl]).astype(BF16)
                units.append((ci, hh))
                a_ts.append(a_t)
                r_ts.append(r_t)
                lhss.append(jnp.concatenate([a_t.astype(BF16), r_t], axis=0))
                rhss.append(jnp.concatenate([b_t, k_t], axis=0).astype(BF16))
                vrefs.append((rows, sl))
        ps = [_dot_nt(lhs, rhs) for lhs, rhs in zip(lhss, rhss)]
        lps = [jnp.where(strict, p[:c, :c], 0.0) for p in ps]
        xos = [_dot(jnp.concatenate([jnp.where(strict, p[:c, c:], 0.0), jnp.where(incl, p[c:, c:], 0.0)],
                                    axis=0).astype(BF16), v_ref[rs, sl]) for p, (rs, sl) in zip(ps, vrefs)]
        for (ci, hh), p, rhs, xo in zip(units, ps, rhss, xos):
            arb_s[ci, hh] = jnp.where(incl, p[c:, :c], 0.0).astype(BF16)
            rhs_s[ci, hh] = rhs
            o0_s[ci, hh] = xo[c:]
        tinvs = [eye + lp for lp in lps]
        s = 2
        while s < c:
            lpbs = [lp.astype(BF16) for lp in lps]
            lps = [_dot(lpb, lpb) for lpb in lpbs]
            tinvs = [t + _dot(t.astype(BF16), lp.astype(BF16)) for t, lp in zip(tinvs, lps)]
            s *= 2
        wus = [_dot(t.astype(BF16), jnp.concatenate([a_t, xo[:c]], axis=1).astype(BF16))
               for t, a_t, xo in zip(tinvs, a_ts, xos)]
        trs = []
        for wu, (rs, sl) in zip(wus, vrefs):
            u0 = wu[:, n:]
            top = u0.astype(BF16)
            low = (u0 - top.astype(F32)).astype(BF16)
            trs.append(_dot_tn(jnp.concatenate([top, low, v_ref[rs, sl]], axis=1), eye_b))
        for (ci, hh), wu, r_t, tr in zip(units, wus, r_ts, trs):
            wr_s[ci, hh] = jnp.concatenate([wu[:, :n].astype(BF16), r_t], axis=0)
            u0_s[ci, hh] = wu[:, n:]
            u0t_s[ci, hh] = tr[:n] + tr[n:2 * n]
            vt_s[ci, hh] = tr[2 * n:].astype(BF16)
        return carry

    lax.fori_loop(0, n_chunks // SCAN_GROUP, local, 0)

    rk = rk_ref[...]
    lng = lg_ref[...]
    lnb = lb_ref[...]
    heads = range(nh)

    def finish(ci, os_):
        rows = pl.ds(pl.multiple_of(ci * c, c), c)
        r = r_ref[rows, :].astype(F32)
        k = k_ref[rows, :].astype(F32)
        v = v_ref[rows, :].astype(F32)
        ys = []
        for hh in heads:
            sl = lanes[hh]
            o = os_[hh]
            mean = jnp.mean(o, axis=-1, keepdims=True)
            var = jnp.mean(jnp.square(o - mean), axis=-1, keepdims=True)
            y = (o - mean) * lax.rsqrt(var + RWKV_LN_EPS) * lng[:, sl] + lnb[:, sl]
            bonus = jnp.sum(r[:, sl] * k[:, sl] * rk[:, sl], axis=-1, keepdims=True) * v[:, sl]
            ys.append(y + bonus)
        y_ref[rows, :] = jnp.concatenate(ys, axis=1).astype(y_ref.dtype)

    def sweep(ci, os_prev):
        g_last = g_s[ci]
        sbs = [state[hh].astype(BF16) for hh in heads]
        wss = [_dot_nt(wr_s[ci, hh], sbs[hh]) for hh in heads]
        uts = [_dot_nt(sbs[hh], wr_s[ci, hh, 0:c, :]) for hh in heads]
        finish(jnp.maximum(ci - 1, 0), os_prev)
        ubs = [(wss[hh][:c] + u0_s[ci, hh]).astype(BF16) for hh in heads]
        uvts = [jnp.concatenate([(uts[hh] + u0t_s[ci, hh]).astype(BF16), vt_s[ci, hh]], axis=1) for hh in heads]
        os_ = [wss[hh][c:] + o0_s[ci, hh] + _dot(arb_s[ci, hh], ubs[hh]) for hh in heads]
        for hh in heads:
            state[hh] = (state[hh] + _dot(uvts[hh], rhs_s[ci, hh])) * g_last[:, lanes[hh]]
        return tuple(os_)

    zeros = tuple(jnp.zeros((c, n), F32) for _ in heads)
    finish(n_chunks - 1, lax.fori_loop(0, n_chunks, sweep, zeros))


def _rwkv_scan(r, lw, k, v, kk, asg, r_k, ln_g, ln_b, batch, seq):
    t, d = r.shape
    rb = SCAN_ROWS
    w = SCAN_LANES
    nc = seq // rb
    nh = w // HEAD
    n_chunks = rb // CHUNK
    assert d % w == 0 and seq % rb == 0 and n_chunks % SCAN_GROUP == 0
    blk = pl.BlockSpec((rb, w), lambda b, hp, ci: (b * nc + ci, hp))
    vec = pl.BlockSpec((1, w), lambda b, hp, ci: (0, hp))
    per_unit = lambda rows, cols, dt: pltpu.VMEM((n_chunks, nh, rows, cols), dt)
    scratch = [pltpu.VMEM((nh, HEAD, HEAD), F32),
               per_unit(2 * CHUNK, HEAD, BF16),
               per_unit(CHUNK, HEAD, F32),
               per_unit(HEAD, CHUNK, F32),
               per_unit(HEAD, CHUNK, BF16),
               per_unit(CHUNK, HEAD, F32),
               per_unit(CHUNK, CHUNK, BF16),
               per_unit(2 * CHUNK, HEAD, BF16),
               pltpu.VMEM((n_chunks, 1, w), F32)]
    return pl.pallas_call(
        _rwkv_scan_body,
        grid=(batch, d // w, nc),
        in_specs=[blk] * 6 + [vec] * 3, out_specs=blk,
        out_shape=jax.ShapeDtypeStruct((t, d), BF16),
        scratch_shapes=scratch,
        compiler_params=_params("parallel", "parallel", "arbitrary"), name="rwkv_scan",
    )(r, lw, k, v, kk, asg, r_k.reshape(1, d), ln_g.reshape(1, d), ln_b.reshape(1, d))


def _split_bf16(x):
    top = lax.bitcast_convert_type(lax.bitcast_convert_type(x, jnp.uint32) & jnp.uint32(0xFFFF0000), F32)
    return top.astype(BF16), (x - top).astype(BF16)


def _wide_scan_body(r_ref, lw_ref, k_ref, v_ref, kk_ref, as_ref, rk_ref, lg_ref, lb_ref, y_ref,
                    state, wr_s, u0_s, u0t_s, vt_s, o0_s, arb_s, rhs_s, g_s):
    c = CHUNK
    n = HEAD
    w = r_ref.shape[1]
    nh = w // n
    n_chunks = r_ref.shape[0] // c

    @pl.when(pl.program_id(2) == 0)
    def _():
        state[...] = jnp.zeros_like(state)

    row = lax.broadcasted_iota(I32, (c, w), 0)
    col = lax.broadcasted_iota(I32, (c, w), 1) & (n - 1)
    strict = row > col
    incl = row >= col
    eye = (row == col).astype(F32)
    same_head = (lax.broadcasted_iota(I32, (w, w), 0) // n) == (lax.broadcasted_iota(I32, (w, w), 1) // n)
    head_ones = jnp.where(same_head, 1.0, 0.0).astype(BF16)
    r64 = lax.broadcasted_iota(I32, (c, c), 0)
    c64 = lax.broadcasted_iota(I32, (c, c), 1)
    eye_c = jnp.where(r64 == c64, 1.0, 0.0).astype(BF16)

    def bd(x):
        return jnp.concatenate([x] * nh, axis=0) * head_ones

    def head_sums(*xs):
        parts = [p for x in xs for p in _split_bf16(x)]
        sums = _dot(jnp.concatenate(parts, axis=0), head_ones)
        return [sums[2 * j * c:(2 * j + 1) * c] + sums[(2 * j + 1) * c:(2 * j + 2) * c] for j in range(len(xs))]

    def local(cj, carry):
        cis, rowss, a_alls, r_alls, lhss, rhss = [], [], [], [], [], []
        for sub in range(SCAN_GROUP):
            ci = cj * SCAN_GROUP + sub
            rows = pl.ds(pl.multiple_of(ci * c, c), c)
            r = r_ref[rows, :].astype(F32)
            lw = lw_ref[rows, :]
            k = k_ref[rows, :].astype(F32)
            kkr = kk_ref[rows, :].astype(F32)
            asg = as_ref[rows, :].astype(F32)
            cum = _cumsum_rows(lw)
            g_in = jnp.exp(cum)
            g_ex = jnp.exp(cum - lw)
            g_inv = jnp.exp(-cum)
            g_s[ci] = g_in[c - 1:c, :]
            kkn = kkr * lax.rsqrt(jnp.maximum(head_sums(kkr * kkr)[0], 1e-24))
            a_all = (-kkn * g_ex).astype(BF16)
            b_all = (kkn * asg * g_inv).astype(BF16)
            k_all = (k * g_inv).astype(BF16)
            r_all = (r * g_in).astype(BF16)
            cis.append(ci)
            rowss.append(rows)
            a_alls.append(a_all)
            r_alls.append(r_all)
            lhss.append(jnp.concatenate([a_all, r_all], axis=0))
            rhss.append((b_all, k_all))
        pbs = [_dot_nt(lhs, bd(b)) for lhs, (b, _) in zip(lhss, rhss)]
        pks = [_dot_nt(lhs, bd(k)) for lhs, (_, k) in zip(lhss, rhss)]
        lps = [jnp.where(strict, pb[:c], 0.0) for pb in pbs]
        xos = [_dot(jnp.concatenate([jnp.where(strict, pk[:c], 0.0), jnp.where(incl, pk[c:], 0.0)],
                                    axis=0).astype(BF16), bd(v_ref[rows, :])) for pk, rows in zip(pks, rowss)]
        for ci, pb, xo, (b, k) in zip(cis, pbs, xos, rhss):
            arb_s[ci] = jnp.where(incl, pb[c:], 0.0).astype(BF16)
            rhs_s[ci] = jnp.concatenate([b, k], axis=0)
            o0_s[ci] = xo[c:]
        tinvs = [eye + lp for lp in lps]
        lpbs = [lp.astype(BF16) for lp in lps]
        lpbs = [_dot(lpb, bd(lpb)).astype(BF16) for lpb in lpbs]
        s = 4
        while s < c:
            both = [_dot(jnp.concatenate([t.astype(BF16), lpb], axis=0), bd(lpb)) for t, lpb in zip(tinvs, lpbs)]
            tinvs = [t + bo[:c] for t, bo in zip(tinvs, both)]
            lpbs = [bo[c:].astype(BF16) for bo in both]
            s *= 2
        tinvs = [t + _dot(t.astype(BF16), bd(lpb)) for t, lpb in zip(tinvs, lpbs)]
        tbs = [t.astype(BF16) for t in tinvs]
        ws_ = [_dot(tb, bd(a)) for tb, a in zip(tbs, a_alls)]
        u0s = [_dot(tb, bd(xo[:c].astype(BF16))) for tb, xo in zip(tbs, xos)]
        trs = []
        for u0, rows in zip(u0s, rowss):
            top, low = _split_bf16(u0)
            trs.append(_dot_tn(jnp.concatenate([top, low, v_ref[rows, :]], axis=1), eye_c))
        for ci, wm, r_all, u0, tr in zip(cis, ws_, r_alls, u0s, trs):
            wr_s[ci] = jnp.concatenate([wm.astype(BF16), r_all], axis=0)
            u0_s[ci] = u0
            u0t_s[ci] = tr[:w] + tr[w:2 * w]
            vt_s[ci] = tr[2 * w:].astype(BF16)
        return carry

    lax.fori_loop(0, n_chunks // SCAN_GROUP, local, 0)

    rk = rk_ref[...]
    lng = lg_ref[...]
    lnb = lb_ref[...]

    def finish(ci, o):
        rows = pl.ds(pl.multiple_of(ci * c, c), c)
        r = r_ref[rows, :].astype(F32)
        k = k_ref[rows, :].astype(F32)
        v = v_ref[rows, :].astype(F32)
        o_sum, bonus = head_sums(o, r * k * rk)
        dev = o - o_sum * (1.0 / n)
        var = head_sums(dev * dev)[0] * (1.0 / n)
        y = dev * lax.rsqrt(var + RWKV_LN_EPS) * lng + lnb
        y_ref[rows, :] = (y + bonus * v).astype(y_ref.dtype)

    def sweep(ci, o_prev):
        sb = state[...].astype(BF16)
        wr = wr_s[ci]
        ws = _dot_nt(wr, sb)
        ut = _dot_nt(sb, wr[:c])
        finish(jnp.maximum(ci - 1, 0), o_prev)
        ub = (ws[:c] + u0_s[ci]).astype(BF16)
        uvt = jnp.concatenate([(ut + u0t_s[ci]).astype(BF16), vt_s[ci]], axis=1)
        o = ws[c:] + o0_s[ci] + _dot(arb_s[ci], bd(ub))
        upd = _dot(uvt, rhs_s[ci])
        state[...] = (state[...] + jnp.where(same_head, upd, 0.0)) * g_s[ci]
        return o

    finish(n_chunks - 1, lax.fori_loop(0, n_chunks, sweep, jnp.zeros((c, w), F32)))


def _wide_scan(r, lw, k, v, kk, asg, r_k, ln_g, ln_b, batch, seq):
    t, d = r.shape
    rb = SCAN_ROWS
    w = SCAN_LANES
    nc = seq // rb
    n_chunks = rb // CHUNK
    assert d % w == 0 and seq % rb == 0 and n_chunks % SCAN_GROUP == 0
    blk = pl.BlockSpec((rb, w), lambda b, hp, ci: (b * nc + ci, hp))
    vec = pl.BlockSpec((1, w), lambda b, hp, ci: (0, hp))
    per_chunk = lambda rows, cols, dt: pltpu.VMEM((n_chunks, rows, cols), dt)
    scratch = [pltpu.VMEM((w, w), F32),
               per_chunk(2 * CHUNK, w, BF16),
               per_chunk(CHUNK, w, F32),
               per_chunk(w, CHUNK, F32),
               per_chunk(w, CHUNK, BF16),
               per_chunk(CHUNK, w, F32),
               per_chunk(CHUNK, w, BF16),
               per_chunk(2 * CHUNK, w, BF16),
               per_chunk(1, w, F32)]
    return pl.pallas_call(
        _wide_scan_body,
        grid=(batch, d // w, nc),
        in_specs=[blk] * 6 + [vec] * 3, out_specs=blk,
        out_shape=jax.ShapeDtypeStruct((t, d), BF16),
        scratch_shapes=scratch,
        compiler_params=_params("parallel", "parallel", "arbitrary"), name="rwkv_scan",
    )(r, lw, k, v, kk, asg, r_k.reshape(1, d), ln_g.reshape(1, d), ln_b.reshape(1, d))


def _linear_res_body(x_ref, a_ref, *rest, gated):
    if gated:
        g_ref, w_ref, o_ref = rest
        a = a_ref[...] * g_ref[...]
    else:
        w_ref, o_ref = rest
        a = a_ref[...]
    o_ref[...] = x_ref[...] + _dot(a, w_ref[...])


def _linear_res(x, a, w, gate=None):
    t, d = x.shape
    tm = ROW_TILE
    row = pl.BlockSpec((tm, d), lambda i: (i, 0))
    ins = [x, a] + ([gate] if gate is not None else []) + [w.astype(BF16)]
    return pl.pallas_call(
        functools.partial(_linear_res_body, gated=gate is not None),
        grid=(t // tm,), in_specs=[row] * (len(ins) - 1) + [_full(w.shape)], out_specs=row,
        out_shape=jax.ShapeDtypeStruct((t, d), F32),
        compiler_params=_params("parallel"), name="linear_res",
    )(*ins)


def _norm_linear_body(x_ref, g_ref, w_ref, o_ref):
    o_ref[...] = _dot(_rms(x_ref[...], g_ref[...]).astype(BF16), w_ref[...]).astype(o_ref.dtype)


def _norm_linear(x, g, w, out_dtype):
    t, d = x.shape
    tm = ROW_TILE
    return pl.pallas_call(
        _norm_linear_body, grid=(t // tm,),
        in_specs=[pl.BlockSpec((tm, d), lambda i: (i, 0)), _full((1, d)), _full(w.shape)],
        out_specs=pl.BlockSpec((tm, w.shape[1]), lambda i: (i, 0)),
        out_shape=jax.ShapeDtypeStruct((t, w.shape[1]), out_dtype),
        compiler_params=_params("parallel"), name="norm_linear",
    )(x, g.reshape(1, d), w.astype(BF16))


def _mem_attn_body(x_ref, kv_ref, gn_ref, wq_ref, wo_ref, gf_ref, wrh_ref, wrl_ref, br_ref, x_o, xn_o, lg_o):
    x = x_ref[...]
    d = x.shape[1]
    hd = d // MEM_HEADS
    q = _dot(_rms(x, gn_ref[...]).astype(BF16), wq_ref[...])
    outs = []
    for hh in range(MEM_HEADS):
        qh = q[:, hh * hd:(hh + 1) * hd].astype(BF16)
        kh = kv_ref[:, hh * hd:(hh + 1) * hd]
        vh = kv_ref[:, d + hh * hd:d + (hh + 1) * hd]
        s = _dot_nt(qh, kh)
        p = jnp.exp(s - jnp.max(s, axis=-1, keepdims=True))
        o = _dot(p.astype(BF16), vh) / jnp.sum(p, axis=-1, keepdims=True)
        outs.append(o.astype(BF16))
    xnew = x + _dot(jnp.concatenate(outs, axis=1), wo_ref[...])
    x_o[...] = xnew
    xn = _rms(xnew, gf_ref[...])
    xn_o[...] = xn
    xh = xn.astype(BF16)
    xl = (xn - xh.astype(F32)).astype(BF16)
    tm = x.shape[0]
    hw = _dot(jnp.concatenate([xh, xl], axis=0), wrh_ref[...])
    lg_o[...] = hw[:tm] + hw[tm:] + _dot(xh, wrl_ref[...]) + br_ref[...]


def _mem_attn(x, kv, seq, mem_tokens, gn, w_q, w_o, g_ffn, w_router, b_router):
    t, d = x.shape
    tm = ROW_TILE
    per_seq = seq // tm
    row = pl.BlockSpec((tm, d), lambda i: (i, 0))
    w_hi = w_router.astype(BF16)
    return pl.pallas_call(
        _mem_attn_body, grid=(t // tm,),
        in_specs=[row, pl.BlockSpec((mem_tokens, 2 * d), lambda i: (i // per_seq, 0)),
                  _full((1, d)), _full((d, d)), _full((d, d)), _full((1, d)),
                  _full((d, LANES)), _full((d, LANES)), _full((1, LANES))],
        out_specs=[row, row, pl.BlockSpec((tm, LANES), lambda i: (i, 0))],
        out_shape=[jax.ShapeDtypeStruct((t, d), F32), jax.ShapeDtypeStruct((t, d), F32),
                   jax.ShapeDtypeStruct((t, LANES), F32)],
        compiler_params=_params("parallel"), name="mem_attn",
    )(x, kv, gn.reshape(1, d), (w_q * (d // MEM_HEADS) ** -0.5).astype(BF16), w_o.astype(BF16),
      g_ffn.reshape(1, d), w_hi, (w_router - w_hi.astype(F32)).astype(BF16), b_router)


def _route_body(lg_ref, idx_o, gate_o, cnt_o, carry):
    @pl.when(pl.program_id(0) == 0)
    def _():
        carry[...] = jnp.zeros_like(carry)

    lg = lg_ref[...]
    tm = lg.shape[0]
    lane = lax.broadcasted_iota(I32, lg.shape, 1)
    lanef = lane.astype(F32)
    big = 1e9
    ninf = -jnp.inf
    gl = jnp.where(lane < N_GROUPS, lg, ninf)
    gmax = jnp.max(gl, axis=-1, keepdims=True)
    g_w = 1.0 / jnp.sum(jnp.exp(gl - gmax), axis=-1, keepdims=True)
    grp = jnp.min(jnp.where(gl == gmax, lanef, big), axis=-1, keepdims=True).astype(I32)
    in_grp = (lane >= N_GROUPS) & (lane < N_GROUPS + N_EXPERTS) & (((lane - N_GROUPS) >> 3) == grp)
    el = jnp.where(in_grp, lg, ninf)
    m1 = jnp.max(el, axis=-1, keepdims=True)
    i1 = jnp.min(jnp.where(el == m1, lanef, big), axis=-1, keepdims=True)
    el2 = jnp.where(lanef == i1, ninf, el)
    m2 = jnp.max(el2, axis=-1, keepdims=True)
    i2 = jnp.min(jnp.where(el2 == m2, lanef, big), axis=-1, keepdims=True)
    tt = jnp.exp(m2 - m1)
    w1 = 1.0 / (1.0 + tt)
    w2 = tt / (1.0 + tt)
    hit1 = lanef == i1
    hit2 = lanef == i2
    oh = jnp.where(hit1 | hit2, 1.0, 0.0)
    r_i = lax.broadcasted_iota(I32, (tm, tm), 0)
    c_i = lax.broadcasted_iota(I32, (tm, tm), 1)
    tri = jnp.where(r_i > c_i, 1.0, 0.0).astype(BF16)
    pos = _dot(tri, oh.astype(BF16)) + carry[...]
    pos1 = jnp.sum(jnp.where(hit1, pos, 0.0), axis=-1, keepdims=True)
    pos2 = jnp.sum(jnp.where(hit2, pos, 0.0), axis=-1, keepdims=True)
    carry[...] = carry[...] + jnp.sum(oh, axis=0, keepdims=True)
    cnt_o[...] = carry[...]
    idx = jnp.where(lane == 0, i1 - N_GROUPS, jnp.where(lane == 1, i2 - N_GROUPS,
          jnp.where(lane == 2, pos1, jnp.where(lane == 3, pos2, 0.0))))
    idx_o[...] = idx.astype(I32)
    gate_o[...] = jnp.where(lane == 0, g_w * w1, jnp.where(lane == 1, g_w * w2, 0.0))


def _route(logits):
    t = logits.shape[0]
    tm = ROW_TILE
    row = pl.BlockSpec((tm, LANES), lambda i: (i, 0))
    return pl.pallas_call(
        _route_body, grid=(t // tm,), in_specs=[row],
        out_specs=[row, row, _full((1, LANES))],
        out_shape=[jax.ShapeDtypeStruct((t, LANES), I32), jax.ShapeDtypeStruct((t, LANES), F32),
                   jax.ShapeDtypeStruct((1, LANES), F32)],
        scratch_shapes=[pltpu.VMEM((1, LANES), F32)],
        compiler_params=_params("arbitrary"), name="moe_route",
    )(logits)


def _row_copy(src, s, dst, d, sem):
    return pltpu.make_async_copy(src.at[pl.ds(s, 1)], dst.at[pl.ds(d, 1)], sem)


def _dispatch_body(dest_ref, xn_ref, xs_in, xs_out, sem):
    del xs_in
    tm = xn_ref.shape[0]

    def issue(r, c):
        _row_copy(xn_ref, r, xs_out, dest_ref[0, 0, r], sem).start()
        _row_copy(xn_ref, r, xs_out, dest_ref[0, 0, tm + r], sem).start()
        return c

    lax.fori_loop(0, tm, issue, 0, unroll=DMA_UNROLL)
    for _ in range(2):
        pltpu.make_async_copy(xn_ref, xs_out.at[pl.ds(0, tm)], sem).wait()


def _dispatch(xn, dest, cap):
    t, d = xn.shape
    tm = ROW_TILE
    return pl.pallas_call(
        _dispatch_body, grid=(t // tm,),
        in_specs=[pl.BlockSpec((1, 1, 2 * tm), lambda i: (i, 0, 0), memory_space=pltpu.SMEM),
                  pl.BlockSpec((tm, d), lambda i: (i, 0)),
                  pl.BlockSpec(memory_space=pl.ANY)],
        out_specs=pl.BlockSpec(memory_space=pl.ANY),
        out_shape=jax.ShapeDtypeStruct((cap, d), F32),
        scratch_shapes=[pltpu.SemaphoreType.DMA],
        input_output_aliases={2: 0},
        compiler_params=_params("arbitrary"), name="moe_dispatch",
    )(dest, xn, jnp.zeros((cap, d), F32))


def _experts_body(be_ref, nb_ref, xs_ref, wgu_ref, wd_ref, ys_ref):
    del be_ref
    live = pl.program_id(0) < nb_ref[0]

    @pl.when(live)
    def _():
        gu = _dot(xs_ref[...].astype(BF16), wgu_ref[0])
        f = gu.shape[1] // 2
        gt = gu[:, :f]
        hid = gt * _sigmoid(gt) * gu[:, f:]
        ys_ref[...] = _dot(hid.astype(BF16), wd_ref[0])

    @pl.when(jnp.logical_not(live))
    def _():
        ys_ref[...] = jnp.zeros_like(ys_ref)


def _experts(xs, block_expert, n_live, w_gate_up, w_down):
    cap, d = xs.shape
    f2 = w_gate_up.shape[2]
    blk = MOE_BLOCK
    grid_spec = pltpu.PrefetchScalarGridSpec(
        num_scalar_prefetch=2, grid=(cap // blk,),
        in_specs=[pl.BlockSpec((blk, d), lambda i, be, nb: (i, 0)),
                  pl.BlockSpec((1, d, f2), lambda i, be, nb: (be[i], 0, 0)),
                  pl.BlockSpec((1, f2 // 2, d), lambda i, be, nb: (be[i], 0, 0))],
        out_specs=pl.BlockSpec((blk, d), lambda i, be, nb: (i, 0)))
    return pl.pallas_call(
        _experts_body, grid_spec=grid_spec, out_shape=jax.ShapeDtypeStruct((cap, d), F32),
        compiler_params=_params("arbitrary"), name="moe_experts",
    )(block_expert, n_live, xs, w_gate_up, w_down)


def _combine_body(dest_ref, x_ref, gate_ref, ys_ref, *rest, n_proj, final_norm):
    norm_refs = rest[:n_proj + (1 if final_norm else 0)]
    w_refs = rest[len(norm_refs):len(norm_refs) + n_proj]
    outs = rest[len(norm_refs) + n_proj:-3]
    buf1, buf2, sem = rest[-3:]
    tm = x_ref.shape[0]

    def issue(r, c):
        _row_copy(ys_ref, dest_ref[0, 0, r], buf1, r, sem).start()
        _row_copy(ys_ref, dest_ref[0, 0, tm + r], buf2, r, sem).start()
        return c

    lax.fori_loop(0, tm, issue, 0, unroll=DMA_UNROLL)
    for buf in (buf1, buf2):
        pltpu.make_async_copy(ys_ref.at[pl.ds(0, tm)], buf, sem).wait()
    gate = gate_ref[...]
    x = x_ref[...] + gate[:, 0:1] * buf1[...] + gate[:, 1:2] * buf2[...]
    if final_norm:
        outs[0][...] = _rms(x, norm_refs[0][...])
        return
    outs[0][...] = x
    for j in range(n_proj):
        outs[1 + j][...] = _dot(_rms(x, norm_refs[j][...]).astype(BF16), w_refs[j][...]).astype(outs[1 + j].dtype)


def _combine(x, ys, dest, gates, norms, weights, final_norm=False):
    t, d = x.shape
    tm = ROW_TILE
    row = pl.BlockSpec((tm, d), lambda i: (i, 0))
    in_specs = [pl.BlockSpec((1, 1, 2 * tm), lambda i: (i, 0, 0), memory_space=pltpu.SMEM),
                row, pl.BlockSpec((tm, LANES), lambda i: (i, 0)), pl.BlockSpec(memory_space=pl.ANY)]
    in_specs += [_full((1, d))] * len(norms) + [_full(w.shape) for w in weights]
    out_specs = [row] + [pl.BlockSpec((tm, w.shape[1]), lambda i: (i, 0)) for w in weights]
    out_shape = [jax.ShapeDtypeStruct((t, d), F32)]
    out_shape += [jax.ShapeDtypeStruct((t, w.shape[1]), BF16) for w in weights]
    return pl.pallas_call(
        functools.partial(_combine_body, n_proj=len(weights), final_norm=final_norm),
        grid=(t // tm,), in_specs=in_specs, out_specs=out_specs, out_shape=out_shape,
        scratch_shapes=[pltpu.VMEM((tm, d), F32), pltpu.VMEM((tm, d), F32), pltpu.SemaphoreType.DMA],
        compiler_params=_params("arbitrary"), name="moe_combine",
    )(dest, x, gates, ys, *[g.reshape(1, d) for g in norms], *[w.astype(BF16) for w in weights])


def _moe(x, xn, logits, w_gate, w_up, w_down, norms, weights, final_norm=False):
    t, d = x.shape
    blk = MOE_BLOCK
    tm = ROW_TILE
    idx, gates, counts = _route(logits)
    counts = counts[0, N_GROUPS:N_GROUPS + N_EXPERTS].astype(I32)
    padded = (counts + blk - 1) // blk * blk
    pend = jnp.cumsum(padded)
    pstart = pend - padded
    cap = 2 * t + N_EXPERTS * blk
    n_blocks = cap // blk
    starts = jnp.arange(n_blocks, dtype=I32) * blk
    block_expert = jnp.minimum(jnp.sum(pend[None, :] <= starts[:, None], axis=1), N_EXPERTS - 1).astype(I32)
    n_live = (pend[-1:] // blk).astype(I32)
    dest = jnp.take(pstart, idx[:, 0:2], axis=0) + idx[:, 2:4]
    dest = dest.reshape(t // tm, tm, 2).transpose(0, 2, 1).reshape(t // tm, 1, 2 * tm)
    xs = _dispatch(xn, dest, cap)
    w_gu = jnp.concatenate([w_gate, w_up], axis=2).astype(BF16)
    ys = _experts(xs, block_expert, n_live, w_gu, w_down.astype(BF16))
    return _combine(x, ys, dest, gates, norms, weights, final_norm)


def _sb_attn_body(q_ref, k_ref, v_ref, tri_ref, o_ref):
    i = pl.program_id(2)
    blk, win = SB_BLOCK, SB_WINDOW
    nh = q_ref.shape[1] // HEAD
    qpos = i * blk + lax.broadcasted_iota(I32, (blk, win), 0)
    col = lax.broadcasted_iota(I32, (blk, win), 1)
    neg_suffix = tri_ref[...]
    lanes = [slice(h * HEAD, (h + 1) * HEAD) for h in range(nh)]
    qs = [(q_ref[:, sl].astype(F32) * (HEAD ** -0.5 * LOG2E)).astype(BF16) for sl in lanes]
    u32 = jnp.uint32

    def more(carry):
        hi, go = carry[0], carry[1]
        return jnp.logical_and(hi > 0, go)

    def window(carry):
        hi, _, accs, rests = carry
        lo = jnp.maximum(hi - win, 0)
        rows = pl.ds(pl.multiple_of(lo, blk), win)
        valid = (lo + col) < jnp.minimum(hi, qpos)
        zs, parts = [], []
        for h in range(nh):
            z = _dot_nt(qs[h], k_ref[rows, lanes[h]])
            neg_abs = lax.bitcast_convert_type(lax.bitcast_convert_type(z, u32) | u32(0x80000000), F32)
            sp = jnp.maximum(z, 0.0) + jnp.log(1.0 + jnp.exp2(neg_abs)) * LOG2E
            sp = jnp.where(valid, sp, 0.0)
            top = lax.bitcast_convert_type(lax.bitcast_convert_type(sp, u32) & u32(0xFFFF0000), F32)
            zs.append(z)
            parts += [top.astype(BF16), (sp - top).astype(BF16)]
        sums = _dot(jnp.concatenate(parts, axis=0), neg_suffix)
        new_accs, new_rests = [], []
        for h in range(nh):
            sfx = sums[2 * h * blk:(2 * h + 1) * blk] + sums[(2 * h + 1) * blk:(2 * h + 2) * blk]
            attn = jnp.where(valid, jnp.exp2(zs[h] + sfx + rests[h]), 0.0)
            new_accs.append(accs[h] + _dot(attn.astype(BF16), v_ref[rows, lanes[h]]))
            new_rests.append(rests[h] + sfx[:, 0:1])
        worst = functools.reduce(jnp.maximum, new_rests)
        return lo, jnp.max(worst) > SB_ZERO_LOG2, tuple(new_accs), tuple(new_rests)

    init = ((i + 1) * blk, True, tuple(jnp.zeros((blk, HEAD), F32) for _ in range(nh)),
            tuple(jnp.zeros((blk, 1), F32) for _ in range(nh)))
    accs = lax.while_loop(more, window, init)[2]
    o_ref[...] = jnp.concatenate(accs, axis=1).astype(o_ref.dtype)


def _sb_attn(q, kv, batch, seq):
    t, d = q.shape
    nq = seq // SB_BLOCK
    w = SB_LANES
    hp = d // w
    win = SB_WINDOW
    assert d % w == 0 and seq % SB_BLOCK == 0 and seq >= win
    tri = -jnp.tril(jnp.ones((win, win), F32)).astype(BF16)
    return pl.pallas_call(
        _sb_attn_body, grid=(batch, hp, nq),
        in_specs=[pl.BlockSpec((SB_BLOCK, w), lambda b, h, i: (b * nq + i, h)),
                  pl.BlockSpec((seq, w), lambda b, h, i: (b, h)),
                  pl.BlockSpec((seq, w), lambda b, h, i: (b, hp + h)),
                  pl.BlockSpec((win, win), lambda b, h, i: (0, 0))],
        out_specs=pl.BlockSpec((SB_BLOCK, w), lambda b, h, i: (b * nq + i, h)),
        out_shape=jax.ShapeDtypeStruct((t, d), BF16),
        compiler_params=_params("parallel", "parallel", "arbitrary"), name="sb_attn",
    )(q, kv, kv, tri)


def _router_weights(w_group, b_group, w_expert, b_expert):
    d = w_group.shape[0]
    pad = LANES - N_GROUPS - N_EXPERTS
    w = jnp.concatenate([w_group, w_expert, jnp.zeros((d, pad), F32)], axis=1)
    b = jnp.concatenate([b_group, b_expert, jnp.zeros((pad,), F32)]).reshape(1, LANES)
    return w, b


def kernel(x, mem, norm_mix, norm_mem, norm_ffn, norm_final, rwkv_mu, rwkv_w_rkv, rwkv_w0, rwkv_w1, rwkv_w2, rwkv_a0, rwkv_a1, rwkv_a2, rwkv_g1, rwkv_g2, rwkv_k_k, rwkv_k_a, rwkv_r_k, rwkv_ln_g, rwkv_ln_b, rwkv_w_o, kv_norm, w_kv_shared, sb_w_q, sb_w_o, mem_norm_kv, mem_w_q, mem_w_kv, mem_w_o, moe_w_group, moe_b_group, moe_w_expert, moe_b_expert, moe_w_gate, moe_w_up, moe_w_down):
    batch, seq, d = x.shape
    depth = norm_mix.shape[0]
    n_a = rwkv_mu.shape[0]
    mem_tokens = mem.shape[1]
    xs = x.reshape(batch * seq, d)
    memf = mem.reshape(batch * mem_tokens, d)
    q = kv = None
    for l in range(depth):
        if l < n_a:
            r, lw, k, v, kk, asg, g = _rwkv_proj(
                xs, seq, norm_mix[l], rwkv_mu[l], rwkv_w_rkv[l], rwkv_w0[l], rwkv_w1[l], rwkv_w2[l],
                rwkv_a0[l], rwkv_a1[l], rwkv_a2[l], rwkv_g1[l], rwkv_g2[l], rwkv_k_k[l], rwkv_k_a[l])
            y = _wide_scan(r, lw, k, v, kk, asg, rwkv_r_k[l], rwkv_ln_g[l], rwkv_ln_b[l], batch, seq)
            xs = _linear_res(xs, y, rwkv_w_o[l], gate=g)
        else:
            o = _sb_attn(q, kv, batch, seq)
            xs = _linear_res(xs, o, sb_w_o[l - n_a])
        mkv = _norm_linear(memf, mem_norm_kv[l], mem_w_kv[l], BF16)
        w_router, b_router = _router_weights(moe_w_group[l], moe_b_group[l], moe_w_expert[l], moe_b_expert[l])
        xs, xn, logits = _mem_attn(xs, mkv, seq, mem_tokens, norm_mem[l], mem_w_q[l], mem_w_o[l],
                                   norm_ffn[l], w_router, b_router)
        last = l == depth - 1
        if last:
            norms, weights = [norm_final], []
        elif l + 1 < n_a:
            norms, weights = [], []
        else:
            norms, weights = [norm_mix[l + 1]], [sb_w_q[l + 1 - n_a]]
            if l == n_a - 1:
                norms, weights = norms + [kv_norm], weights + [w_kv_shared]
        res = _moe(xs, xn, logits, moe_w_gate[l], moe_w_up[l], moe_w_down[l], norms, weights, final_norm=last)
        if last:
            return res[0].reshape(batch, seq, d)
        xs = res[0]
        if weights:
            q = res[1]
            if l == n_a - 1:
                kv = res[2]
```

```python
import functools

import jax
import jax.numpy as jnp
from jax import lax
from jax.experimental import pallas as pl
from jax.experimental.pallas import tpu as pltpu

F32 = jnp.float32
BF16 = jnp.bfloat16
I32 = jnp.int32
HIGHEST = lax.Precision.HIGHEST

HEAD = 64
MEM_HEADS = 4
N_GROUPS = 4
EXPERTS_PER_GROUP = 8
N_EXPERTS = N_GROUPS * EXPERTS_PER_GROUP
NORM_EPS = 1e-6
RWKV_LN_EPS = 64e-5

LANES = 128
CHUNK = 64
SB_BLOCK = 128
SB_WINDOW = 256
SB_LANES = 512
SB_ZERO_LOG2 = -160.0
SB_HIDDEN = -1e30
LOG2E = 1.4426950408889634
MOE_BLOCK = 512
DMA_UNROLL = 8
ROW_TILE = 512
PROJ_TILE = 256
SCAN_ROWS = 512
SCAN_LANES = 256
SCAN_GROUP = 8
VMEM_LIMIT = 56 * 1024 * 1024


def _params(*sem):
    return pltpu.CompilerParams(dimension_semantics=sem, vmem_limit_bytes=VMEM_LIMIT)


def _dot(a, b):
    return jnp.dot(a, b, preferred_element_type=F32)


def _dot_nt(a, b):
    return lax.dot_general(a, b, (((1,), (1,)), ((), ())), preferred_element_type=F32)


def _dot_tn(a, b):
    return lax.dot_general(a, b, (((0,), (0,)), ((), ())), preferred_element_type=F32)


def _rms(x, g):
    ms = jnp.mean(x * x, axis=-1, keepdims=True)
    return x * lax.rsqrt(ms + NORM_EPS) * g


def _sigmoid(x):
    return 1.0 / (1.0 + jnp.exp(-x))


def _softplus(x):
    return jnp.maximum(x, 0.0) + jnp.log(1.0 + jnp.exp(-jnp.abs(x)))


def _full(shape):
    return pl.BlockSpec(shape, lambda *_: (0,) * len(shape))


def _rwkv_proj_body(x_ref, xp_ref, gn_ref, mu_ref, wr_ref, wk_ref, wv_ref, w1_ref, a1_ref, g1_ref,
                    w2_ref, a2_ref, g2_ref, vec_ref,
                    r_o, lw_o, k_o, v_o, kk_o, as_o, g_o, *, tiles_per_seq):
    i = pl.program_id(0)
    gn = gn_ref[...]
    h = _rms(x_ref[...], gn)
    hp = _rms(xp_ref[...], gn)[7:8, :]
    hp = jnp.where(lax.rem(i, tiles_per_seq) == 0, 0.0, hp)
    row = lax.broadcasted_iota(I32, (h.shape[0], 1), 0)
    hprev = jnp.where(row == 0, hp, pltpu.roll(h, 1, 0))
    xx = hprev - h
    mu = mu_ref[...]

    def lerp(j):
        return (h + xx * mu[j:j + 1, :]).astype(BF16)

    vec = vec_ref[...]
    r = _dot(lerp(0), wr_ref[...])
    k = _dot(lerp(1), wk_ref[...])
    v = _dot(lerp(2), wv_ref[...])
    w = vec[0:1] + _dot(jnp.tanh(_dot(lerp(3), w1_ref[...])).astype(BF16), w2_ref[...])
    w = -_softplus(-w) - 0.5
    a = _sigmoid(vec[1:2] + _dot(_dot(lerp(4), a1_ref[...]).astype(BF16), a2_ref[...]))
    g = _dot(_sigmoid(_dot(lerp(5), g1_ref[...])).astype(BF16), g2_ref[...])
    r_o[...] = r.astype(BF16)
    lw_o[...] = -jnp.exp(w)
    k_o[...] = (k * (1.0 + (a - 1.0) * vec[3:4])).astype(BF16)
    v_o[...] = v.astype(BF16)
    kk_o[...] = (k * vec[2:3]).astype(BF16)
    as_o[...] = a.astype(BF16)
    g_o[...] = g.astype(BF16)


def _rwkv_proj(x, seq, gn, mu, w_rkv, w0, w1, w2, a0, a1, a2, g1, g2, k_k, k_a):
    t, d = x.shape
    tm = PROJ_TILE
    pad8 = lambda rows: jnp.concatenate([rows, jnp.zeros((8 - rows.shape[0], d), F32)], axis=0)
    vec = pad8(jnp.stack([w0, a0, k_k, k_a]))
    mu8 = pad8(mu)
    wts = [w_rkv[0], w_rkv[1], w_rkv[2], w1, a1, g1, w2, a2, g2]
    wts = [w.astype(BF16) for w in wts]
    row_spec = pl.BlockSpec((tm, d), lambda i: (i, 0))
    prev_spec = pl.BlockSpec((8, d), lambda i: (jnp.maximum(i * (tm // 8) - 1, 0), 0))
    in_specs = [row_spec, prev_spec, _full((1, d)), _full((8, d))]
    in_specs += [_full(w.shape) for w in wts] + [_full((8, d))]
    outs = [jax.ShapeDtypeStruct((t, d), dt) for dt in (BF16, F32, BF16, BF16, BF16, BF16, BF16)]
    return pl.pallas_call(
        functools.partial(_rwkv_proj_body, tiles_per_seq=seq // tm),
        grid=(t // tm,), in_specs=in_specs, out_specs=[row_spec] * 7, out_shape=outs,
        compiler_params=_params("parallel"), name="rwkv_proj",
    )(x, x, gn.reshape(1, d), mu8, *wts, vec)


def _cumsum_rows(x):
    n = x.shape[0]
    row = lax.broadcasted_iota(I32, (n, 1), 0)
    s = 1
    while s < n:
        x = x + jnp.where(row >= s, pltpu.roll(x, s, 0), 0.0)
        s *= 2
    return x


def _rwkv_scan_body(r_ref, lw_ref, k_ref, v_ref, kk_ref, as_ref, rk_ref, lg_ref, lb_ref, y_ref,
                    state, wr_s, u0_s, u0t_s, vt_s, o0_s, arb_s, rhs_s, g_s):
    c = CHUNK
    n = HEAD
    nh = r_ref.shape[1] // n
    n_chunks = r_ref.shape[0] // c

    @pl.when(pl.program_id(2) == 0)
    def _():
        state[...] = jnp.zeros_like(state)

    row = lax.broadcasted_iota(I32, (c, c), 0)
    col = lax.broadcasted_iota(I32, (c, c), 1)
    strict = row > col
    incl = row >= col
    eye = (row == col).astype(F32)
    eye_b = eye.astype(BF16)
    lanes = [slice(hh * n, (hh + 1) * n) for hh in range(nh)]

    def local(cj, carry):
        units, a_ts, r_ts, lhss, rhss, vrefs = [], [], [], [], [], []
        for sub in range(SCAN_GROUP):
            ci = cj * SCAN_GROUP + sub
            rows = pl.ds(pl.multiple_of(ci * c, c), c)
            r = r_ref[rows, :].astype(F32)
            lw = lw_ref[rows, :]
            k = k_ref[rows, :].astype(F32)
            kkr = kk_ref[rows, :].astype(F32)
            asg = as_ref[rows, :].astype(F32)
            cum = _cumsum_rows(lw)
            g_in = jnp.exp(cum)
            g_ex = jnp.exp(cum - lw)
            g_inv = jnp.exp(-cum)
            g_s[ci] = g_in[c - 1:c, :]
            for hh, sl in enumerate(lanes):
                kh = kkr[:, sl]
                kkn = kh * lax.rsqrt(jnp.maximum(jnp.sum(kh * kh, axis=-1, keepdims=True), 1e-24))
                a_t = -kkn * g_ex[:, sl]
                b_t = kkn * asg[:, sl] * g_inv[:, sl]
                k_t = k[:, sl] * g_inv[:, sl]
                r_t = (r[:, sl] * g_in[:, sl]).astype(BF16)
                units.append((ci, hh))
                a_ts.append(a_t)
                r_ts.append(r_t)
                lhss.append(jnp.concatenate([a_t.astype(BF16), r_t], axis=0))
                rhss.append(jnp.concatenate([b_t, k_t], axis=0).astype(BF16))
                vrefs.append((rows, sl))
        ps = [_dot_nt(lhs, rhs) for lhs, rhs in zip(lhss, rhss)]
        lps = [jnp.where(strict, p[:c, :c], 0.0) for p in ps]
        xos = [_dot(jnp.concatenate([jnp.where(strict, p[:c, c:], 0.0), jnp.where(incl, p[c:, c:], 0.0)],
                                    axis=0).astype(BF16), v_ref[rs, sl]) for p, (rs, sl) in zip(ps, vrefs)]
        for (ci, hh), p, rhs, xo in zip(units, ps, rhss, xos):
            arb_s[ci, hh] = jnp.where(incl, p[c:, :c], 0.0).astype(BF16)
            rhs_s[ci, hh] = rhs
            o0_s[ci, hh] = xo[c:]
        tinvs = [eye + lp for lp in lps]
        s = 2
        while s < c:
            lpbs = [lp.astype(BF16) for lp in lps]
            lps = [_dot(lpb, lpb) for lpb in lpbs]
            tinvs = [t + _dot(t.astype(BF16), lp.astype(BF16)) for t, lp in zip(tinvs, lps)]
            s *= 2
        wus = [_dot(t.astype(BF16), jnp.concatenate([a_t, xo[:c]], axis=1).astype(BF16))
               for t, a_t, xo in zip(tinvs, a_ts, xos)]
        trs = []
        for wu, (rs, sl) in zip(wus, vrefs):
            u0 = wu[:, n:]
            top = u0.astype(BF16)
            low = (u0 - top.astype(F32)).astype(BF16)
            trs.append(_dot_tn(jnp.concatenate([top, low, v_ref[rs, sl]], axis=1), eye_b))
        for (ci, hh), wu, r_t, tr in zip(units, wus, r_ts, trs):
            wr_s[ci, hh] = jnp.concatenate([wu[:, :n].astype(BF16), r_t], axis=0)
            u0_s[ci, hh] = wu[:, n:]
            u0t_s[ci, hh] = tr[:n] + tr[n:2 * n]
            vt_s[ci, hh] = tr[2 * n:].astype(BF16)
        return carry

    lax.fori_loop(0, n_chunks // SCAN_GROUP, local, 0)

    rk = rk_ref[...]
    lng = lg_ref[...]
    lnb = lb_ref[...]
    heads = range(nh)

    def finish(ci, os_):
        rows = pl.ds(pl.multiple_of(ci * c, c), c)
        r = r_ref[rows, :].astype(F32)
        k = k_ref[rows, :].astype(F32)
        v = v_ref[rows, :].astype(F32)
        ys = []
        for hh in heads:
            sl = lanes[hh]
            o = os_[hh]
            mean = jnp.mean(o, axis=-1, keepdims=True)
            var = jnp.mean(jnp.square(o - mean), axis=-1, keepdims=True)
            y = (o - mean) * lax.rsqrt(var + RWKV_LN_EPS) * lng[:, sl] + lnb[:, sl]
            bonus = jnp.sum(r[:, sl] * k[:, sl] * rk[:, sl], axis=-1, keepdims=True) * v[:, sl]
            ys.append(y + bonus)
        y_ref[rows, :] = jnp.concatenate(ys, axis=1).astype(y_ref.dtype)

    def sweep(ci, os_prev):
        g_last = g_s[ci]
        sbs = [state[hh].astype(BF16) for hh in heads]
        wss = [_dot_nt(wr_s[ci, hh], sbs[hh]) for hh in heads]
        uts = [_dot_nt(sbs[hh], wr_s[ci, hh, 0:c, :]) for hh in heads]
        finish(jnp.maximum(ci - 1, 0), os_prev)
        ubs = [(wss[hh][:c] + u0_s[ci, hh]).astype(BF16) for hh in heads]
        uvts = [jnp.concatenate([(uts[hh] + u0t_s[ci, hh]).astype(BF16), vt_s[ci, hh]], axis=1) for hh in heads]
        os_ = [wss[hh][c:] + o0_s[ci, hh] + _dot(arb_s[ci, hh], ubs[hh]) for hh in heads]
        for hh in heads:
            state[hh] = (state[hh] + _dot(uvts[hh], rhs_s[ci, hh])) * g_last[:, lanes[hh]]
        return tuple(os_)

    zeros = tuple(jnp.zeros((c, n), F32) for _ in heads)
    finish(n_chunks - 1, lax.fori_loop(0, n_chunks, sweep, zeros))


def _rwkv_scan(r, lw, k, v, kk, asg, r_k, ln_g, ln_b, batch, seq):
    t, d = r.shape
    rb = SCAN_ROWS
    w = SCAN_LANES
    nc = seq // rb
    nh = w // HEAD
    n_chunks = rb // CHUNK
    assert d % w == 0 and seq % rb == 0 and n_chunks % SCAN_GROUP == 0
    blk = pl.BlockSpec((rb, w), lambda b, hp, ci: (b * nc + ci, hp))
    vec = pl.BlockSpec((1, w), lambda b, hp, ci: (0, hp))
    per_unit = lambda rows, cols, dt: pltpu.VMEM((n_chunks, nh, rows, cols), dt)
    scratch = [pltpu.VMEM((nh, HEAD, HEAD), F32),
               per_unit(2 * CHUNK, HEAD, BF16),
               per_unit(CHUNK, HEAD, F32),
               per_unit(HEAD, CHUNK, F32),
               per_unit(HEAD, CHUNK, BF16),
               per_unit(CHUNK, HEAD, F32),
               per_unit(CHUNK, CHUNK, BF16),
               per_unit(2 * CHUNK, HEAD, BF16),
               pltpu.VMEM((n_chunks, 1, w), F32)]
    return pl.pallas_call(
        _rwkv_scan_body,
        grid=(batch, d // w, nc),
        in_specs=[blk] * 6 + [vec] * 3, out_specs=blk,
        out_shape=jax.ShapeDtypeStruct((t, d), BF16),
        scratch_shapes=scratch,
        compiler_params=_params("parallel", "parallel", "arbitrary"), name="rwkv_scan",
    )(r, lw, k, v, kk, asg, r_k.reshape(1, d), ln_g.reshape(1, d), ln_b.reshape(1, d))


def _split_bf16(x):
    top = lax.bitcast_convert_type(lax.bitcast_convert_type(x, jnp.uint32) & jnp.uint32(0xFFFF0000), F32)
    return top.astype(BF16), (x - top).astype(BF16)


def _wide_scan_body(r_ref, lw_ref, k_ref, v_ref, kk_ref, as_ref, rk_ref, lg_ref, lb_ref, y_ref,
                    state, wr_s, u0_s, u0t_s, vt_s, o0_s, arb_s, rhs_s, g_s):
    c = CHUNK
    n = HEAD
    w = r_ref.shape[1]
    nh = w // n
    n_chunks = r_ref.shape[0] // c

    @pl.when(pl.program_id(2) == 0)
    def _():
        state[...] = jnp.zeros_like(state)

    row = lax.broadcasted_iota(I32, (c, w), 0)
    col = lax.broadcasted_iota(I32, (c, w), 1) & (n - 1)
    strict = row > col
    incl = row >= col
    eye = (row == col).astype(F32)
    same_head = (lax.broadcasted_iota(I32, (w, w), 0) // n) == (lax.broadcasted_iota(I32, (w, w), 1) // n)
    head_ones = jnp.where(same_head, 1.0, 0.0).astype(BF16)
    r64 = lax.broadcasted_iota(I32, (c, c), 0)
    c64 = lax.broadcasted_iota(I32, (c, c), 1)
    eye_c = jnp.where(r64 == c64, 1.0, 0.0).astype(BF16)

    def bd(x):
        return jnp.concatenate([x] * nh, axis=0) * head_ones

    def head_sums(*xs):
        parts = [p for x in xs for p in _split_bf16(x)]
        sums = _dot(jnp.concatenate(parts, axis=0), head_ones)
        return [sums[2 * j * c:(2 * j + 1) * c] + sums[(2 * j + 1) * c:(2 * j + 2) * c] for j in range(len(xs))]

    def local(cj, carry):
        cis, rowss, a_alls, r_alls, lhss, rhss = [], [], [], [], [], []
        for sub in range(SCAN_GROUP):
            ci = cj * SCAN_GROUP + sub
            rows = pl.ds(pl.multiple_of(ci * c, c), c)
            r = r_ref[rows, :].astype(F32)
            lw = lw_ref[rows, :]
            k = k_ref[rows, :].astype(F32)
            kkr = kk_ref[rows, :].astype(F32)
            asg = as_ref[rows, :].astype(F32)
            cum = _cumsum_rows(lw)
            g_in = jnp.exp(cum)
            g_ex = jnp.exp(cum - lw)
            g_inv = jnp.exp(-cum)
            g_s[ci] = g_in[c - 1:c, :]
            kkn = kkr * lax.rsqrt(jnp.maximum(head_sums(kkr * kkr)[0], 1e-24))
            a_all = (-kkn * g_ex).astype(BF16)
            b_all = (kkn * asg * g_inv).astype(BF16)
            k_all = (k * g_inv).astype(BF16)
            r_all = (r * g_in).astype(BF16)
            cis.append(ci)
            rowss.append(rows)
            a_alls.append(a_all)
            r_alls.append(r_all)
            lhss.append(jnp.concatenate([a_all, r_all], axis=0))
            rhss.append((b_all, k_all))
        pbs = [_dot_nt(lhs, bd(b)) for lhs, (b, _) in zip(lhss, rhss)]
        pks = [_dot_nt(lhs, bd(k)) for lhs, (_, k) in zip(lhss, rhss)]
        lps = [jnp.where(strict, pb[:c], 0.0) for pb in pbs]
        xos = [_dot(jnp.concatenate([jnp.where(strict, pk[:c], 0.0), jnp.where(incl, pk[c:], 0.0)],
                                    axis=0).astype(BF16), bd(v_ref[rows, :])) for pk, rows in zip(pks, rowss)]
        for ci, pb, xo, (b, k) in zip(cis, pbs, xos, rhss):
            arb_s[ci] = jnp.where(incl, pb[c:], 0.0).astype(BF16)
            rhs_s[ci] = jnp.concatenate([b, k], axis=0)
            o0_s[ci] = xo[c:]
        tinvs = [eye + lp for lp in lps]
        lpbs = [lp.astype(BF16) for lp in lps]
        lpbs = [_dot(lpb, bd(lpb)).astype(BF16) for lpb in lpbs]
        s = 4
        while s < c:
            both = [_dot(jnp.concatenate([t.astype(BF16), lpb], axis=0), bd(lpb)) for t, lpb in zip(tinvs, lpbs)]
            tinvs = [t + bo[:c] for t, bo in zip(tinvs, both)]
            lpbs = [bo[c:].astype(BF16) for bo in both]
            s *= 2
        tinvs = [t + _dot(t.astype(BF16), bd(lpb)) for t, lpb in zip(tinvs, lpbs)]
        tbs = [t.astype(BF16) for t in tinvs]
        ws_ = [_dot(tb, bd(a)) for tb, a in zip(tbs, a_alls)]
        u0s = [_dot(tb, bd(xo[:c].astype(BF16))) for tb, xo in zip(tbs, xos)]
        trs = []
        for u0, rows in zip(u0s, rowss):
            top, low = _split_bf16(u0)
            trs.append(_dot_tn(jnp.concatenate([top, low, v_ref[rows, :]], axis=1), eye_c))
        for ci, wm, r_all, u0, tr in zip(cis, ws_, r_alls, u0s, trs):
            wr_s[ci] = jnp.concatenate([wm.astype(BF16), r_all], axis=0)
            u0_s[ci] = u0
            u0t_s[ci] = tr[:w] + tr[w:2 * w]
            vt_s[ci] = tr[2 * w:].astype(BF16)
        return carry

    lax.fori_loop(0, n_chunks // SCAN_GROUP, local, 0)

    rk = rk_ref[...]
    lng = lg_ref[...]
    lnb = lb_ref[...]

    def finish(ci, o):
        rows = pl.ds(pl.multiple_of(ci * c, c), c)
        r = r_ref[rows, :].astype(F32)
        k = k_ref[rows, :].astype(F32)
        v = v_ref[rows, :].astype(F32)
        o_sum, bonus = head_sums(o, r * k * rk)
        dev = o - o_sum * (1.0 / n)
        var = head_sums(dev * dev)[0] * (1.0 / n)
        y = dev * lax.rsqrt(var + RWKV_LN_EPS) * lng + lnb
        y_ref[rows, :] = (y + bonus * v).astype(y_ref.dtype)

    def sweep(ci, o_prev):
        sb = state[...].astype(BF16)
        wr = wr_s[ci]
        ws = _dot_nt(wr, sb)
        ut = _dot_nt(sb, wr[:c])
        finish(jnp.maximum(ci - 1, 0), o_prev)
        ub = (ws[:c] + u0_s[ci]).astype(BF16)
        uvt = jnp.concatenate([(ut + u0t_s[ci]).astype(BF16), vt_s[ci]], axis=1)
        o = ws[c:] + o0_s[ci] + _dot(arb_s[ci], bd(ub))
        upd = _dot(uvt, rhs_s[ci])
        state[...] = (state[...] + jnp.where(same_head, upd, 0.0)) * g_s[ci]
        return o

    finish(n_chunks - 1, lax.fori_loop(0, n_chunks, sweep, jnp.zeros((c, w), F32)))


def _wide_scan(r, lw, k, v, kk, asg, r_k, ln_g, ln_b, batch, seq):
    t, d = r.shape
    rb = SCAN_ROWS
    w = SCAN_LANES
    nc = seq // rb
    n_chunks = rb // CHUNK
    assert d % w == 0 and seq % rb == 0 and n_chunks % SCAN_GROUP == 0
    blk = pl.BlockSpec((rb, w), lambda b, hp, ci: (b * nc + ci, hp))
    vec = pl.BlockSpec((1, w), lambda b, hp, ci: (0, hp))
    per_chunk = lambda rows, cols, dt: pltpu.VMEM((n_chunks, rows, cols), dt)
    scratch = [pltpu.VMEM((w, w), F32),
               per_chunk(2 * CHUNK, w, BF16),
               per_chunk(CHUNK, w, F32),
               per_chunk(w, CHUNK, F32),
               per_chunk(w, CHUNK, BF16),
               per_chunk(CHUNK, w, F32),
               per_chunk(CHUNK, w, BF16),
               per_chunk(2 * CHUNK, w, BF16),
               per_chunk(1, w, F32)]
    return pl.pallas_call(
        _wide_scan_body,
        grid=(batch, d // w, nc),
        in_specs=[blk] * 6 + [vec] * 3, out_specs=blk,
        out_shape=jax.ShapeDtypeStruct((t, d), BF16),
        scratch_shapes=scratch,
        compiler_params=_params("parallel", "parallel", "arbitrary"), name="rwkv_scan",
    )(r, lw, k, v, kk, asg, r_k.reshape(1, d), ln_g.reshape(1, d), ln_b.reshape(1, d))


def _linear_res_body(x_ref, a_ref, *rest, gated):
    if gated:
        g_ref, w_ref, o_ref = rest
        a = a_ref[...] * g_ref[...]
    else:
        w_ref, o_ref = rest
        a = a_ref[...]
    o_ref[...] = x_ref[...] + _dot(a, w_ref[...])


def _linear_res(x, a, w, gate=None):
    t, d = x.shape
    tm = ROW_TILE
    row = pl.BlockSpec((tm, d), lambda i: (i, 0))
    ins = [x, a] + ([gate] if gate is not None else []) + [w.astype(BF16)]
    return pl.pallas_call(
        functools.partial(_linear_res_body, gated=gate is not None),
        grid=(t // tm,), in_specs=[row] * (len(ins) - 1) + [_full(w.shape)], out_specs=row,
        out_shape=jax.ShapeDtypeStruct((t, d), F32),
        compiler_params=_params("parallel"), name="linear_res",
    )(*ins)


def _norm_linear_body(x_ref, g_ref, w_ref, o_ref):
    o_ref[...] = _dot(_rms(x_ref[...], g_ref[...]).astype(BF16), w_ref[...]).astype(o_ref.dtype)


def _norm_linear(x, g, w, out_dtype):
    t, d = x.shape
    tm = ROW_TILE
    return pl.pallas_call(
        _norm_linear_body, grid=(t // tm,),
        in_specs=[pl.BlockSpec((tm, d), lambda i: (i, 0)), _full((1, d)), _full(w.shape)],
        out_specs=pl.BlockSpec((tm, w.shape[1]), lambda i: (i, 0)),
        out_shape=jax.ShapeDtypeStruct((t, w.shape[1]), out_dtype),
        compiler_params=_params("parallel"), name="norm_linear",
    )(x, g.reshape(1, d), w.astype(BF16))


def _mem_attn_body(x_ref, a_ref, *rest, gated):
    if gated:
        mix = a_ref[...] * rest[0][...]
        rest = rest[1:]
    else:
        mix = a_ref[...]
    wm_ref, kv_ref, gn_ref, wq_ref, wo_ref, gf_ref, wrh_ref, wrl_ref, br_ref, x_o, xn_o, lg_o = rest
    x = x_ref[...] + _dot(mix, wm_ref[...])
    d = x.shape[1]
    hd = d // MEM_HEADS
    q = _dot(_rms(x, gn_ref[...]).astype(BF16), wq_ref[...])
    outs = []
    for hh in range(MEM_HEADS):
        qh = q[:, hh * hd:(hh + 1) * hd].astype(BF16)
        kh = kv_ref[:, hh * hd:(hh + 1) * hd]
        vh = kv_ref[:, d + hh * hd:d + (hh + 1) * hd]
        s = _dot_nt(qh, kh)
        p = jnp.exp(s - jnp.max(s, axis=-1, keepdims=True))
        o = _dot(p.astype(BF16), vh) / jnp.sum(p, axis=-1, keepdims=True)
        outs.append(o.astype(BF16))
    xnew = x + _dot(jnp.concatenate(outs, axis=1), wo_ref[...])
    x_o[...] = xnew
    xn = _rms(xnew, gf_ref[...])
    xn_o[...] = xn
    xh = xn.astype(BF16)
    xl = (xn - xh.astype(F32)).astype(BF16)
    tm = x.shape[0]
    hw = _dot(jnp.concatenate([xh, xl], axis=0), wrh_ref[...])
    lg_o[...] = hw[:tm] + hw[tm:] + _dot(xh, wrl_ref[...]) + br_ref[...]


def _mem_attn(x, mix, gate, w_mix, kv, seq, mem_tokens, gn, w_q, w_o, g_ffn, w_router, b_router):
    t, d = x.shape
    tm = ROW_TILE
    per_seq = seq // tm
    row = pl.BlockSpec((tm, d), lambda i: (i, 0))
    w_hi = w_router.astype(BF16)
    acts = [x, mix] + ([gate] if gate is not None else [])
    return pl.pallas_call(
        functools.partial(_mem_attn_body, gated=gate is not None), grid=(t // tm,),
        in_specs=[row] * len(acts) + [_full((d, d)), pl.BlockSpec((mem_tokens, 2 * d), lambda i: (i // per_seq, 0)),
                                      _full((1, d)), _full((d, d)), _full((d, d)), _full((1, d)),
                                      _full((d, LANES)), _full((d, LANES)), _full((1, LANES))],
        out_specs=[row, row, pl.BlockSpec((tm, LANES), lambda i: (i, 0))],
        out_shape=[jax.ShapeDtypeStruct((t, d), F32), jax.ShapeDtypeStruct((t, d), F32),
                   jax.ShapeDtypeStruct((t, LANES), F32)],
        compiler_params=_params("parallel"), name="mem_attn",
    )(*acts, w_mix.astype(BF16), kv, gn.reshape(1, d), (w_q * (d // MEM_HEADS) ** -0.5).astype(BF16),
      w_o.astype(BF16), g_ffn.reshape(1, d), w_hi, (w_router - w_hi.astype(F32)).astype(BF16), b_router)


def _route_body(lg_ref, idx_o, gate_o, cnt_o, carry):
    @pl.when(pl.program_id(0) == 0)
    def _():
        carry[...] = jnp.zeros_like(carry)

    lg = lg_ref[...]
    tm = lg.shape[0]
    lane = lax.broadcasted_iota(I32, lg.shape, 1)
    lanef = lane.astype(F32)
    big = 1e9
    ninf = -jnp.inf
    gl = jnp.where(lane < N_GROUPS, lg, ninf)
    gmax = jnp.max(gl, axis=-1, keepdims=True)
    g_w = 1.0 / jnp.sum(jnp.exp(gl - gmax), axis=-1, keepdims=True)
    grp = jnp.min(jnp.where(gl == gmax, lanef, big), axis=-1, keepdims=True).astype(I32)
    in_grp = (lane >= N_GROUPS) & (lane < N_GROUPS + N_EXPERTS) & (((lane - N_GROUPS) >> 3) == grp)
    el = jnp.where(in_grp, lg, ninf)
    m1 = jnp.max(el, axis=-1, keepdims=True)
    i1 = jnp.min(jnp.where(el == m1, lanef, big), axis=-1, keepdims=True)
    el2 = jnp.where(lanef == i1, ninf, el)
    m2 = jnp.max(el2, axis=-1, keepdims=True)
    i2 = jnp.min(jnp.where(el2 == m2, lanef, big), axis=-1, keepdims=True)
    tt = jnp.exp(m2 - m1)
    w1 = 1.0 / (1.0 + tt)
    w2 = tt / (1.0 + tt)
    hit1 = lanef == i1
    hit2 = lanef == i2
    oh = jnp.where(hit1 | hit2, 1.0, 0.0)
    r_i = lax.broadcasted_iota(I32, (tm, tm), 0)
    c_i = lax.broadcasted_iota(I32, (tm, tm), 1)
    tri = jnp.where(r_i > c_i, 1.0, 0.0).astype(BF16)
    pos = _dot(tri, oh.astype(BF16)) + carry[...]
    pos1 = jnp.sum(jnp.where(hit1, pos, 0.0), axis=-1, keepdims=True)
    pos2 = jnp.sum(jnp.where(hit2, pos, 0.0), axis=-1, keepdims=True)
    carry[...] = carry[...] + jnp.sum(oh, axis=0, keepdims=True)
    cnt_o[...] = carry[...]
    idx = jnp.where(lane == 0, i1 - N_GROUPS, jnp.where(lane == 1, i2 - N_GROUPS,
          jnp.where(lane == 2, pos1, jnp.where(lane == 3, pos2, 0.0))))
    idx_o[...] = idx.astype(I32)
    gate_o[...] = jnp.where(lane == 0, g_w * w1, jnp.where(lane == 1, g_w * w2, 0.0))


def _route(logits):
    t = logits.shape[0]
    tm = ROW_TILE
    row = pl.BlockSpec((tm, LANES), lambda i: (i, 0))
    return pl.pallas_call(
        _route_body, grid=(t // tm,), in_specs=[row],
        out_specs=[row, row, _full((1, LANES))],
        out_shape=[jax.ShapeDtypeStruct((t, LANES), I32), jax.ShapeDtypeStruct((t, LANES), F32),
                   jax.ShapeDtypeStruct((1, LANES), F32)],
        scratch_shapes=[pltpu.VMEM((1, LANES), F32)],
        compiler_params=_params("arbitrary"), name="moe_route",
    )(logits)


def _row_copy(src, s, dst, d, sem):
    return pltpu.make_async_copy(src.at[pl.ds(s, 1)], dst.at[pl.ds(d, 1)], sem)


def _dispatch_body(pend_ref, pad_ref, dest_ref, xn_ref, xs_out, zero_buf, sem):
    tm = xn_ref.shape[0]
    blk = zero_buf.shape[0]

    @pl.when(pl.program_id(0) == 0)
    def _():
        zero_buf[...] = jnp.zeros_like(zero_buf)

        def clear_block(start):
            cp = pltpu.make_async_copy(zero_buf, xs_out.at[pl.ds(pl.multiple_of(start, blk), blk)], sem)
            cp.start()
            cp.wait()

        def clear(e, c):
            @pl.when(pad_ref[e] > 0)
            def _():
                clear_block(pend_ref[e] - blk)
            return c

        def clear_tail(j, c):
            clear_block(j * blk)
            return c

        lax.fori_loop(0, N_EXPERTS, clear, 0)
        lax.fori_loop(pend_ref[N_EXPERTS - 1] // blk, xs_out.shape[0] // blk, clear_tail, 0)

    def issue(r, c):
        _row_copy(xn_ref, r, xs_out, dest_ref[0, 0, r], sem).start()
        _row_copy(xn_ref, r, xs_out, dest_ref[0, 0, tm + r], sem).start()
        return c

    lax.fori_loop(0, tm, issue, 0, unroll=DMA_UNROLL)
    for _ in range(2):
        pltpu.make_async_copy(xn_ref, xs_out.at[pl.ds(0, tm)], sem).wait()


def _dispatch(xn, dest, pend, padded, cap):
    t, d = xn.shape
    tm = ROW_TILE
    grid_spec = pltpu.PrefetchScalarGridSpec(
        num_scalar_prefetch=2, grid=(t // tm,),
        in_specs=[pl.BlockSpec((1, 1, 2 * tm), lambda i, pe, pa: (i, 0, 0), memory_space=pltpu.SMEM),
                  pl.BlockSpec((tm, d), lambda i, pe, pa: (i, 0))],
        out_specs=pl.BlockSpec(memory_space=pl.ANY),
        scratch_shapes=[pltpu.VMEM((MOE_BLOCK, d), F32), pltpu.SemaphoreType.DMA])
    return pl.pallas_call(
        _dispatch_body, grid_spec=grid_spec, out_shape=jax.ShapeDtypeStruct((cap, d), F32),
        compiler_params=_params("arbitrary"), name="moe_dispatch",
    )(pend, padded, dest, xn)


def _experts_body(be_ref, nb_ref, xs_ref, wgu_ref, wd_ref, ys_ref):
    del be_ref
    live = pl.program_id(0) < nb_ref[0]

    @pl.when(live)
    def _():
        gu = _dot(xs_ref[...].astype(BF16), wgu_ref[0])
        f = gu.shape[1] // 2
        gt = gu[:, :f]
        hid = gt * _sigmoid(gt) * gu[:, f:]
        ys_ref[...] = _dot(hid.astype(BF16), wd_ref[0])

    @pl.when(jnp.logical_not(live))
    def _():
        ys_ref[...] = jnp.zeros_like(ys_ref)


def _experts(xs, block_expert, n_live, w_gate_up, w_down):
    cap, d = xs.shape
    f2 = w_gate_up.shape[2]
    blk = MOE_BLOCK
    grid_spec = pltpu.PrefetchScalarGridSpec(
        num_scalar_prefetch=2, grid=(cap // blk,),
        in_specs=[pl.BlockSpec((blk, d), lambda i, be, nb: (jnp.minimum(i, nb[0] - 1), 0)),
                  pl.BlockSpec((1, d, f2), lambda i, be, nb: (be[i], 0, 0)),
                  pl.BlockSpec((1, f2 // 2, d), lambda i, be, nb: (be[i], 0, 0))],
        out_specs=pl.BlockSpec((blk, d), lambda i, be, nb: (i, 0)))
    return pl.pallas_call(
        _experts_body, grid_spec=grid_spec, out_shape=jax.ShapeDtypeStruct((cap, d), F32),
        compiler_params=_params("arbitrary"), name="moe_experts",
    )(block_expert, n_live, xs, w_gate_up, w_down)


def _combine_body(dest_ref, x_ref, gate_ref, ys_ref, *rest, n_proj, final_norm):
    norm_refs = rest[:n_proj + (1 if final_norm else 0)]
    w_refs = rest[len(norm_refs):len(norm_refs) + n_proj]
    outs = rest[len(norm_refs) + n_proj:-3]
    buf1, buf2, sem = rest[-3:]
    tm = x_ref.shape[0]

    def issue(r, c):
        _row_copy(ys_ref, dest_ref[0, 0, r], buf1, r, sem).start()
        _row_copy(ys_ref, dest_ref[0, 0, tm + r], buf2, r, sem).start()
        return c

    lax.fori_loop(0, tm, issue, 0, unroll=DMA_UNROLL)
    for buf in (buf1, buf2):
        pltpu.make_async_copy(ys_ref.at[pl.ds(0, tm)], buf, sem).wait()
    gate = gate_ref[...]
    x = x_ref[...] + gate[:, 0:1] * buf1[...] + gate[:, 1:2] * buf2[...]
    if final_norm:
        outs[0][...] = _rms(x, norm_refs[0][...])
        return
    outs[0][...] = x
    for j in range(n_proj):
        outs[1 + j][...] = _dot(_rms(x, norm_refs[j][...]).astype(BF16), w_refs[j][...]).astype(outs[1 + j].dtype)


def _combine(x, ys, dest, gates, norms, weights, final_norm=False):
    t, d = x.shape
    tm = ROW_TILE
    row = pl.BlockSpec((tm, d), lambda i: (i, 0))
    in_specs = [pl.BlockSpec((1, 1, 2 * tm), lambda i: (i, 0, 0), memory_space=pltpu.SMEM),
                row, pl.BlockSpec((tm, LANES), lambda i: (i, 0)), pl.BlockSpec(memory_space=pl.ANY)]
    in_specs += [_full((1, d))] * len(norms) + [_full(w.shape) for w in weights]
    out_specs = [row] + [pl.BlockSpec((tm, w.shape[1]), lambda i: (i, 0)) for w in weights]
    out_shape = [jax.ShapeDtypeStruct((t, d), F32)]
    out_shape += [jax.ShapeDtypeStruct((t, w.shape[1]), BF16) for w in weights]
    return pl.pallas_call(
        functools.partial(_combine_body, n_proj=len(weights), final_norm=final_norm),
        grid=(t // tm,), in_specs=in_specs, out_specs=out_specs, out_shape=out_shape,
        scratch_shapes=[pltpu.VMEM((tm, d), F32), pltpu.VMEM((tm, d), F32), pltpu.SemaphoreType.DMA],
        compiler_params=_params("arbitrary"), name="moe_combine",
    )(dest, x, gates, ys, *[g.reshape(1, d) for g in norms], *[w.astype(BF16) for w in weights])


def _moe(x, xn, logits, w_gate, w_up, w_down, norms, weights, final_norm=False):
    t, d = x.shape
    blk = MOE_BLOCK
    tm = ROW_TILE
    idx, gates, counts = _route(logits)
    counts = counts[0, N_GROUPS:N_GROUPS + N_EXPERTS].astype(I32)
    padded = (counts + blk - 1) // blk * blk
    pend = jnp.cumsum(padded)
    pstart = pend - padded
    cap = 2 * t + N_EXPERTS * blk
    n_blocks = cap // blk
    starts = jnp.arange(n_blocks, dtype=I32) * blk
    block_expert = jnp.minimum(jnp.sum(pend[None, :] <= starts[:, None], axis=1), N_EXPERTS - 1).astype(I32)
    n_live = (pend[-1:] // blk).astype(I32)
    dest = jnp.take(pstart, idx[:, 0:2], axis=0) + idx[:, 2:4]
    dest = dest.reshape(t // tm, tm, 2).transpose(0, 2, 1).reshape(t // tm, 1, 2 * tm)
    xs = _dispatch(xn, dest, pend.astype(I32), padded.astype(I32), cap)
    w_gu = jnp.concatenate([w_gate, w_up], axis=2).astype(BF16)
    ys = _experts(xs, block_expert, n_live, w_gu, w_down.astype(BF16))
    return _combine(x, ys, dest, gates, norms, weights, final_norm)


def _sb_masks():
    row = lax.broadcasted_iota(I32, (SB_BLOCK, SB_WINDOW), 0)
    col = lax.broadcasted_iota(I32, (SB_BLOCK, SB_WINDOW), 1)
    visible = jnp.stack([col < row, col < row + SB_BLOCK, col >= 0, col < SB_BLOCK])
    return jnp.where(visible, 0.0, SB_HIDDEN).astype(F32)


def _sb_attn_body(q_ref, k_ref, v_ref, tri_ref, bias_ref, o_ref):
    i = pl.program_id(2)
    blk, win = SB_BLOCK, SB_WINDOW
    nh = q_ref.shape[1] // HEAD
    neg_suffix = tri_ref[...]
    lanes = [slice(h * HEAD, (h + 1) * HEAD) for h in range(nh)]
    qs = [(q_ref[:, sl].astype(F32) * (HEAD ** -0.5 * LOG2E)).astype(BF16) for sl in lanes]
    u32 = jnp.uint32

    def more(carry):
        hi, go = carry[0], carry[1]
        return jnp.logical_and(hi > 0, go)

    def window(carry):
        hi, _, accs, rests = carry
        lo = jnp.maximum(hi - win, 0)
        rows = pl.ds(pl.multiple_of(lo, blk), win)
        first = hi == (i + 1) * blk
        which = jnp.where(first, jnp.where(i == 0, 0, 1), jnp.where(hi < win, 3, 2))
        bias = bias_ref[which]
        zs, parts = [], []
        for h in range(nh):
            z = _dot_nt(qs[h], k_ref[rows, lanes[h]]) + bias
            neg_abs = lax.bitcast_convert_type(lax.bitcast_convert_type(z, u32) | u32(0x80000000), F32)
            sp = jnp.maximum(z, 0.0) + jnp.log(1.0 + jnp.exp2(neg_abs)) * LOG2E
            top = lax.bitcast_convert_type(lax.bitcast_convert_type(sp, u32) & u32(0xFFFF0000), F32)
            zs.append(z)
            parts += [top.astype(BF16), (sp - top).astype(BF16)]
        sums = _dot(jnp.concatenate(parts, axis=0), neg_suffix)
        new_accs, new_rests = [], []
        for h in range(nh):
            sfx = sums[2 * h * blk:(2 * h + 1) * blk] + sums[(2 * h + 1) * blk:(2 * h + 2) * blk]
            attn = jnp.exp2(zs[h] + sfx + rests[h])
            new_accs.append(accs[h] + _dot(attn.astype(BF16), v_ref[rows, lanes[h]]))
            new_rests.append(rests[h] + sfx[:, 0:1])
        worst = functools.reduce(jnp.maximum, new_rests)
        return lo, jnp.max(worst) > SB_ZERO_LOG2, tuple(new_accs), tuple(new_rests)

    init = ((i + 1) * blk, True, tuple(jnp.zeros((blk, HEAD), F32) for _ in range(nh)),
            tuple(jnp.zeros((blk, 1), F32) for _ in range(nh)))
    accs = lax.while_loop(more, window, init)[2]
    o_ref[...] = jnp.concatenate(accs, axis=1).astype(o_ref.dtype)


def _sb_attn(q, kv, batch, seq):
    t, d = q.shape
    nq = seq // SB_BLOCK
    w = SB_LANES
    hp = d // w
    win = SB_WINDOW
    assert d % w == 0 and seq % SB_BLOCK == 0 and seq >= win and win == 2 * SB_BLOCK
    tri = -jnp.tril(jnp.ones((win, win), F32)).astype(BF16)
    return pl.pallas_call(
        _sb_attn_body, grid=(batch, hp, nq),
        in_specs=[pl.BlockSpec((SB_BLOCK, w), lambda b, h, i: (b * nq + i, h)),
                  pl.BlockSpec((seq, w), lambda b, h, i: (b, h)),
                  pl.BlockSpec((seq, w), lambda b, h, i: (b, hp + h)),
                  pl.BlockSpec((win, win), lambda b, h, i: (0, 0)),
                  pl.BlockSpec((4, SB_BLOCK, win), lambda b, h, i: (0, 0, 0))],
        out_specs=pl.BlockSpec((SB_BLOCK, w), lambda b, h, i: (b * nq + i, h)),
        out_shape=jax.ShapeDtypeStruct((t, d), BF16),
        compiler_params=_params("parallel", "parallel", "arbitrary"), name="sb_attn",
    )(q, kv, kv, tri, _sb_masks())


def _router_weights(w_group, b_group, w_expert, b_expert):
    d = w_group.shape[0]
    pad = LANES - N_GROUPS - N_EXPERTS
    w = jnp.concatenate([w_group, w_expert, jnp.zeros((d, pad), F32)], axis=1)
    b = jnp.concatenate([b_group, b_expert, jnp.zeros((pad,), F32)]).reshape(1, LANES)
    return w, b


def kernel(x, mem, norm_mix, norm_mem, norm_ffn, norm_final, rwkv_mu, rwkv_w_rkv, rwkv_w0, rwkv_w1, rwkv_w2, rwkv_a0, rwkv_a1, rwkv_a2, rwkv_g1, rwkv_g2, rwkv_k_k, rwkv_k_a, rwkv_r_k, rwkv_ln_g, rwkv_ln_b, rwkv_w_o, kv_norm, w_kv_shared, sb_w_q, sb_w_o, mem_norm_kv, mem_w_q, mem_w_kv, mem_w_o, moe_w_group, moe_b_group, moe_w_expert, moe_b_expert, moe_w_gate, moe_w_up, moe_w_down):
    batch, seq, d = x.shape
    depth = norm_mix.shape[0]
    n_a = rwkv_mu.shape[0]
    mem_tokens = mem.shape[1]
    xs = x.reshape(batch * seq, d)
    memf = mem.reshape(batch * mem_tokens, d)
    q = kv = None
    for l in range(depth):
        if l < n_a:
            r, lw, k, v, kk, asg, g = _rwkv_proj(
                xs, seq, norm_mix[l], rwkv_mu[l], rwkv_w_rkv[l], rwkv_w0[l], rwkv_w1[l], rwkv_w2[l],
                rwkv_a0[l], rwkv_a1[l], rwkv_a2[l], rwkv_g1[l], rwkv_g2[l], rwkv_k_k[l], rwkv_k_a[l])
            mix = _wide_scan(r, lw, k, v, kk, asg, rwkv_r_k[l], rwkv_ln_g[l], rwkv_ln_b[l], batch, seq)
            gate, w_mix = g, rwkv_w_o[l]
        else:
            mix = _sb_attn(q, kv, batch, seq)
            gate, w_mix = None, sb_w_o[l - n_a]
        mkv = _norm_linear(memf, mem_norm_kv[l], mem_w_kv[l], BF16)
        w_router, b_router = _router_weights(moe_w_group[l], moe_b_group[l], moe_w_expert[l], moe_b_expert[l])
        xs, xn, logits = _mem_attn(xs, mix, gate, w_mix, mkv, seq, mem_tokens, norm_mem[l], mem_w_q[l], mem_w_o[l],
                                   norm_ffn[l], w_router, b_router)
        last = l == depth - 1
        if last:
            norms, weights = [norm_final], []
        elif l + 1 < n_a:
            norms, weights = [], []
        else:
            norms, weights = [norm_mix[l + 1]], [sb_w_q[l + 1 - n_a]]
            if l == n_a - 1:
                norms, weights = norms + [kv_norm], weights + [w_kv_shared]
        res = _moe(xs, xn, logits, moe_w_gate[l], moe_w_up[l], moe_w_down[l], norms, weights, final_norm=last)
        if last:
            return res[0].reshape(batch, seq, d)
        xs = res[0]
        if weights:
            q = res[1]
            if l == n_a - 1:
                kv = res[2]
```

```python
import functools

import jax
import jax.numpy as jnp
from jax import lax
from jax.experimental import pallas as pl
from jax.experimental.pallas import tpu as pltpu

F32 = jnp.float32
BF16 = jnp.bfloat16
I32 = jnp.int32

HEAD = 64
MEM_HEADS = 4
N_GROUPS = 4
EXPERTS_PER_GROUP = 8
N_EXPERTS = N_GROUPS * EXPERTS_PER_GROUP
NORM_EPS = 1e-6
RWKV_LN_EPS = 64e-5

LANES = 128
CHUNK = 64
SB_BLOCK = 128
SB_WINDOW = 256
SB_LANES = 512
SB_ZERO_LOG2 = -160.0
SB_HIDDEN = -1e30
LOG2E = 1.4426950408889634
MOE_BLOCK = 512
DMA_UNROLL = 8
ROW_TILE = 512
PROJ_TILE = 512
SCAN_ROWS = 512
SCAN_LANES = 256
SCAN_GROUP = 8
VMEM_LIMIT = 56 * 1024 * 1024


def _params(*sem):
    return pltpu.CompilerParams(dimension_semantics=sem, vmem_limit_bytes=VMEM_LIMIT)


def _dot(a, b):
    return jnp.dot(a, b, preferred_element_type=F32)


def _dot_nt(a, b):
    return lax.dot_general(a, b, (((1,), (1,)), ((), ())), preferred_element_type=F32)


def _dot_tn(a, b):
    return lax.dot_general(a, b, (((0,), (0,)), ((), ())), preferred_element_type=F32)


def _rms(x, g):
    ms = jnp.mean(x * x, axis=-1, keepdims=True)
    return x * lax.rsqrt(ms + NORM_EPS) * g


def _sigmoid(x):
    return 1.0 / (1.0 + jnp.exp(-x))


def _softplus(x):
    return jnp.maximum(x, 0.0) + jnp.log(1.0 + jnp.exp(-jnp.abs(x)))


def _full(shape):
    return pl.BlockSpec(shape, lambda *_: (0,) * len(shape))


def _rwkv_proj_body(x_ref, xp_ref, gn_ref, mu_ref, wr_ref, wk_ref, wv_ref, w1_ref, a1_ref, g1_ref,
                    w2_ref, a2_ref, g2_ref, vec_ref,
                    r_o, lw_o, k_o, v_o, kk_o, as_o, g_o, *, tiles_per_seq):
    i = pl.program_id(0)
    gn = gn_ref[...]
    h = _rms(x_ref[...], gn)
    hp = _rms(xp_ref[...], gn)[7:8, :]
    hp = jnp.where(lax.rem(i, tiles_per_seq) == 0, 0.0, hp)
    row = lax.broadcasted_iota(I32, (h.shape[0], 1), 0)
    hprev = jnp.where(row == 0, hp, pltpu.roll(h, 1, 0))
    xx = hprev - h
    mu = mu_ref[...]

    def lerp(j):
        return (h + xx * mu[j:j + 1, :]).astype(BF16)

    vec = vec_ref[...]
    r = _dot(lerp(0), wr_ref[...])
    k = _dot(lerp(1), wk_ref[...])
    v = _dot(lerp(2), wv_ref[...])
    w = vec[0:1] + _dot(jnp.tanh(_dot(lerp(3), w1_ref[...])).astype(BF16), w2_ref[...])
    w = -_softplus(-w) - 0.5
    a = _sigmoid(vec[1:2] + _dot(_dot(lerp(4), a1_ref[...]).astype(BF16), a2_ref[...]))
    g = _dot(_sigmoid(_dot(lerp(5), g1_ref[...])).astype(BF16), g2_ref[...])
    r_o[...] = r.astype(BF16)
    lw_o[...] = -jnp.exp(w)
    k_o[...] = (k * (1.0 + (a - 1.0) * vec[3:4])).astype(BF16)
    v_o[...] = v.astype(BF16)
    kk_o[...] = (k * vec[2:3]).astype(BF16)
    as_o[...] = a.astype(BF16)
    g_o[...] = g.astype(BF16)


def _rwkv_proj(x, seq, gn, mu, w_rkv, w0, w1, w2, a0, a1, a2, g1, g2, k_k, k_a):
    t, d = x.shape
    tm = PROJ_TILE
    pad8 = lambda rows: jnp.concatenate([rows, jnp.zeros((8 - rows.shape[0], d), F32)], axis=0)
    vec = pad8(jnp.stack([w0, a0, k_k, k_a]))
    mu8 = pad8(mu)
    wts = [w_rkv[0], w_rkv[1], w_rkv[2], w1, a1, g1, w2, a2, g2]
    wts = [w.astype(BF16) for w in wts]
    row_spec = pl.BlockSpec((tm, d), lambda i: (i, 0))
    prev_spec = pl.BlockSpec((8, d), lambda i: (jnp.maximum(i * (tm // 8) - 1, 0), 0))
    in_specs = [row_spec, prev_spec, _full((1, d)), _full((8, d))]
    in_specs += [_full(w.shape) for w in wts] + [_full((8, d))]
    outs = [jax.ShapeDtypeStruct((t, d), dt) for dt in (BF16, F32, BF16, BF16, BF16, BF16, BF16)]
    return pl.pallas_call(
        functools.partial(_rwkv_proj_body, tiles_per_seq=seq // tm),
        grid=(t // tm,), in_specs=in_specs, out_specs=[row_spec] * 7, out_shape=outs,
        compiler_params=_params("parallel"), name="rwkv_proj",
    )(x, x, gn.reshape(1, d), mu8, *wts, vec)


def _cumsum_rows(x):
    n = x.shape[0]
    row = lax.broadcasted_iota(I32, (n, 1), 0)
    s = 1
    while s < n:
        x = x + jnp.where(row >= s, pltpu.roll(x, s, 0), 0.0)
        s *= 2
    return x


def _split_bf16(x):
    top = lax.bitcast_convert_type(lax.bitcast_convert_type(x, jnp.uint32) & jnp.uint32(0xFFFF0000), F32)
    return top.astype(BF16), (x - top).astype(BF16)


def _wide_scan_body(r_ref, lw_ref, k_ref, v_ref, kk_ref, as_ref, rk_ref, lg_ref, lb_ref, y_ref,
                    state, wr_s, u0_s, u0t_s, vt_s, o0_s, arb_s, rhs_s, g_s):
    c = CHUNK
    n = HEAD
    w = r_ref.shape[1]
    nh = w // n
    n_chunks = r_ref.shape[0] // c

    @pl.when(pl.program_id(2) == 0)
    def _():
        state[...] = jnp.zeros_like(state)

    row = lax.broadcasted_iota(I32, (c, w), 0)
    col = lax.broadcasted_iota(I32, (c, w), 1) & (n - 1)
    strict = row > col
    incl = row >= col
    eye = (row == col).astype(F32)
    same_head = (lax.broadcasted_iota(I32, (w, w), 0) // n) == (lax.broadcasted_iota(I32, (w, w), 1) // n)
    head_ones = jnp.where(same_head, 1.0, 0.0).astype(BF16)
    r64 = lax.broadcasted_iota(I32, (c, c), 0)
    c64 = lax.broadcasted_iota(I32, (c, c), 1)
    eye_c = jnp.where(r64 == c64, 1.0, 0.0).astype(BF16)

    def bd(x):
        return jnp.concatenate([x] * nh, axis=0) * head_ones

    def head_sums(*xs):
        parts = [p for x in xs for p in _split_bf16(x)]
        sums = _dot(jnp.concatenate(parts, axis=0), head_ones)
        return [sums[2 * j * c:(2 * j + 1) * c] + sums[(2 * j + 1) * c:(2 * j + 2) * c] for j in range(len(xs))]

    def local(cj, carry):
        cis, rowss, a_alls, r_alls, lhss, rhss = [], [], [], [], [], []
        for sub in range(SCAN_GROUP):
            ci = cj * SCAN_GROUP + sub
            rows = pl.ds(pl.multiple_of(ci * c, c), c)
            r = r_ref[rows, :].astype(F32)
            lw = lw_ref[rows, :]
            k = k_ref[rows, :].astype(F32)
            kkr = kk_ref[rows, :].astype(F32)
            asg = as_ref[rows, :].astype(F32)
            cum = _cumsum_rows(lw)
            g_in = jnp.exp(cum)
            g_ex = jnp.exp(cum - lw)
            g_inv = jnp.exp(-cum)
            g_s[ci] = g_in[c - 1:c, :]
            kkn = kkr * lax.rsqrt(jnp.maximum(head_sums(kkr * kkr)[0], 1e-24))
            a_all = (-kkn * g_ex).astype(BF16)
            b_all = (kkn * asg * g_inv).astype(BF16)
            k_all = (k * g_inv).astype(BF16)
            r_all = (r * g_in).astype(BF16)
            cis.append(ci)
            rowss.append(rows)
            a_alls.append(a_all)
            r_alls.append(r_all)
            lhss.append(jnp.concatenate([a_all, r_all], axis=0))
            rhss.append((b_all, k_all))
        pbs = [_dot_nt(lhs, bd(b)) for lhs, (b, _) in zip(lhss, rhss)]
        pks = [_dot_nt(lhs, bd(k)) for lhs, (_, k) in zip(lhss, rhss)]
        lps = [jnp.where(strict, pb[:c], 0.0) for pb in pbs]
        xos = [_dot(jnp.concatenate([jnp.where(strict, pk[:c], 0.0), jnp.where(incl, pk[c:], 0.0)],
                                    axis=0).astype(BF16), bd(v_ref[rows, :])) for pk, rows in zip(pks, rowss)]
        for ci, pb, xo, (b, k) in zip(cis, pbs, xos, rhss):
            arb_s[ci] = jnp.where(incl, pb[c:], 0.0).astype(BF16)
            rhs_s[ci] = jnp.concatenate([b, k], axis=0)
            o0_s[ci] = xo[c:]
        tinvs = [eye + lp for lp in lps]
        lpbs = [lp.astype(BF16) for lp in lps]
        lpbs = [_dot(lpb, bd(lpb)).astype(BF16) for lpb in lpbs]
        s = 4
        while s < c:
            both = [_dot(jnp.concatenate([t.astype(BF16), lpb], axis=0), bd(lpb)) for t, lpb in zip(tinvs, lpbs)]
            tinvs = [t + bo[:c] for t, bo in zip(tinvs, both)]
            lpbs = [bo[c:].astype(BF16) for bo in both]
            s *= 2
        tinvs = [t + _dot(t.astype(BF16), bd(lpb)) for t, lpb in zip(tinvs, lpbs)]
        tbs = [t.astype(BF16) for t in tinvs]
        ws_ = [_dot(tb, bd(a)) for tb, a in zip(tbs, a_alls)]
        u0s = [_dot(tb, bd(xo[:c].astype(BF16))) for tb, xo in zip(tbs, xos)]
        trs = []
        for u0, rows in zip(u0s, rowss):
            top, low = _split_bf16(u0)
            trs.append(_dot_tn(jnp.concatenate([top, low, v_ref[rows, :]], axis=1), eye_c))
        for ci, wm, r_all, u0, tr in zip(cis, ws_, r_alls, u0s, trs):
            wr_s[ci] = jnp.concatenate([wm.astype(BF16), r_all], axis=0)
            u0_s[ci] = u0
            u0t_s[ci] = tr[:w] + tr[w:2 * w]
            vt_s[ci] = tr[2 * w:].astype(BF16)
        return carry

    lax.fori_loop(0, n_chunks // SCAN_GROUP, local, 0)

    rk = rk_ref[...]
    lng = lg_ref[...]
    lnb = lb_ref[...]

    def finish(ci, o):
        rows = pl.ds(pl.multiple_of(ci * c, c), c)
        r = r_ref[rows, :].astype(F32)
        k = k_ref[rows, :].astype(F32)
        v = v_ref[rows, :].astype(F32)
        o_sum, bonus = head_sums(o, r * k * rk)
        dev = o - o_sum * (1.0 / n)
        var = head_sums(dev * dev)[0] * (1.0 / n)
        y = dev * lax.rsqrt(var + RWKV_LN_EPS) * lng + lnb
        y_ref[rows, :] = (y + bonus * v).astype(y_ref.dtype)

    def sweep(ci, o_prev):
        sb = state[...].astype(BF16)
        wr = wr_s[ci]
        ws = _dot_nt(wr, sb)
        ut = _dot_nt(sb, wr[:c])
        finish(jnp.maximum(ci - 1, 0), o_prev)
        ub = (ws[:c] + u0_s[ci]).astype(BF16)
        uvt = jnp.concatenate([(ut + u0t_s[ci]).astype(BF16), vt_s[ci]], axis=1)
        o = ws[c:] + o0_s[ci] + _dot(arb_s[ci], bd(ub))
        upd = _dot(uvt, rhs_s[ci])
        state[...] = (state[...] + jnp.where(same_head, upd, 0.0)) * g_s[ci]
        return o

    finish(n_chunks - 1, lax.fori_loop(0, n_chunks, sweep, jnp.zeros((c, w), F32)))


def _wide_scan(r, lw, k, v, kk, asg, r_k, ln_g, ln_b, batch, seq):
    t, d = r.shape
    rb = SCAN_ROWS
    w = SCAN_LANES
    nc = seq // rb
    n_chunks = rb // CHUNK
    assert d % w == 0 and seq % rb == 0 and n_chunks % SCAN_GROUP == 0
    blk = pl.BlockSpec((rb, w), lambda b, hp, ci: (b * nc + ci, hp))
    vec = pl.BlockSpec((1, w), lambda b, hp, ci: (0, hp))
    per_chunk = lambda rows, cols, dt: pltpu.VMEM((n_chunks, rows, cols), dt)
    scratch = [pltpu.VMEM((w, w), F32),
               per_chunk(2 * CHUNK, w, BF16),
               per_chunk(CHUNK, w, F32),
               per_chunk(w, CHUNK, F32),
               per_chunk(w, CHUNK, BF16),
               per_chunk(CHUNK, w, F32),
               per_chunk(CHUNK, w, BF16),
               per_chunk(2 * CHUNK, w, BF16),
               per_chunk(1, w, F32)]
    return pl.pallas_call(
        _wide_scan_body,
        grid=(batch, d // w, nc),
        in_specs=[blk] * 6 + [vec] * 3, out_specs=blk,
        out_shape=jax.ShapeDtypeStruct((t, d), BF16),
        scratch_shapes=scratch,
        compiler_params=_params("parallel", "parallel", "arbitrary"), name="rwkv_scan",
    )(r, lw, k, v, kk, asg, r_k.reshape(1, d), ln_g.reshape(1, d), ln_b.reshape(1, d))


def _norm_linear_body(x_ref, g_ref, w_ref, o_ref):
    o_ref[...] = _dot(_rms(x_ref[...], g_ref[...]).astype(BF16), w_ref[...]).astype(o_ref.dtype)


def _norm_linear(x, g, w, out_dtype):
    t, d = x.shape
    tm = ROW_TILE
    return pl.pallas_call(
        _norm_linear_body, grid=(t // tm,),
        in_specs=[pl.BlockSpec((tm, d), lambda i: (i, 0)), _full((1, d)), _full(w.shape)],
        out_specs=pl.BlockSpec((tm, w.shape[1]), lambda i: (i, 0)),
        out_shape=jax.ShapeDtypeStruct((t, w.shape[1]), out_dtype),
        compiler_params=_params("parallel"), name="norm_linear",
    )(x, g.reshape(1, d), w.astype(BF16))


def _mem_attn_body(x_ref, a_ref, *rest, gated):
    if gated:
        mix = a_ref[...] * rest[0][...]
        rest = rest[1:]
    else:
        mix = a_ref[...]
    wm_ref, kv_ref, gn_ref, wq_ref, wo_ref, gf_ref, wrh_ref, wrl_ref, br_ref, x_o, xn_o, lg_o = rest
    x = x_ref[...] + _dot(mix, wm_ref[...])
    d = x.shape[1]
    hd = d // MEM_HEADS
    q = _dot(_rms(x, gn_ref[...]).astype(BF16), wq_ref[...])
    outs = []
    for hh in range(MEM_HEADS):
        qh = q[:, hh * hd:(hh + 1) * hd].astype(BF16)
        kh = kv_ref[:, hh * hd:(hh + 1) * hd]
        vh = kv_ref[:, d + hh * hd:d + (hh + 1) * hd]
        s = _dot_nt(qh, kh)
        p = jnp.exp(s - jnp.max(s, axis=-1, keepdims=True))
        o = _dot(p.astype(BF16), vh) / jnp.sum(p, axis=-1, keepdims=True)
        outs.append(o.astype(BF16))
    xnew = x + _dot(jnp.concatenate(outs, axis=1), wo_ref[...])
    x_o[...] = xnew
    xn = _rms(xnew, gf_ref[...])
    xn_o[...] = xn
    xh = xn.astype(BF16)
    xl = (xn - xh.astype(F32)).astype(BF16)
    tm = x.shape[0]
    hw = _dot(jnp.concatenate([xh, xl], axis=0), wrh_ref[...])
    lg_o[...] = hw[:tm] + hw[tm:] + _dot(xh, wrl_ref[...]) + br_ref[...]


def _mem_attn(x, mix, gate, w_mix, kv, seq, mem_tokens, gn, w_q, w_o, g_ffn, w_router, b_router):
    t, d = x.shape
    tm = ROW_TILE
    per_seq = seq // tm
    row = pl.BlockSpec((tm, d), lambda i: (i, 0))
    w_hi = w_router.astype(BF16)
    acts = [x, mix] + ([gate] if gate is not None else [])
    return pl.pallas_call(
        functools.partial(_mem_attn_body, gated=gate is not None), grid=(t // tm,),
        in_specs=[row] * len(acts) + [_full((d, d)), pl.BlockSpec((mem_tokens, 2 * d), lambda i: (i // per_seq, 0)),
                                      _full((1, d)), _full((d, d)), _full((d, d)), _full((1, d)),
                                      _full((d, LANES)), _full((d, LANES)), _full((1, LANES))],
        out_specs=[row, row, pl.BlockSpec((tm, LANES), lambda i: (i, 0))],
        out_shape=[jax.ShapeDtypeStruct((t, d), F32), jax.ShapeDtypeStruct((t, d), F32),
                   jax.ShapeDtypeStruct((t, LANES), F32)],
        compiler_params=_params("parallel"), name="mem_attn",
    )(*acts, w_mix.astype(BF16), kv, gn.reshape(1, d), (w_q * (d // MEM_HEADS) ** -0.5).astype(BF16),
      w_o.astype(BF16), g_ffn.reshape(1, d), w_hi, (w_router - w_hi.astype(F32)).astype(BF16), b_router)


def _route_body(lg_ref, idx_o, gate_o, cnt_o, carry):
    @pl.when(pl.program_id(0) == 0)
    def _():
        carry[...] = jnp.zeros_like(carry)

    lg = lg_ref[...]
    tm = lg.shape[0]
    lane = lax.broadcasted_iota(I32, lg.shape, 1)
    lanef = lane.astype(F32)
    big = 1e9
    ninf = -jnp.inf
    gl = jnp.where(lane < N_GROUPS, lg, ninf)
    gmax = jnp.max(gl, axis=-1, keepdims=True)
    g_w = 1.0 / jnp.sum(jnp.exp(gl - gmax), axis=-1, keepdims=True)
    grp = jnp.min(jnp.where(gl == gmax, lanef, big), axis=-1, keepdims=True).astype(I32)
    in_grp = (lane >= N_GROUPS) & (lane < N_GROUPS + N_EXPERTS) & (((lane - N_GROUPS) >> 3) == grp)
    el = jnp.where(in_grp, lg, ninf)
    m1 = jnp.max(el, axis=-1, keepdims=True)
    i1 = jnp.min(jnp.where(el == m1, lanef, big), axis=-1, keepdims=True)
    el2 = jnp.where(lanef == i1, ninf, el)
    m2 = jnp.max(el2, axis=-1, keepdims=True)
    i2 = jnp.min(jnp.where(el2 == m2, lanef, big), axis=-1, keepdims=True)
    tt = jnp.exp(m2 - m1)
    w1 = 1.0 / (1.0 + tt)
    w2 = tt / (1.0 + tt)
    hit1 = lanef == i1
    hit2 = lanef == i2
    oh = jnp.where(hit1 | hit2, 1.0, 0.0)
    r_i = lax.broadcasted_iota(I32, (tm, tm), 0)
    c_i = lax.broadcasted_iota(I32, (tm, tm), 1)
    tri = jnp.where(r_i > c_i, 1.0, 0.0).astype(BF16)
    pos = _dot(tri, oh.astype(BF16)) + carry[...]
    pos1 = jnp.sum(jnp.where(hit1, pos, 0.0), axis=-1, keepdims=True)
    pos2 = jnp.sum(jnp.where(hit2, pos, 0.0), axis=-1, keepdims=True)
    carry[...] = carry[...] + jnp.sum(oh, axis=0, keepdims=True)
    cnt_o[...] = carry[...]
    idx = jnp.where(lane == 0, i1 - N_GROUPS, jnp.where(lane == 1, i2 - N_GROUPS,
          jnp.where(lane == 2, pos1, jnp.where(lane == 3, pos2, 0.0))))
    idx_o[...] = idx.astype(I32)
    gate_o[...] = jnp.where(lane == 0, g_w * w1, jnp.where(lane == 1, g_w * w2, 0.0))


def _route(logits):
    t = logits.shape[0]
    tm = ROW_TILE
    row = pl.BlockSpec((tm, LANES), lambda i: (i, 0))
    return pl.pallas_call(
        _route_body, grid=(t // tm,), in_specs=[row],
        out_specs=[row, row, _full((1, LANES))],
        out_shape=[jax.ShapeDtypeStruct((t, LANES), I32), jax.ShapeDtypeStruct((t, LANES), F32),
                   jax.ShapeDtypeStruct((1, LANES), F32)],
        scratch_shapes=[pltpu.VMEM((1, LANES), F32)],
        compiler_params=_params("arbitrary"), name="moe_route",
    )(logits)


def _row_copy(src, s, dst, d, sem):
    return pltpu.make_async_copy(src.at[pl.ds(s, 1)], dst.at[pl.ds(d, 1)], sem)


def _dispatch_body(pend_ref, pad_ref, dest_ref, xn_ref, xs_out, zero_buf, sem):
    tm = xn_ref.shape[0]
    blk = zero_buf.shape[0]

    @pl.when(pl.program_id(0) == 0)
    def _():
        zero_buf[...] = jnp.zeros_like(zero_buf)

        def clear_block(start):
            cp = pltpu.make_async_copy(zero_buf, xs_out.at[pl.ds(pl.multiple_of(start, blk), blk)], sem)
            cp.start()
            cp.wait()

        def clear(e, c):
            @pl.when(pad_ref[e] > 0)
            def _():
                clear_block(pend_ref[e] - blk)
            return c

        def clear_tail(j, c):
            clear_block(j * blk)
            return c

        lax.fori_loop(0, N_EXPERTS, clear, 0)
        lax.fori_loop(pend_ref[N_EXPERTS - 1] // blk, xs_out.shape[0] // blk, clear_tail, 0)

    def issue(r, c):
        _row_copy(xn_ref, r, xs_out, dest_ref[0, 0, r], sem).start()
        _row_copy(xn_ref, r, xs_out, dest_ref[0, 0, tm + r], sem).start()
        return c

    lax.fori_loop(0, tm, issue, 0, unroll=DMA_UNROLL)
    for _ in range(2):
        pltpu.make_async_copy(xn_ref, xs_out.at[pl.ds(0, tm)], sem).wait()


def _dispatch(xn, dest, pend, padded, cap):
    t, d = xn.shape
    tm = ROW_TILE
    grid_spec = pltpu.PrefetchScalarGridSpec(
        num_scalar_prefetch=2, grid=(t // tm,),
        in_specs=[pl.BlockSpec((1, 1, 2 * tm), lambda i, pe, pa: (i, 0, 0), memory_space=pltpu.SMEM),
                  pl.BlockSpec((tm, d), lambda i, pe, pa: (i, 0))],
        out_specs=pl.BlockSpec(memory_space=pl.ANY),
        scratch_shapes=[pltpu.VMEM((MOE_BLOCK, d), F32), pltpu.SemaphoreType.DMA])
    return pl.pallas_call(
        _dispatch_body, grid_spec=grid_spec, out_shape=jax.ShapeDtypeStruct((cap, d), F32),
        compiler_params=_params("arbitrary"), name="moe_dispatch",
    )(pend, padded, dest, xn)


def _experts_body(be_ref, nb_ref, xs_ref, wgu_ref, wd_ref, ys_ref):
    del be_ref
    live = pl.program_id(0) < nb_ref[0]

    @pl.when(live)
    def _():
        gu = _dot(xs_ref[...].astype(BF16), wgu_ref[0])
        f = gu.shape[1] // 2
        gt = gu[:, :f]
        hid = gt * _sigmoid(gt) * gu[:, f:]
        ys_ref[...] = _dot(hid.astype(BF16), wd_ref[0])

    @pl.when(jnp.logical_not(live))
    def _():
        ys_ref[...] = jnp.zeros_like(ys_ref)


def _experts(xs, block_expert, n_live, w_gate_up, w_down):
    cap, d = xs.shape
    f2 = w_gate_up.shape[2]
    blk = MOE_BLOCK
    grid_spec = pltpu.PrefetchScalarGridSpec(
        num_scalar_prefetch=2, grid=(cap // blk,),
        in_specs=[pl.BlockSpec((blk, d), lambda i, be, nb: (jnp.minimum(i, nb[0] - 1), 0)),
                  pl.BlockSpec((1, d, f2), lambda i, be, nb: (be[i], 0, 0)),
                  pl.BlockSpec((1, f2 // 2, d), lambda i, be, nb: (be[i], 0, 0))],
        out_specs=pl.BlockSpec((blk, d), lambda i, be, nb: (i, 0)))
    return pl.pallas_call(
        _experts_body, grid_spec=grid_spec, out_shape=jax.ShapeDtypeStruct((cap, d), F32),
        compiler_params=_params("arbitrary"), name="moe_experts",
    )(block_expert, n_live, xs, w_gate_up, w_down)


def _combine_body(dest_ref, x_ref, gate_ref, ys_ref, *rest, n_proj, final_norm):
    norm_refs = rest[:n_proj + (1 if final_norm else 0)]
    w_refs = rest[len(norm_refs):len(norm_refs) + n_proj]
    outs = rest[len(norm_refs) + n_proj:-3]
    buf1, buf2, sem = rest[-3:]
    tm = x_ref.shape[0]

    def issue(r, c):
        _row_copy(ys_ref, dest_ref[0, 0, r], buf1, r, sem).start()
        _row_copy(ys_ref, dest_ref[0, 0, tm + r], buf2, r, sem).start()
        return c

    lax.fori_loop(0, tm, issue, 0, unroll=DMA_UNROLL)
    for buf in (buf1, buf2):
        pltpu.make_async_copy(ys_ref.at[pl.ds(0, tm)], buf, sem).wait()
    gate = gate_ref[...]
    x = x_ref[...] + gate[:, 0:1] * buf1[...] + gate[:, 1:2] * buf2[...]
    if final_norm:
        outs[0][...] = _rms(x, norm_refs[0][...])
        return
    outs[0][...] = x
    for j in range(n_proj):
        outs[1 + j][...] = _dot(_rms(x, norm_refs[j][...]).astype(BF16), w_refs[j][...]).astype(outs[1 + j].dtype)


def _combine(x, ys, dest, gates, norms, weights, final_norm=False):
    t, d = x.shape
    tm = ROW_TILE
    row = pl.BlockSpec((tm, d), lambda i: (i, 0))
    in_specs = [pl.BlockSpec((1, 1, 2 * tm), lambda i: (i, 0, 0), memory_space=pltpu.SMEM),
                row, pl.BlockSpec((tm, LANES), lambda i: (i, 0)), pl.BlockSpec(memory_space=pl.ANY)]
    in_specs += [_full((1, d))] * len(norms) + [_full(w.shape) for w in weights]
    out_specs = [row] + [pl.BlockSpec((tm, w.shape[1]), lambda i: (i, 0)) for w in weights]
    out_shape = [jax.ShapeDtypeStruct((t, d), F32)]
    out_shape += [jax.ShapeDtypeStruct((t, w.shape[1]), BF16) for w in weights]
    return pl.pallas_call(
        functools.partial(_combine_body, n_proj=len(weights), final_norm=final_norm),
        grid=(t // tm,), in_specs=in_specs, out_specs=out_specs, out_shape=out_shape,
        scratch_shapes=[pltpu.VMEM((tm, d), F32), pltpu.VMEM((tm, d), F32), pltpu.SemaphoreType.DMA],
        compiler_params=_params("arbitrary"), name="moe_combine",
    )(dest, x, gates, ys, *[g.reshape(1, d) for g in norms], *[w.astype(BF16) for w in weights])


def _moe(x, xn, logits, w_gate, w_up, w_down, norms, weights, final_norm=False):
    t, d = x.shape
    blk = MOE_BLOCK
    tm = ROW_TILE
    idx, gates, counts = _route(logits)
    counts = counts[0, N_GROUPS:N_GROUPS + N_EXPERTS].astype(I32)
    padded = (counts + blk - 1) // blk * blk
    pend = jnp.cumsum(padded)
    pstart = pend - padded
    cap = 2 * t + N_EXPERTS * blk
    n_blocks = cap // blk
    starts = jnp.arange(n_blocks, dtype=I32) * blk
    block_expert = jnp.minimum(jnp.sum(pend[None, :] <= starts[:, None], axis=1), N_EXPERTS - 1).astype(I32)
    n_live = (pend[-1:] // blk).astype(I32)
    chosen = idx[:, 0:2, None] == jnp.arange(N_EXPERTS, dtype=I32)
    dest = jnp.sum(jnp.where(chosen, pstart, 0), axis=-1) + idx[:, 2:4]
    dest = dest.reshape(t // tm, tm, 2).transpose(0, 2, 1).reshape(t // tm, 1, 2 * tm)
    xs = _dispatch(xn, dest, pend.astype(I32), padded.astype(I32), cap)
    w_gu = jnp.concatenate([w_gate, w_up], axis=2).astype(BF16)
    ys = _experts(xs, block_expert, n_live, w_gu, w_down.astype(BF16))
    return _combine(x, ys, dest, gates, norms, weights, final_norm)


def _sb_masks():
    row = lax.broadcasted_iota(I32, (SB_BLOCK, SB_WINDOW), 0)
    col = lax.broadcasted_iota(I32, (SB_BLOCK, SB_WINDOW), 1)
    visible = jnp.stack([col < row, col < row + SB_BLOCK, col >= 0, col < SB_BLOCK])
    return jnp.where(visible, 0.0, SB_HIDDEN).astype(F32)


def _sb_attn_body(q_ref, k_ref, v_ref, tri_ref, bias_ref, o_ref):
    i = pl.program_id(2)
    blk, win = SB_BLOCK, SB_WINDOW
    nh = q_ref.shape[1] // HEAD
    neg_suffix = tri_ref[...]
    lanes = [slice(h * HEAD, (h + 1) * HEAD) for h in range(nh)]
    qs = [(q_ref[:, sl].astype(F32) * (HEAD ** -0.5 * LOG2E)).astype(BF16) for sl in lanes]
    u32 = jnp.uint32

    def more(carry):
        hi, go = carry[0], carry[1]
        return jnp.logical_and(hi > 0, go)

    def window(carry):
        hi, _, accs, rests = carry
        lo = jnp.maximum(hi - win, 0)
        rows = pl.ds(pl.multiple_of(lo, blk), win)
        first = hi == (i + 1) * blk
        which = jnp.where(first, jnp.where(i == 0, 0, 1), jnp.where(hi < win, 3, 2))
        bias = bias_ref[which]
        zs, parts = [], []
        for h in range(nh):
            z = _dot_nt(qs[h], k_ref[rows, lanes[h]]) + bias
            neg_abs = lax.bitcast_convert_type(lax.bitcast_convert_type(z, u32) | u32(0x80000000), F32)
            sp = jnp.maximum(z, 0.0) + jnp.log(1.0 + jnp.exp2(neg_abs)) * LOG2E
            top = lax.bitcast_convert_type(lax.bitcast_convert_type(sp, u32) & u32(0xFFFF0000), F32)
            zs.append(z)
            parts += [top.astype(BF16), (sp - top).astype(BF16)]
        sums = _dot(jnp.concatenate(parts, axis=0), neg_suffix)
        new_accs, new_rests = [], []
        for h in range(nh):
            sfx = sums[2 * h * blk:(2 * h + 1) * blk] + sums[(2 * h + 1) * blk:(2 * h + 2) * blk]
            attn = jnp.exp2(zs[h] + sfx + rests[h])
            new_accs.append(accs[h] + _dot(attn.astype(BF16), v_ref[rows, lanes[h]]))
            new_rests.append(rests[h] + sfx[:, 0:1])
        worst = functools.reduce(jnp.maximum, new_rests)
        return lo, jnp.max(worst) > SB_ZERO_LOG2, tuple(new_accs), tuple(new_rests)

    init = ((i + 1) * blk, True, tuple(jnp.zeros((blk, HEAD), F32) for _ in range(nh)),
            tuple(jnp.zeros((blk, 1), F32) for _ in range(nh)))
    accs = lax.while_loop(more, window, window(init))[2]
    o_ref[...] = jnp.concatenate(accs, axis=1).astype(o_ref.dtype)


def _sb_attn(q, kv, batch, seq):
    t, d = q.shape
    nq = seq // SB_BLOCK
    w = SB_LANES
    hp = d // w
    win = SB_WINDOW
    assert d % w == 0 and seq % SB_BLOCK == 0 and seq >= win and win == 2 * SB_BLOCK
    tri = -jnp.tril(jnp.ones((win, win), F32)).astype(BF16)
    return pl.pallas_call(
        _sb_attn_body, grid=(batch, hp, nq),
        in_specs=[pl.BlockSpec((SB_BLOCK, w), lambda b, h, i: (b * nq + i, h)),
                  pl.BlockSpec((seq, w), lambda b, h, i: (b, h)),
                  pl.BlockSpec((seq, w), lambda b, h, i: (b, hp + h)),
                  pl.BlockSpec((win, win), lambda b, h, i: (0, 0)),
                  pl.BlockSpec((4, SB_BLOCK, win), lambda b, h, i: (0, 0, 0))],
        out_specs=pl.BlockSpec((SB_BLOCK, w), lambda b, h, i: (b * nq + i, h)),
        out_shape=jax.ShapeDtypeStruct((t, d), BF16),
        compiler_params=_params("parallel", "parallel", "arbitrary"), name="sb_attn",
    )(q, kv, kv, tri, _sb_masks())


def _router_weights(w_group, b_group, w_expert, b_expert):
    d = w_group.shape[0]
    pad = LANES - N_GROUPS - N_EXPERTS
    w = jnp.concatenate([w_group, w_expert, jnp.zeros((d, pad), F32)], axis=1)
    b = jnp.concatenate([b_group, b_expert, jnp.zeros((pad,), F32)]).reshape(1, LANES)
    return w, b


def kernel(x, mem, norm_mix, norm_mem, norm_ffn, norm_final, rwkv_mu, rwkv_w_rkv, rwkv_w0, rwkv_w1, rwkv_w2, rwkv_a0, rwkv_a1, rwkv_a2, rwkv_g1, rwkv_g2, rwkv_k_k, rwkv_k_a, rwkv_r_k, rwkv_ln_g, rwkv_ln_b, rwkv_w_o, kv_norm, w_kv_shared, sb_w_q, sb_w_o, mem_norm_kv, mem_w_q, mem_w_kv, mem_w_o, moe_w_group, moe_b_group, moe_w_expert, moe_b_expert, moe_w_gate, moe_w_up, moe_w_down):
    batch, seq, d = x.shape
    depth = norm_mix.shape[0]
    n_a = rwkv_mu.shape[0]
    mem_tokens = mem.shape[1]
    xs = x.reshape(batch * seq, d)
    memf = mem.reshape(batch * mem_tokens, d)
    q = kv = None
    for l in range(depth):
        if l < n_a:
            r, lw, k, v, kk, asg, g = _rwkv_proj(
                xs, seq, norm_mix[l], rwkv_mu[l], rwkv_w_rkv[l], rwkv_w0[l], rwkv_w1[l], rwkv_w2[l],
                rwkv_a0[l], rwkv_a1[l], rwkv_a2[l], rwkv_g1[l], rwkv_g2[l], rwkv_k_k[l], rwkv_k_a[l])
            mix = _wide_scan(r, lw, k, v, kk, asg, rwkv_r_k[l], rwkv_ln_g[l], rwkv_ln_b[l], batch, seq)
            gate, w_mix = g, rwkv_w_o[l]
        else:
            mix = _sb_attn(q, kv, batch, seq)
            gate, w_mix = None, sb_w_o[l - n_a]
        mkv = _norm_linear(memf, mem_norm_kv[l], mem_w_kv[l], BF16)
        w_router, b_router = _router_weights(moe_w_group[l], moe_b_group[l], moe_w_expert[l], moe_b_expert[l])
        xs, xn, logits = _mem_attn(xs, mix, gate, w_mix, mkv, seq, mem_tokens, norm_mem[l], mem_w_q[l], mem_w_o[l],
                                   norm_ffn[l], w_router, b_router)
        last = l == depth - 1
        if last:
            norms, weights = [norm_final], []
        elif l + 1 < n_a:
            norms, weights = [], []
        else:
            norms, weights = [norm_mix[l + 1]], [sb_w_q[l + 1 - n_a]]
            if l == n_a - 1:
                norms, weights = norms + [kv_norm], weights + [w_kv_shared]
        res = _moe(xs, xn, logits, moe_w_gate[l], moe_w_up[l], moe_w_down[l], norms, weights, final_norm=last)
        if last:
            return res[0].reshape(batch, seq, d)
        xs = res[0]
        if weights:
            q = res[1]
            if l == n_a - 1:
                kv = res[2]
```

```python
import functools

import jax
import jax.numpy as jnp
from jax import lax
from jax.experimental import pallas as pl
from jax.experimental.pallas import tpu as pltpu

F32 = jnp.float32
BF16 = jnp.bfloat16
I32 = jnp.int32

HEAD = 64
MEM_HEADS = 4
N_GROUPS = 4
EXPERTS_PER_GROUP = 8
N_EXPERTS = N_GROUPS * EXPERTS_PER_GROUP
NORM_EPS = 1e-6
RWKV_LN_EPS = 64e-5

LANES = 128
CHUNK = 64
SB_BLOCK = 128
SB_WINDOW = 256
SB_LANES = 512
SB_ZERO_LOG2 = -160.0
SB_HIDDEN = -1e30
LOG2E = 1.4426950408889634
MOE_BLOCK = 512
DMA_UNROLL = 8
ROW_TILE = 512
PROJ_TILE = 512
SCAN_ROWS = 512
SCAN_LANES = 256
SCAN_GROUP = 8
VMEM_LIMIT = 56 * 1024 * 1024


def _params(*sem):
    return pltpu.CompilerParams(dimension_semantics=sem, vmem_limit_bytes=VMEM_LIMIT)


def _dot(a, b):
    return jnp.dot(a, b, preferred_element_type=F32)


def _dot_nt(a, b):
    return lax.dot_general(a, b, (((1,), (1,)), ((), ())), preferred_element_type=F32)


def _dot_tn(a, b):
    return lax.dot_general(a, b, (((0,), (0,)), ((), ())), preferred_element_type=F32)


def _rms(x, g):
    ms = jnp.mean(x * x, axis=-1, keepdims=True)
    return x * lax.rsqrt(ms + NORM_EPS) * g


def _sigmoid(x):
    return 1.0 / (1.0 + jnp.exp(-x))


def _softplus(x):
    return jnp.maximum(x, 0.0) + jnp.log(1.0 + jnp.exp(-jnp.abs(x)))


def _full(shape):
    return pl.BlockSpec(shape, lambda *_: (0,) * len(shape))


def _rwkv_proj_body(x_ref, xp_ref, gn_ref, mu_ref, wr_ref, wk_ref, wv_ref, w1_ref, a1_ref, g1_ref,
                    w2_ref, a2_ref, g2_ref, vec_ref,
                    r_o, lw_o, k_o, v_o, kk_o, as_o, g_o, *, tiles_per_seq):
    i = pl.program_id(0)
    gn = gn_ref[...]
    h = _rms(x_ref[...], gn)
    hp = _rms(xp_ref[...], gn)[7:8, :]
    hp = jnp.where(lax.rem(i, tiles_per_seq) == 0, 0.0, hp)
    row = lax.broadcasted_iota(I32, (h.shape[0], 1), 0)
    hprev = jnp.where(row == 0, hp, pltpu.roll(h, 1, 0))
    xx = hprev - h
    mu = mu_ref[...]

    def lerp(j):
        return (h + xx * mu[j:j + 1, :]).astype(BF16)

    vec = vec_ref[...]
    r = _dot(lerp(0), wr_ref[...])
    k = _dot(lerp(1), wk_ref[...])
    v = _dot(lerp(2), wv_ref[...])
    w = vec[0:1] + _dot(jnp.tanh(_dot(lerp(3), w1_ref[...])).astype(BF16), w2_ref[...])
    w = -_softplus(-w) - 0.5
    a = _sigmoid(vec[1:2] + _dot(_dot(lerp(4), a1_ref[...]).astype(BF16), a2_ref[...]))
    g = _dot(_sigmoid(_dot(lerp(5), g1_ref[...])).astype(BF16), g2_ref[...])
    r_o[...] = r.astype(BF16)
    lw_o[...] = -jnp.exp(w)
    k_o[...] = (k * (1.0 + (a - 1.0) * vec[3:4])).astype(BF16)
    v_o[...] = v.astype(BF16)
    kk_o[...] = (k * vec[2:3]).astype(BF16)
    as_o[...] = a.astype(BF16)
    g_o[...] = g.astype(BF16)


def _rwkv_proj(x, seq, gn, mu, w_rkv, w0, w1, w2, a0, a1, a2, g1, g2, k_k, k_a):
    t, d = x.shape
    tm = PROJ_TILE
    pad8 = lambda rows: jnp.concatenate([rows, jnp.zeros((8 - rows.shape[0], d), F32)], axis=0)
    vec = pad8(jnp.stack([w0, a0, k_k, k_a]))
    mu8 = pad8(mu)
    wts = [w_rkv[0], w_rkv[1], w_rkv[2], w1, a1, g1, w2, a2, g2]
    wts = [w.astype(BF16) for w in wts]
    row_spec = pl.BlockSpec((tm, d), lambda i: (i, 0))
    prev_spec = pl.BlockSpec((8, d), lambda i: (jnp.maximum(i * (tm // 8) - 1, 0), 0))
    in_specs = [row_spec, prev_spec, _full((1, d)), _full((8, d))]
    in_specs += [_full(w.shape) for w in wts] + [_full((8, d))]
    outs = [jax.ShapeDtypeStruct((t, d), dt) for dt in (BF16, F32, BF16, BF16, BF16, BF16, BF16)]
    return pl.pallas_call(
        functools.partial(_rwkv_proj_body, tiles_per_seq=seq // tm),
        grid=(t // tm,), in_specs=in_specs, out_specs=[row_spec] * 7, out_shape=outs,
        compiler_params=_params("parallel"), name="rwkv_proj",
    )(x, x, gn.reshape(1, d), mu8, *wts, vec)


def _cumsum_rows(x):
    n = x.shape[0]
    row = lax.broadcasted_iota(I32, (n, 1), 0)
    s = 1
    while s < n:
        x = x + jnp.where(row >= s, pltpu.roll(x, s, 0), 0.0)
        s *= 2
    return x


def _split_bf16(x):
    top = lax.bitcast_convert_type(lax.bitcast_convert_type(x, jnp.uint32) & jnp.uint32(0xFFFF0000), F32)
    return top.astype(BF16), (x - top).astype(BF16)


def _wide_scan_body(r_ref, lw_ref, k_ref, v_ref, kk_ref, as_ref, rk_ref, lg_ref, lb_ref, y_ref,
                    state, o0_s, n_s, gc_s, g_s):
    c = CHUNK
    n = HEAD
    w = r_ref.shape[1]
    nh = w // n
    n_chunks = r_ref.shape[0] // c

    @pl.when(pl.program_id(2) == 0)
    def _():
        state[...] = jnp.zeros_like(state)

    row = lax.broadcasted_iota(I32, (c, w), 0)
    col = lax.broadcasted_iota(I32, (c, w), 1) & (n - 1)
    strict = row > col
    incl = row >= col
    eye = (row == col).astype(F32)
    same_head = (lax.broadcasted_iota(I32, (w, w), 0) // n) == (lax.broadcasted_iota(I32, (w, w), 1) // n)
    head_ones = jnp.where(same_head, 1.0, 0.0).astype(BF16)
    eye_c = (lax.broadcasted_iota(I32, (c, c), 0) == lax.broadcasted_iota(I32, (c, c), 1)).astype(BF16)

    def bd(x):
        return jnp.concatenate([x] * nh, axis=0) * head_ones

    def head_sums(*xs):
        parts = [p for x in xs for p in _split_bf16(x)]
        sums = _dot(jnp.concatenate(parts, axis=0), head_ones)
        return [sums[2 * j * c:(2 * j + 1) * c] + sums[(2 * j + 1) * c:(2 * j + 2) * c] for j in range(len(xs))]

    def local(cj, carry):
        cis, rowss, a_alls, r_alls, lhss, rhss, g_lasts = [], [], [], [], [], [], []
        for sub in range(SCAN_GROUP):
            ci = cj * SCAN_GROUP + sub
            rows = pl.ds(pl.multiple_of(ci * c, c), c)
            r = r_ref[rows, :].astype(F32)
            lw = lw_ref[rows, :]
            k = k_ref[rows, :].astype(F32)
            kkr = kk_ref[rows, :].astype(F32)
            asg = as_ref[rows, :].astype(F32)
            cum = _cumsum_rows(lw)
            g_in = jnp.exp(cum)
            g_ex = jnp.exp(cum - lw)
            g_inv = jnp.exp(-cum)
            g_s[ci] = g_in[c - 1:c, :]
            g_lasts.append(g_in[c - 1:c, :])
            kkn = kkr * lax.rsqrt(jnp.maximum(head_sums(kkr * kkr)[0], 1e-24))
            a_all = (-kkn * g_ex).astype(BF16)
            b_all = (kkn * asg * g_inv).astype(BF16)
            k_all = (k * g_inv).astype(BF16)
            r_all = (r * g_in).astype(BF16)
            cis.append(ci)
            rowss.append(rows)
            a_alls.append(a_all)
            r_alls.append(r_all)
            lhss.append(jnp.concatenate([a_all, r_all], axis=0))
            rhss.append((b_all, k_all))
        pbs = [_dot_nt(lhs, bd(b)) for lhs, (b, _) in zip(lhss, rhss)]
        pks = [_dot_nt(lhs, bd(k)) for lhs, (_, k) in zip(lhss, rhss)]
        lps = [jnp.where(strict, pb[:c], 0.0) for pb in pbs]
        xos = [_dot(jnp.concatenate([jnp.where(strict, pk[:c], 0.0), jnp.where(incl, pk[c:], 0.0)],
                                    axis=0).astype(BF16), bd(v_ref[rows, :])) for pk, rows in zip(pks, rowss)]
        arbs = [jnp.where(incl, pb[c:], 0.0).astype(BF16) for pb in pbs]
        tinvs = [eye + lp for lp in lps]
        lpbs = [lp.astype(BF16) for lp in lps]
        lpbs = [_dot(lpb, bd(lpb)).astype(BF16) for lpb in lpbs]
        s = 4
        while s < c:
            both = [_dot(jnp.concatenate([t.astype(BF16), lpb], axis=0), bd(lpb)) for t, lpb in zip(tinvs, lpbs)]
            tinvs = [t + bo[:c] for t, bo in zip(tinvs, both)]
            lpbs = [bo[c:].astype(BF16) for bo in both]
            s *= 2
        tinvs = [t + _dot(t.astype(BF16), bd(lpb)) for t, lpb in zip(tinvs, lpbs)]
        tbs = [t.astype(BF16) for t in tinvs]
        wbs = [_dot(tb, bd(a)).astype(BF16) for tb, a in zip(tbs, a_alls)]
        u0bs = [_dot(tb, bd(xo[:c].astype(BF16))).astype(BF16) for tb, xo in zip(tbs, xos)]
        qs = [r_all.astype(F32) + _dot(arb, bd(wb)) for r_all, arb, wb in zip(r_alls, arbs, wbs)]
        o0s = [xo[c:] + _dot(arb, bd(u0b)) for xo, arb, u0b in zip(xos, arbs, u0bs)]
        ncs = [jnp.where(same_head, _dot_tn(wb, b), 0.0) * g for wb, (b, _), g in zip(wbs, rhss, g_lasts)]
        gcs = [jnp.where(same_head, _dot_tn(jnp.concatenate([u0b, v_ref[rows, :]], axis=0),
                                            jnp.concatenate([b, k], axis=0)), 0.0) * g
               for u0b, rows, (b, k), g in zip(u0bs, rowss, rhss, g_lasts)]
        qts = [_dot_tn(q.astype(BF16), eye_c) for q in qs]
        for ci, qt, o0, nc, gc in zip(cis, qts, o0s, ncs, gcs):
            o0_s[ci] = o0
            n_s[ci] = jnp.concatenate([nc, qt, jnp.zeros((w, LANES - c), F32)], axis=1).astype(BF16)
            gc_s[ci] = gc
        return carry

    lax.fori_loop(0, n_chunks // SCAN_GROUP, local, 0)

    rk = rk_ref[...]
    lng = lg_ref[...]
    lnb = lb_ref[...]

    def sweep(ci, carry):
        s = state[...]
        both = _dot(s.astype(BF16), n_s[ci])
        state[...] = s * g_s[ci] + both[:, :w] + gc_s[ci]
        o0_s[ci] = jnp.transpose(both[:, w:])[:c] + o0_s[ci]
        return carry

    lax.fori_loop(0, n_chunks, sweep, 0)

    rowss = [pl.ds(ci * c, c) for ci in range(n_chunks)]
    os_ = [o0_s[ci] for ci in range(n_chunks)]
    sums = [head_sums(o, r_ref[rows, :].astype(F32) * k_ref[rows, :].astype(F32) * rk) for o, rows in zip(os_, rowss)]
    devs = [o - o_sum * (1.0 / n) for o, (o_sum, _) in zip(os_, sums)]
    vars_ = [head_sums(dev * dev)[0] * (1.0 / n) for dev in devs]
    for rows, dev, var, (_, bonus) in zip(rowss, devs, vars_, sums):
        y = dev * lax.rsqrt(var + RWKV_LN_EPS) * lng + lnb
        y_ref[rows, :] = (y + bonus * v_ref[rows, :].astype(F32)).astype(y_ref.dtype)


def _wide_scan(r, lw, k, v, kk, asg, r_k, ln_g, ln_b, batch, seq):
    t, d = r.shape
    rb = SCAN_ROWS
    w = SCAN_LANES
    nc = seq // rb
    n_chunks = rb // CHUNK
    assert d % w == 0 and seq % rb == 0 and n_chunks % SCAN_GROUP == 0
    blk = pl.BlockSpec((rb, w), lambda b, hp, ci: (b * nc + ci, hp))
    vec = pl.BlockSpec((1, w), lambda b, hp, ci: (0, hp))
    per_chunk = lambda rows, cols, dt: pltpu.VMEM((n_chunks, rows, cols), dt)
    scratch = [pltpu.VMEM((w, w), F32),
               per_chunk(CHUNK, w, F32),
               per_chunk(w, w + LANES, BF16),
               per_chunk(w, w, F32),
               per_chunk(1, w, F32)]
    return pl.pallas_call(
        _wide_scan_body,
        grid=(batch, d // w, nc),
        in_specs=[blk] * 6 + [vec] * 3, out_specs=blk,
        out_shape=jax.ShapeDtypeStruct((t, d), BF16),
        scratch_shapes=scratch,
        compiler_params=_params("parallel", "parallel", "arbitrary"), name="rwkv_scan",
    )(r, lw, k, v, kk, asg, r_k.reshape(1, d), ln_g.reshape(1, d), ln_b.reshape(1, d))


def _norm_linear_body(x_ref, g_ref, w_ref, o_ref):
    o_ref[...] = _dot(_rms(x_ref[...], g_ref[...]).astype(BF16), w_ref[...]).astype(o_ref.dtype)


def _norm_linear(x, g, w, out_dtype):
    t, d = x.shape
    tm = ROW_TILE
    return pl.pallas_call(
        _norm_linear_body, grid=(t // tm,),
        in_specs=[pl.BlockSpec((tm, d), lambda i: (i, 0)), _full((1, d)), _full(w.shape)],
        out_specs=pl.BlockSpec((tm, w.shape[1]), lambda i: (i, 0)),
        out_shape=jax.ShapeDtypeStruct((t, w.shape[1]), out_dtype),
        compiler_params=_params("parallel"), name="norm_linear",
    )(x, g.reshape(1, d), w.astype(BF16))


def _mem_attn_body(x_ref, a_ref, *rest, gated):
    if gated:
        mix = a_ref[...] * rest[0][...]
        rest = rest[1:]
    else:
        mix = a_ref[...]
    wm_ref, kv_ref, gn_ref, wq_ref, wo_ref, gf_ref, wrh_ref, wrl_ref, br_ref, x_o, xn_o, lg_o = rest
    x = x_ref[...] + _dot(mix, wm_ref[...])
    d = x.shape[1]
    hd = d // MEM_HEADS
    q = _dot(_rms(x, gn_ref[...]).astype(BF16), wq_ref[...])
    outs = []
    for hh in range(MEM_HEADS):
        qh = q[:, hh * hd:(hh + 1) * hd].astype(BF16)
        kh = kv_ref[:, hh * hd:(hh + 1) * hd]
        vh = kv_ref[:, d + hh * hd:d + (hh + 1) * hd]
        s = _dot_nt(qh, kh)
        p = jnp.exp(s - jnp.max(s, axis=-1, keepdims=True))
        o = _dot(p.astype(BF16), vh) / jnp.sum(p, axis=-1, keepdims=True)
        outs.append(o.astype(BF16))
    xnew = x + _dot(jnp.concatenate(outs, axis=1), wo_ref[...])
    x_o[...] = xnew
    xn = _rms(xnew, gf_ref[...])
    xn_o[...] = xn
    xh = xn.astype(BF16)
    xl = (xn - xh.astype(F32)).astype(BF16)
    tm = x.shape[0]
    hw = _dot(jnp.concatenate([xh, xl], axis=0), wrh_ref[...])
    lg_o[...] = hw[:tm] + hw[tm:] + _dot(xh, wrl_ref[...]) + br_ref[...]


def _mem_attn(x, mix, gate, w_mix, kv, seq, mem_tokens, gn, w_q, w_o, g_ffn, w_router, b_router):
    t, d = x.shape
    tm = ROW_TILE
    per_seq = seq // tm
    row = pl.BlockSpec((tm, d), lambda i: (i, 0))
    w_hi = w_router.astype(BF16)
    acts = [x, mix] + ([gate] if gate is not None else [])
    return pl.pallas_call(
        functools.partial(_mem_attn_body, gated=gate is not None), grid=(t // tm,),
        in_specs=[row] * len(acts) + [_full((d, d)), pl.BlockSpec((mem_tokens, 2 * d), lambda i: (i // per_seq, 0)),
                                      _full((1, d)), _full((d, d)), _full((d, d)), _full((1, d)),
                                      _full((d, LANES)), _full((d, LANES)), _full((1, LANES))],
        out_specs=[row, row, pl.BlockSpec((tm, LANES), lambda i: (i, 0))],
        out_shape=[jax.ShapeDtypeStruct((t, d), F32), jax.ShapeDtypeStruct((t, d), F32),
                   jax.ShapeDtypeStruct((t, LANES), F32)],
        compiler_params=_params("parallel"), name="mem_attn",
    )(*acts, w_mix.astype(BF16), kv, gn.reshape(1, d), (w_q * (d // MEM_HEADS) ** -0.5).astype(BF16),
      w_o.astype(BF16), g_ffn.reshape(1, d), w_hi, (w_router - w_hi.astype(F32)).astype(BF16), b_router)


def _route_body(lg_ref, idx_o, gate_o, cnt_o, carry):
    @pl.when(pl.program_id(0) == 0)
    def _():
        carry[...] = jnp.zeros_like(carry)

    lg = lg_ref[...]
    tm = lg.shape[0]
    lane = lax.broadcasted_iota(I32, lg.shape, 1)
    lanef = lane.astype(F32)
    big = 1e9
    ninf = -jnp.inf
    gl = jnp.where(lane < N_GROUPS, lg, ninf)
    gmax = jnp.max(gl, axis=-1, keepdims=True)
    g_w = 1.0 / jnp.sum(jnp.exp(gl - gmax), axis=-1, keepdims=True)
    grp = jnp.min(jnp.where(gl == gmax, lanef, big), axis=-1, keepdims=True).astype(I32)
    in_grp = (lane >= N_GROUPS) & (lane < N_GROUPS + N_EXPERTS) & (((lane - N_GROUPS) >> 3) == grp)
    el = jnp.where(in_grp, lg, ninf)
    m1 = jnp.max(el, axis=-1, keepdims=True)
    i1 = jnp.min(jnp.where(el == m1, lanef, big), axis=-1, keepdims=True)
    el2 = jnp.where(lanef == i1, ninf, el)
    m2 = jnp.max(el2, axis=-1, keepdims=True)
    i2 = jnp.min(jnp.where(el2 == m2, lanef, big), axis=-1, keepdims=True)
    tt = jnp.exp(m2 - m1)
    w1 = 1.0 / (1.0 + tt)
    w2 = tt / (1.0 + tt)
    hit1 = lanef == i1
    hit2 = lanef == i2
    oh = jnp.where(hit1 | hit2, 1.0, 0.0)
    r_i = lax.broadcasted_iota(I32, (tm, tm), 0)
    c_i = lax.broadcasted_iota(I32, (tm, tm), 1)
    tri = jnp.where(r_i > c_i, 1.0, 0.0).astype(BF16)
    pos = _dot(tri, oh.astype(BF16)) + carry[...]
    pos1 = jnp.sum(jnp.where(hit1, pos, 0.0), axis=-1, keepdims=True)
    pos2 = jnp.sum(jnp.where(hit2, pos, 0.0), axis=-1, keepdims=True)
    carry[...] = carry[...] + jnp.sum(oh, axis=0, keepdims=True)
    cnt_o[...] = carry[...]
    idx = jnp.where(lane == 0, i1 - N_GROUPS, jnp.where(lane == 1, i2 - N_GROUPS,
          jnp.where(lane == 2, pos1, jnp.where(lane == 3, pos2, 0.0))))
    idx_o[...] = idx.astype(I32)
    gate_o[...] = jnp.where(lane == 0, g_w * w1, jnp.where(lane == 1, g_w * w2, 0.0))


def _route(logits):
    t = logits.shape[0]
    tm = ROW_TILE
    row = pl.BlockSpec((tm, LANES), lambda i: (i, 0))
    return pl.pallas_call(
        _route_body, grid=(t // tm,), in_specs=[row],
        out_specs=[row, row, _full((1, LANES))],
        out_shape=[jax.ShapeDtypeStruct((t, LANES), I32), jax.ShapeDtypeStruct((t, LANES), F32),
                   jax.ShapeDtypeStruct((1, LANES), F32)],
        scratch_shapes=[pltpu.VMEM((1, LANES), F32)],
        compiler_params=_params("arbitrary"), name="moe_route",
    )(logits)


def _row_copy(src, s, dst, d, sem):
    return pltpu.make_async_copy(src.at[pl.ds(s, 1)], dst.at[pl.ds(d, 1)], sem)


def _dispatch_body(pend_ref, pad_ref, dest_ref, xn_ref, xs_out, zero_buf, sem):
    tm = xn_ref.shape[0]
    blk = zero_buf.shape[0]

    @pl.when(pl.program_id(0) == 0)
    def _():
        zero_buf[...] = jnp.zeros_like(zero_buf)

        def clear_block(start):
            cp = pltpu.make_async_copy(zero_buf, xs_out.at[pl.ds(pl.multiple_of(start, blk), blk)], sem)
            cp.start()
            cp.wait()

        def clear(e, c):
            @pl.when(pad_ref[e] > 0)
            def _():
                clear_block(pend_ref[e] - blk)
            return c

        def clear_tail(j, c):
            clear_block(j * blk)
            return c

        lax.fori_loop(0, N_EXPERTS, clear, 0)
        lax.fori_loop(pend_ref[N_EXPERTS - 1] // blk, xs_out.shape[0] // blk, clear_tail, 0)

    def issue(r, c):
        _row_copy(xn_ref, r, xs_out, dest_ref[0, 0, r], sem).start()
        _row_copy(xn_ref, r, xs_out, dest_ref[0, 0, tm + r], sem).start()
        return c

    lax.fori_loop(0, tm, issue, 0, unroll=DMA_UNROLL)
    for _ in range(2):
        pltpu.make_async_copy(xn_ref, xs_out.at[pl.ds(0, tm)], sem).wait()


def _dispatch(xn, dest, pend, padded, cap):
    t, d = xn.shape
    tm = ROW_TILE
    grid_spec = pltpu.PrefetchScalarGridSpec(
        num_scalar_prefetch=2, grid=(t // tm,),
        in_specs=[pl.BlockSpec((1, 1, 2 * tm), lambda i, pe, pa: (i, 0, 0), memory_space=pltpu.SMEM),
                  pl.BlockSpec((tm, d), lambda i, pe, pa: (i, 0))],
        out_specs=pl.BlockSpec(memory_space=pl.ANY),
        scratch_shapes=[pltpu.VMEM((MOE_BLOCK, d), F32), pltpu.SemaphoreType.DMA])
    return pl.pallas_call(
        _dispatch_body, grid_spec=grid_spec, out_shape=jax.ShapeDtypeStruct((cap, d), F32),
        compiler_params=_params("arbitrary"), name="moe_dispatch",
    )(pend, padded, dest, xn)


def _experts_body(be_ref, nb_ref, xs_ref, wgu_ref, wd_ref, ys_ref):
    del be_ref
    live = pl.program_id(0) < nb_ref[0]

    @pl.when(live)
    def _():
        gu = _dot(xs_ref[...].astype(BF16), wgu_ref[0])
        f = gu.shape[1] // 2
        gt = gu[:, :f]
        hid = gt * _sigmoid(gt) * gu[:, f:]
        ys_ref[...] = _dot(hid.astype(BF16), wd_ref[0])

    @pl.when(jnp.logical_not(live))
    def _():
        ys_ref[...] = jnp.zeros_like(ys_ref)


def _experts(xs, block_expert, n_live, w_gate_up, w_down):
    cap, d = xs.shape
    f2 = w_gate_up.shape[2]
    blk = MOE_BLOCK
    grid_spec = pltpu.PrefetchScalarGridSpec(
        num_scalar_prefetch=2, grid=(cap // blk,),
        in_specs=[pl.BlockSpec((blk, d), lambda i, be, nb: (jnp.minimum(i, nb[0] - 1), 0)),
                  pl.BlockSpec((1, d, f2), lambda i, be, nb: (be[i], 0, 0)),
                  pl.BlockSpec((1, f2 // 2, d), lambda i, be, nb: (be[i], 0, 0))],
        out_specs=pl.BlockSpec((blk, d), lambda i, be, nb: (i, 0)))
    return pl.pallas_call(
        _experts_body, grid_spec=grid_spec, out_shape=jax.ShapeDtypeStruct((cap, d), F32),
        compiler_params=_params("arbitrary"), name="moe_experts",
    )(block_expert, n_live, xs, w_gate_up, w_down)


def _combine_body(dest_ref, x_ref, gate_ref, ys_ref, *rest, n_proj, final_norm):
    norm_refs = rest[:n_proj + (1 if final_norm else 0)]
    w_refs = rest[len(norm_refs):len(norm_refs) + n_proj]
    outs = rest[len(norm_refs) + n_proj:-3]
    buf1, buf2, sem = rest[-3:]
    tm = x_ref.shape[0]

    def issue(r, c):
        _row_copy(ys_ref, dest_ref[0, 0, r], buf1, r, sem).start()
        _row_copy(ys_ref, dest_ref[0, 0, tm + r], buf2, r, sem).start()
        return c

    lax.fori_loop(0, tm, issue, 0, unroll=DMA_UNROLL)
    for buf in (buf1, buf2):
        pltpu.make_async_copy(ys_ref.at[pl.ds(0, tm)], buf, sem).wait()
    gate = gate_ref[...]
    x = x_ref[...] + gate[:, 0:1] * buf1[...] + gate[:, 1:2] * buf2[...]
    if final_norm:
        outs[0][...] = _rms(x, norm_refs[0][...])
        return
    outs[0][...] = x
    for j in range(n_proj):
        outs[1 + j][...] = _dot(_rms(x, norm_refs[j][...]).astype(BF16), w_refs[j][...]).astype(outs[1 + j].dtype)


def _combine(x, ys, dest, gates, norms, weights, final_norm=False):
    t, d = x.shape
    tm = ROW_TILE
    row = pl.BlockSpec((tm, d), lambda i: (i, 0))
    in_specs = [pl.BlockSpec((1, 1, 2 * tm), lambda i: (i, 0, 0), memory_space=pltpu.SMEM),
                row, pl.BlockSpec((tm, LANES), lambda i: (i, 0)), pl.BlockSpec(memory_space=pl.ANY)]
    in_specs += [_full((1, d))] * len(norms) + [_full(w.shape) for w in weights]
    out_specs = [row] + [pl.BlockSpec((tm, w.shape[1]), lambda i: (i, 0)) for w in weights]
    out_shape = [jax.ShapeDtypeStruct((t, d), F32)]
    out_shape += [jax.ShapeDtypeStruct((t, w.shape[1]), BF16) for w in weights]
    return pl.pallas_call(
        functools.partial(_combine_body, n_proj=len(weights), final_norm=final_norm),
        grid=(t // tm,), in_specs=in_specs, out_specs=out_specs, out_shape=out_shape,
        scratch_shapes=[pltpu.VMEM((tm, d), F32), pltpu.VMEM((tm, d), F32), pltpu.SemaphoreType.DMA],
        compiler_params=_params("arbitrary"), name="moe_combine",
    )(dest, x, gates, ys, *[g.reshape(1, d) for g in norms], *[w.astype(BF16) for w in weights])


def _moe(x, xn, logits, w_gate, w_up, w_down, norms, weights, final_norm=False):
    t, d = x.shape
    blk = MOE_BLOCK
    tm = ROW_TILE
    idx, gates, counts = _route(logits)
    counts = counts[0, N_GROUPS:N_GROUPS + N_EXPERTS].astype(I32)
    padded = (counts + blk - 1) // blk * blk
    pend = jnp.cumsum(padded)
    pstart = pend - padded
    cap = 2 * t + N_EXPERTS * blk
    n_blocks = cap // blk
    starts = jnp.arange(n_blocks, dtype=I32) * blk
    block_expert = jnp.minimum(jnp.sum(pend[None, :] <= starts[:, None], axis=1), N_EXPERTS - 1).astype(I32)
    n_live = (pend[-1:] // blk).astype(I32)
    chosen = idx[:, 0:2, None] == jnp.arange(N_EXPERTS, dtype=I32)
    dest = jnp.sum(jnp.where(chosen, pstart, 0), axis=-1) + idx[:, 2:4]
    dest = dest.reshape(t // tm, tm, 2).transpose(0, 2, 1).reshape(t // tm, 1, 2 * tm)
    xs = _dispatch(xn, dest, pend.astype(I32), padded.astype(I32), cap)
    w_gu = jnp.concatenate([w_gate, w_up], axis=2).astype(BF16)
    ys = _experts(xs, block_expert, n_live, w_gu, w_down.astype(BF16))
    return _combine(x, ys, dest, gates, norms, weights, final_norm)


def _sb_masks():
    row = lax.broadcasted_iota(I32, (SB_BLOCK, SB_WINDOW), 0)
    col = lax.broadcasted_iota(I32, (SB_BLOCK, SB_WINDOW), 1)
    visible = jnp.stack([col < row, col < row + SB_BLOCK, col >= 0, col < SB_BLOCK])
    return jnp.where(visible, 0.0, SB_HIDDEN).astype(F32)


def _sb_attn_body(q_ref, k_ref, v_ref, tri_ref, bias_ref, o_ref):
    i = pl.program_id(2)
    blk, win = SB_BLOCK, SB_WINDOW
    nh = q_ref.shape[1] // HEAD
    neg_suffix = tri_ref[...]
    lanes = [slice(h * HEAD, (h + 1) * HEAD) for h in range(nh)]
    qs = [(q_ref[:, sl].astype(F32) * (HEAD ** -0.5 * LOG2E)).astype(BF16) for sl in lanes]
    u32 = jnp.uint32

    def more(carry):
        hi, go = carry[0], carry[1]
        return jnp.logical_and(hi > 0, go)

    def window(carry):
        hi, _, accs, rests = carry
        lo = jnp.maximum(hi - win, 0)
        rows = pl.ds(pl.multiple_of(lo, blk), win)
        first = hi == (i + 1) * blk
        which = jnp.where(first, jnp.where(i == 0, 0, 1), jnp.where(hi < win, 3, 2))
        bias = bias_ref[which]
        zs, parts = [], []
        for h in range(nh):
            z = _dot_nt(qs[h], k_ref[rows, lanes[h]]) + bias
            neg_abs = lax.bitcast_convert_type(lax.bitcast_convert_type(z, u32) | u32(0x80000000), F32)
            sp = jnp.maximum(z, 0.0) + jnp.log(1.0 + jnp.exp2(neg_abs)) * LOG2E
            top = lax.bitcast_convert_type(lax.bitcast_convert_type(sp, u32) & u32(0xFFFF0000), F32)
            zs.append(z)
            parts += [top.astype(BF16), (sp - top).astype(BF16)]
        sums = _dot(jnp.concatenate(parts, axis=0), neg_suffix)
        new_accs, new_rests = [], []
        for h in range(nh):
            sfx = sums[2 * h * blk:(2 * h + 1) * blk] + sums[(2 * h + 1) * blk:(2 * h + 2) * blk]
            attn = jnp.exp2(zs[h] + sfx + rests[h])
            new_accs.append(accs[h] + _dot(attn.astype(BF16), v_ref[rows, lanes[h]]))
            new_rests.append(rests[h] + sfx[:, 0:1])
        worst = functools.reduce(jnp.maximum, new_rests)
        return lo, jnp.max(worst) > SB_ZERO_LOG2, tuple(new_accs), tuple(new_rests)

    init = ((i + 1) * blk, True, tuple(jnp.zeros((blk, HEAD), F32) for _ in range(nh)),
            tuple(jnp.zeros((blk, 1), F32) for _ in range(nh)))
    accs = lax.while_loop(more, window, window(init))[2]
    o_ref[...] = jnp.concatenate(accs, axis=1).astype(o_ref.dtype)


def _sb_attn(q, kv, batch, seq):
    t, d = q.shape
    nq = seq // SB_BLOCK
    w = SB_LANES
    hp = d // w
    win = SB_WINDOW
    assert d % w == 0 and seq % SB_BLOCK == 0 and seq >= win and win == 2 * SB_BLOCK
    tri = -jnp.tril(jnp.ones((win, win), F32)).astype(BF16)
    return pl.pallas_call(
        _sb_attn_body, grid=(batch, hp, nq),
        in_specs=[pl.BlockSpec((SB_BLOCK, w), lambda b, h, i: (b * nq + i, h)),
                  pl.BlockSpec((seq, w), lambda b, h, i: (b, h)),
                  pl.BlockSpec((seq, w), lambda b, h, i: (b, hp + h)),
                  pl.BlockSpec((win, win), lambda b, h, i: (0, 0)),
                  pl.BlockSpec((4, SB_BLOCK, win), lambda b, h, i: (0, 0, 0))],
        out_specs=pl.BlockSpec((SB_BLOCK, w), lambda b, h, i: (b * nq + i, h)),
        out_shape=jax.ShapeDtypeStruct((t, d), BF16),
        compiler_params=_params("parallel", "parallel", "arbitrary"), name="sb_attn",
    )(q, kv, kv, tri, _sb_masks())


def _router_weights(w_group, b_group, w_expert, b_expert):
    d = w_group.shape[0]
    pad = LANES - N_GROUPS - N_EXPERTS
    w = jnp.concatenate([w_group, w_expert, jnp.zeros((d, pad), F32)], axis=1)
    b = jnp.concatenate([b_group, b_expert, jnp.zeros((pad,), F32)]).reshape(1, LANES)
    return w, b


def kernel(x, mem, norm_mix, norm_mem, norm_ffn, norm_final, rwkv_mu, rwkv_w_rkv, rwkv_w0, rwkv_w1, rwkv_w2, rwkv_a0, rwkv_a1, rwkv_a2, rwkv_g1, rwkv_g2, rwkv_k_k, rwkv_k_a, rwkv_r_k, rwkv_ln_g, rwkv_ln_b, rwkv_w_o, kv_norm, w_kv_shared, sb_w_q, sb_w_o, mem_norm_kv, mem_w_q, mem_w_kv, mem_w_o, moe_w_group, moe_b_group, moe_w_expert, moe_b_expert, moe_w_gate, moe_w_up, moe_w_down):
    batch, seq, d = x.shape
    depth = norm_mix.shape[0]
    n_a = rwkv_mu.shape[0]
    mem_tokens = mem.shape[1]
    xs = x.reshape(batch * seq, d)
    memf = mem.reshape(batch * mem_tokens, d)
    q = kv = None
    for l in range(depth):
        if l < n_a:
            r, lw, k, v, kk, asg, g = _rwkv_proj(
                xs, seq, norm_mix[l], rwkv_mu[l], rwkv_w_rkv[l], rwkv_w0[l], rwkv_w1[l], rwkv_w2[l],
                rwkv_a0[l], rwkv_a1[l], rwkv_a2[l], rwkv_g1[l], rwkv_g2[l], rwkv_k_k[l], rwkv_k_a[l])
            mix = _wide_scan(r, lw, k, v, kk, asg, rwkv_r_k[l], rwkv_ln_g[l], rwkv_ln_b[l], batch, seq)
            gate, w_mix = g, rwkv_w_o[l]
        else:
            mix = _sb_attn(q, kv, batch, seq)
            gate, w_mix = None, sb_w_o[l - n_a]
        mkv = _norm_linear(memf, mem_norm_kv[l], mem_w_kv[l], BF16)
        w_router, b_router = _router_weights(moe_w_group[l], moe_b_group[l], moe_w_expert[l], moe_b_expert[l])
        xs, xn, logits = _mem_attn(xs, mix, gate, w_mix, mkv, seq, mem_tokens, norm_mem[l], mem_w_q[l], mem_w_o[l],
                                   norm_ffn[l], w_router, b_router)
        last = l == depth - 1
        if last:
            norms, weights = [norm_final], []
        elif l + 1 < n_a:
            norms, weights = [], []
        else:
            norms, weights = [norm_mix[l + 1]], [sb_w_q[l + 1 - n_a]]
            if l == n_a - 1:
                norms, weights = norms + [kv_norm], weights + [w_kv_shared]
        res = _moe(xs, xn, logits, moe_w_gate[l], moe_w_up[l], moe_w_down[l], norms, weights, final_norm=last)
        if last:
            return res[0].reshape(batch, seq, d)
        xs = res[0]
        if weights:
            q = res[1]
            if l == n_a - 1:
                kv = res[2]
```

```python
import functools

import jax
import jax.numpy as jnp
from jax import lax
from jax.experimental import pallas as pl
from jax.experimental.pallas import tpu as pltpu

F32 = jnp.float32
BF16 = jnp.bfloat16
I32 = jnp.int32

HEAD = 64
MEM_HEADS = 4
N_GROUPS = 4
EXPERTS_PER_GROUP = 8
N_EXPERTS = N_GROUPS * EXPERTS_PER_GROUP
NORM_EPS = 1e-6
RWKV_LN_EPS = 64e-5

LANES = 128
CHUNK = 64
SB_BLOCK = 128
SB_WINDOW = 256
SB_LANES = 1024
SB_ZERO_LOG2 = -160.0
SB_HIDDEN = -1e30
LOG2E = 1.4426950408889634
MOE_BLOCK = 512
DMA_UNROLL = 8
ROW_TILE = 512
PROJ_TILE = 512
SCAN_ROWS = 512
SCAN_LANES = 256
SCAN_GROUP = 8
VMEM_LIMIT = 56 * 1024 * 1024


def _params(*sem):
    return pltpu.CompilerParams(dimension_semantics=sem, vmem_limit_bytes=VMEM_LIMIT)


def _dot(a, b):
    return jnp.dot(a, b, preferred_element_type=F32)


def _dot_nt(a, b):
    return lax.dot_general(a, b, (((1,), (1,)), ((), ())), preferred_element_type=F32)


def _dot_tn(a, b):
    return lax.dot_general(a, b, (((0,), (0,)), ((), ())), preferred_element_type=F32)


def _rms(x, g):
    ms = jnp.mean(x * x, axis=-1, keepdims=True)
    return x * lax.rsqrt(ms + NORM_EPS) * g


def _sigmoid(x):
    return 1.0 / (1.0 + jnp.exp(-x))


def _softplus(x):
    return jnp.maximum(x, 0.0) + jnp.log(1.0 + jnp.exp(-jnp.abs(x)))


def _full(shape):
    return pl.BlockSpec(shape, lambda *_: (0,) * len(shape))


def _rwkv_proj_body(x_ref, xp_ref, gn_ref, mu_ref, wr_ref, wk_ref, wv_ref, w1_ref, a1_ref, g1_ref,
                    w2_ref, a2_ref, g2_ref, vec_ref,
                    r_o, lw_o, k_o, v_o, kk_o, as_o, g_o, *, tiles_per_seq):
    i = pl.program_id(0)
    gn = gn_ref[...]
    h = _rms(x_ref[...], gn)
    hp = _rms(xp_ref[...], gn)[7:8, :]
    hp = jnp.where(lax.rem(i, tiles_per_seq) == 0, 0.0, hp)
    row = lax.broadcasted_iota(I32, (h.shape[0], 1), 0)
    hprev = jnp.where(row == 0, hp, pltpu.roll(h, 1, 0))
    xx = hprev - h
    mu = mu_ref[...]

    def lerp(j):
        return (h + xx * mu[j:j + 1, :]).astype(BF16)

    vec = vec_ref[...]
    r = _dot(lerp(0), wr_ref[...])
    k = _dot(lerp(1), wk_ref[...])
    v = _dot(lerp(2), wv_ref[...])
    w = vec[0:1] + _dot(jnp.tanh(_dot(lerp(3), w1_ref[...])).astype(BF16), w2_ref[...])
    w = -_softplus(-w) - 0.5
    a = _sigmoid(vec[1:2] + _dot(_dot(lerp(4), a1_ref[...]).astype(BF16), a2_ref[...]))
    g = _dot(_sigmoid(_dot(lerp(5), g1_ref[...])).astype(BF16), g2_ref[...])
    r_o[...] = r.astype(BF16)
    lw_o[...] = -jnp.exp(w)
    k_o[...] = (k * (1.0 + (a - 1.0) * vec[3:4])).astype(BF16)
    v_o[...] = v.astype(BF16)
    kk_o[...] = (k * vec[2:3]).astype(BF16)
    as_o[...] = a.astype(BF16)
    g_o[...] = g.astype(BF16)


def _rwkv_proj(x, seq, gn, mu, w_rkv, w0, w1, w2, a0, a1, a2, g1, g2, k_k, k_a):
    t, d = x.shape
    tm = PROJ_TILE
    pad8 = lambda rows: jnp.concatenate([rows, jnp.zeros((8 - rows.shape[0], d), F32)], axis=0)
    vec = pad8(jnp.stack([w0, a0, k_k, k_a]))
    mu8 = pad8(mu)
    wts = [w_rkv[0], w_rkv[1], w_rkv[2], w1, a1, g1, w2, a2, g2]
    wts = [w.astype(BF16) for w in wts]
    row_spec = pl.BlockSpec((tm, d), lambda i: (i, 0))
    prev_spec = pl.BlockSpec((8, d), lambda i: (jnp.maximum(i * (tm // 8) - 1, 0), 0))
    in_specs = [row_spec, prev_spec, _full((1, d)), _full((8, d))]
    in_specs += [_full(w.shape) for w in wts] + [_full((8, d))]
    outs = [jax.ShapeDtypeStruct((t, d), dt) for dt in (BF16, F32, BF16, BF16, BF16, BF16, BF16)]
    return pl.pallas_call(
        functools.partial(_rwkv_proj_body, tiles_per_seq=seq // tm),
        grid=(t // tm,), in_specs=in_specs, out_specs=[row_spec] * 7, out_shape=outs,
        compiler_params=_params("parallel"), name="rwkv_proj",
    )(x, x, gn.reshape(1, d), mu8, *wts, vec)


def _cumsum_rows(x):
    n = x.shape[0]
    row = lax.broadcasted_iota(I32, (n, 1), 0)
    s = 1
    while s < n:
        x = x + jnp.where(row >= s, pltpu.roll(x, s, 0), 0.0)
        s *= 2
    return x


def _split_bf16(x):
    top = lax.bitcast_convert_type(lax.bitcast_convert_type(x, jnp.uint32) & jnp.uint32(0xFFFF0000), F32)
    return top.astype(BF16), (x - top).astype(BF16)


def _wide_scan_body(r_ref, lw_ref, k_ref, v_ref, kk_ref, as_ref, rk_ref, lg_ref, lb_ref, y_ref,
                    state, o0_s, n_s, gc_s, g_s):
    c = CHUNK
    n = HEAD
    w = r_ref.shape[1]
    nh = w // n
    n_chunks = r_ref.shape[0] // c

    @pl.when(pl.program_id(2) == 0)
    def _():
        state[...] = jnp.zeros_like(state)

    row = lax.broadcasted_iota(I32, (c, w), 0)
    col = lax.broadcasted_iota(I32, (c, w), 1) & (n - 1)
    strict = row > col
    incl = row >= col
    eye = (row == col).astype(F32)
    same_head = (lax.broadcasted_iota(I32, (w, w), 0) // n) == (lax.broadcasted_iota(I32, (w, w), 1) // n)
    head_ones = jnp.where(same_head, 1.0, 0.0).astype(BF16)
    eye_c = (lax.broadcasted_iota(I32, (c, c), 0) == lax.broadcasted_iota(I32, (c, c), 1)).astype(BF16)

    def bd(x):
        return jnp.concatenate([x] * nh, axis=0) * head_ones

    def head_sums(*xs):
        parts = [p for x in xs for p in _split_bf16(x)]
        sums = _dot(jnp.concatenate(parts, axis=0), head_ones)
        return [sums[2 * j * c:(2 * j + 1) * c] + sums[(2 * j + 1) * c:(2 * j + 2) * c] for j in range(len(xs))]

    def local(cj, carry):
        cis, rowss, a_alls, r_alls, lhss, rhss, g_lasts = [], [], [], [], [], [], []
        for sub in range(SCAN_GROUP):
            ci = cj * SCAN_GROUP + sub
            rows = pl.ds(pl.multiple_of(ci * c, c), c)
            r = r_ref[rows, :].astype(F32)
            lw = lw_ref[rows, :]
            k = k_ref[rows, :].astype(F32)
            kkr = kk_ref[rows, :].astype(F32)
            asg = as_ref[rows, :].astype(F32)
            cum = _cumsum_rows(lw)
            g_in = jnp.exp(cum)
            g_ex = jnp.exp(cum - lw)
            g_inv = jnp.exp(-cum)
            g_s[ci] = g_in[c - 1:c, :]
            g_lasts.append(g_in[c - 1:c, :])
            kkn = kkr * lax.rsqrt(jnp.maximum(head_sums(kkr * kkr)[0], 1e-24))
            a_all = (-kkn * g_ex).astype(BF16)
            b_all = (kkn * asg * g_inv).astype(BF16)
            k_all = (k * g_inv).astype(BF16)
            r_all = (r * g_in).astype(BF16)
            cis.append(ci)
            rowss.append(rows)
            a_alls.append(a_all)
            r_alls.append(r_all)
            lhss.append(jnp.concatenate([a_all, r_all], axis=0))
            rhss.append((b_all, k_all))
        pbs = [_dot_nt(lhs, bd(b)) for lhs, (b, _) in zip(lhss, rhss)]
        pks = [_dot_nt(lhs, bd(k)) for lhs, (_, k) in zip(lhss, rhss)]
        lps = [jnp.where(strict, pb[:c], 0.0) for pb in pbs]
        xos = [_dot(jnp.concatenate([jnp.where(strict, pk[:c], 0.0), jnp.where(incl, pk[c:], 0.0)],
                                    axis=0).astype(BF16), bd(v_ref[rows, :])) for pk, rows in zip(pks, rowss)]
        arbs = [jnp.where(incl, pb[c:], 0.0).astype(BF16) for pb in pbs]
        tinvs = [eye + lp for lp in lps]
        lpbs = [lp.astype(BF16) for lp in lps]
        lpbs = [_dot(lpb, bd(lpb)).astype(BF16) for lpb in lpbs]
        s = 4
        while s < c:
            both = [_dot(jnp.concatenate([t.astype(BF16), lpb], axis=0), bd(lpb)) for t, lpb in zip(tinvs, lpbs)]
            tinvs = [t + bo[:c] for t, bo in zip(tinvs, both)]
            lpbs = [bo[c:].astype(BF16) for bo in both]
            s *= 2
        tinvs = [t + _dot(t.astype(BF16), bd(lpb)) for t, lpb in zip(tinvs, lpbs)]
        tbs = [t.astype(BF16) for t in tinvs]
        wbs = [_dot(tb, bd(a)).astype(BF16) for tb, a in zip(tbs, a_alls)]
        u0bs = [_dot(tb, bd(xo[:c].astype(BF16))).astype(BF16) for tb, xo in zip(tbs, xos)]
        qs = [r_all.astype(F32) + _dot(arb, bd(wb)) for r_all, arb, wb in zip(r_alls, arbs, wbs)]
        o0s = [xo[c:] + _dot(arb, bd(u0b)) for xo, arb, u0b in zip(xos, arbs, u0bs)]
        ncs = [jnp.where(same_head, _dot_tn(wb, b), 0.0) * g for wb, (b, _), g in zip(wbs, rhss, g_lasts)]
        gcs = [jnp.where(same_head, _dot_tn(jnp.concatenate([u0b, v_ref[rows, :]], axis=0),
                                            jnp.concatenate([b, k], axis=0)), 0.0) * g
               for u0b, rows, (b, k), g in zip(u0bs, rowss, rhss, g_lasts)]
        qts = [_dot_tn(q.astype(BF16), eye_c) for q in qs]
        for ci, qt, o0, nc, gc in zip(cis, qts, o0s, ncs, gcs):
            o0_s[ci] = o0
            n_s[ci] = jnp.concatenate([nc, qt, jnp.zeros((w, LANES - c), F32)], axis=1).astype(BF16)
            gc_s[ci] = gc
        return carry

    lax.fori_loop(0, n_chunks // SCAN_GROUP, local, 0)

    rk = rk_ref[...]
    lng = lg_ref[...]
    lnb = lb_ref[...]

    def sweep(ci, carry):
        s = state[...]
        both = _dot(s.astype(BF16), n_s[ci])
        state[...] = s * g_s[ci] + both[:, :w] + gc_s[ci]
        o0_s[ci] = jnp.transpose(both[:, w:])[:c] + o0_s[ci]
        return carry

    lax.fori_loop(0, n_chunks, sweep, 0)

    rowss = [pl.ds(ci * c, c) for ci in range(n_chunks)]
    os_ = [o0_s[ci] for ci in range(n_chunks)]
    sums = [head_sums(o, r_ref[rows, :].astype(F32) * k_ref[rows, :].astype(F32) * rk) for o, rows in zip(os_, rowss)]
    devs = [o - o_sum * (1.0 / n) for o, (o_sum, _) in zip(os_, sums)]
    vars_ = [head_sums(dev * dev)[0] * (1.0 / n) for dev in devs]
    for rows, dev, var, (_, bonus) in zip(rowss, devs, vars_, sums):
        y = dev * lax.rsqrt(var + RWKV_LN_EPS) * lng + lnb
        y_ref[rows, :] = (y + bonus * v_ref[rows, :].astype(F32)).astype(y_ref.dtype)


def _wide_scan(r, lw, k, v, kk, asg, r_k, ln_g, ln_b, batch, seq):
    t, d = r.shape
    rb = SCAN_ROWS
    w = SCAN_LANES
    nc = seq // rb
    n_chunks = rb // CHUNK
    assert d % w == 0 and seq % rb == 0 and n_chunks % SCAN_GROUP == 0
    blk = pl.BlockSpec((rb, w), lambda b, hp, ci: (b * nc + ci, hp))
    vec = pl.BlockSpec((1, w), lambda b, hp, ci: (0, hp))
    per_chunk = lambda rows, cols, dt: pltpu.VMEM((n_chunks, rows, cols), dt)
    scratch = [pltpu.VMEM((w, w), F32),
               per_chunk(CHUNK, w, F32),
               per_chunk(w, w + LANES, BF16),
               per_chunk(w, w, F32),
               per_chunk(1, w, F32)]
    return pl.pallas_call(
        _wide_scan_body,
        grid=(batch, d // w, nc),
        in_specs=[blk] * 6 + [vec] * 3, out_specs=blk,
        out_shape=jax.ShapeDtypeStruct((t, d), BF16),
        scratch_shapes=scratch,
        compiler_params=_params("parallel", "parallel", "arbitrary"), name="rwkv_scan",
    )(r, lw, k, v, kk, asg, r_k.reshape(1, d), ln_g.reshape(1, d), ln_b.reshape(1, d))


def _norm_linear_body(x_ref, g_ref, w_ref, o_ref):
    o_ref[...] = _dot(_rms(x_ref[...], g_ref[...]).astype(BF16), w_ref[...]).astype(o_ref.dtype)


def _norm_linear(x, g, w, out_dtype):
    t, d = x.shape
    tm = ROW_TILE
    return pl.pallas_call(
        _norm_linear_body, grid=(t // tm,),
        in_specs=[pl.BlockSpec((tm, d), lambda i: (i, 0)), _full((1, d)), _full(w.shape)],
        out_specs=pl.BlockSpec((tm, w.shape[1]), lambda i: (i, 0)),
        out_shape=jax.ShapeDtypeStruct((t, w.shape[1]), out_dtype),
        compiler_params=_params("parallel"), name="norm_linear",
    )(x, g.reshape(1, d), w.astype(BF16))


def _mem_attn_body(x_ref, a_ref, *rest, gated):
    if gated:
        mix = a_ref[...] * rest[0][...]
        rest = rest[1:]
    else:
        mix = a_ref[...]
    (wm_ref, kv_ref, gn_ref, wq_ref, wo_ref, gf_ref, wrh_ref, wrl_ref, br_ref,
     x_o, xn_o, idx_o, gate_o, cnt_o, carry) = rest
    x = x_ref[...] + _dot(mix, wm_ref[...])
    d = x.shape[1]
    hd = d // MEM_HEADS
    q = _dot(_rms(x, gn_ref[...]).astype(BF16), wq_ref[...])
    outs = []
    for hh in range(MEM_HEADS):
        qh = q[:, hh * hd:(hh + 1) * hd].astype(BF16)
        kh = kv_ref[:, hh * hd:(hh + 1) * hd]
        vh = kv_ref[:, d + hh * hd:d + (hh + 1) * hd]
        s = _dot_nt(qh, kh)
        p = jnp.exp(s - jnp.max(s, axis=-1, keepdims=True))
        o = _dot(p.astype(BF16), vh) / jnp.sum(p, axis=-1, keepdims=True)
        outs.append(o.astype(BF16))
    xnew = x + _dot(jnp.concatenate(outs, axis=1), wo_ref[...])
    x_o[...] = xnew
    xn = _rms(xnew, gf_ref[...])
    xn_o[...] = xn
    xh = xn.astype(BF16)
    xl = (xn - xh.astype(F32)).astype(BF16)
    tm = x.shape[0]
    hw = _dot(jnp.concatenate([xh, xl], axis=0), wrh_ref[...])
    logits = hw[:tm] + hw[tm:] + _dot(xh, wrl_ref[...]) + br_ref[...]
    _route_tile(logits, idx_o, gate_o, cnt_o, carry)


def _mem_attn(x, mix, gate, w_mix, kv, seq, mem_tokens, gn, w_q, w_o, g_ffn, w_router, b_router):
    t, d = x.shape
    tm = ROW_TILE
    per_seq = seq // tm
    row = pl.BlockSpec((tm, d), lambda i: (i, 0))
    lane_row = pl.BlockSpec((tm, LANES), lambda i: (i, 0))
    w_hi = w_router.astype(BF16)
    acts = [x, mix] + ([gate] if gate is not None else [])
    return pl.pallas_call(
        functools.partial(_mem_attn_body, gated=gate is not None), grid=(t // tm,),
        in_specs=[row] * len(acts) + [_full((d, d)), pl.BlockSpec((mem_tokens, 2 * d), lambda i: (i // per_seq, 0)),
                                      _full((1, d)), _full((d, d)), _full((d, d)), _full((1, d)),
                                      _full((d, LANES)), _full((d, LANES)), _full((1, LANES))],
        out_specs=[row, row, lane_row, lane_row, _full((1, LANES))],
        out_shape=[jax.ShapeDtypeStruct((t, d), F32), jax.ShapeDtypeStruct((t, d), F32),
                   jax.ShapeDtypeStruct((t, LANES), I32), jax.ShapeDtypeStruct((t, LANES), F32),
                   jax.ShapeDtypeStruct((1, LANES), F32)],
        scratch_shapes=[pltpu.VMEM((1, LANES), F32)],
        compiler_params=_params("arbitrary"), name="mem_attn",
    )(*acts, w_mix.astype(BF16), kv, gn.reshape(1, d), (w_q * (d // MEM_HEADS) ** -0.5).astype(BF16),
      w_o.astype(BF16), g_ffn.reshape(1, d), w_hi, (w_router - w_hi.astype(F32)).astype(BF16), b_router)


def _route_tile(lg, idx_o, gate_o, cnt_o, carry):
    @pl.when(pl.program_id(0) == 0)
    def _():
        carry[...] = jnp.zeros_like(carry)

    tm = lg.shape[0]
    lane = lax.broadcasted_iota(I32, lg.shape, 1)
    lanef = lane.astype(F32)
    big = 1e9
    ninf = -jnp.inf
    gl = jnp.where(lane < N_GROUPS, lg, ninf)
    gmax = jnp.max(gl, axis=-1, keepdims=True)
    g_w = 1.0 / jnp.sum(jnp.exp(gl - gmax), axis=-1, keepdims=True)
    grp = jnp.min(jnp.where(gl == gmax, lanef, big), axis=-1, keepdims=True).astype(I32)
    in_grp = (lane >= N_GROUPS) & (lane < N_GROUPS + N_EXPERTS) & (((lane - N_GROUPS) >> 3) == grp)
    el = jnp.where(in_grp, lg, ninf)
    m1 = jnp.max(el, axis=-1, keepdims=True)
    i1 = jnp.min(jnp.where(el == m1, lanef, big), axis=-1, keepdims=True)
    el2 = jnp.where(lanef == i1, ninf, el)
    m2 = jnp.max(el2, axis=-1, keepdims=True)
    i2 = jnp.min(jnp.where(el2 == m2, lanef, big), axis=-1, keepdims=True)
    tt = jnp.exp(m2 - m1)
    w1 = 1.0 / (1.0 + tt)
    w2 = tt / (1.0 + tt)
    hit1 = lanef == i1
    hit2 = lanef == i2
    oh = jnp.where(hit1 | hit2, 1.0, 0.0)
    r_i = lax.broadcasted_iota(I32, (tm, tm), 0)
    c_i = lax.broadcasted_iota(I32, (tm, tm), 1)
    tri = jnp.where(r_i > c_i, 1.0, 0.0).astype(BF16)
    pos = _dot(tri, oh.astype(BF16)) + carry[...]
    pos1 = jnp.sum(jnp.where(hit1, pos, 0.0), axis=-1, keepdims=True)
    pos2 = jnp.sum(jnp.where(hit2, pos, 0.0), axis=-1, keepdims=True)
    carry[...] = carry[...] + jnp.sum(oh, axis=0, keepdims=True)
    cnt_o[...] = carry[...]
    idx = jnp.where(lane == 0, i1 - N_GROUPS, jnp.where(lane == 1, i2 - N_GROUPS,
          jnp.where(lane == 2, pos1, jnp.where(lane == 3, pos2, 0.0))))
    idx_o[...] = idx.astype(I32)
    gate_o[...] = jnp.where(lane == 0, g_w * w1, jnp.where(lane == 1, g_w * w2, 0.0))


def _row_copy(src, s, dst, d, sem):
    return pltpu.make_async_copy(src.at[pl.ds(s, 1)], dst.at[pl.ds(d, 1)], sem)


def _dispatch_body(pend_ref, pad_ref, dest_ref, xn_ref, xs_out, zero_buf, sem):
    tm = xn_ref.shape[0]
    blk = zero_buf.shape[0]

    @pl.when(pl.program_id(0) == 0)
    def _():
        zero_buf[...] = jnp.zeros_like(zero_buf)

        def clear_block(start):
            cp = pltpu.make_async_copy(zero_buf, xs_out.at[pl.ds(pl.multiple_of(start, blk), blk)], sem)
            cp.start()
            cp.wait()

        def clear(e, c):
            @pl.when(pad_ref[e] > 0)
            def _():
                clear_block(pend_ref[e] - blk)
            return c

        def clear_tail(j, c):
            clear_block(j * blk)
            return c

        lax.fori_loop(0, N_EXPERTS, clear, 0)
        lax.fori_loop(pend_ref[N_EXPERTS - 1] // blk, xs_out.shape[0] // blk, clear_tail, 0)

    def issue(r, c):
        _row_copy(xn_ref, r, xs_out, dest_ref[0, 0, r], sem).start()
        _row_copy(xn_ref, r, xs_out, dest_ref[0, 0, tm + r], sem).start()
        return c

    lax.fori_loop(0, tm, issue, 0, unroll=DMA_UNROLL)
    for _ in range(2):
        pltpu.make_async_copy(xn_ref, xs_out.at[pl.ds(0, tm)], sem).wait()


def _dispatch(xn, dest, pend, padded, cap):
    t, d = xn.shape
    tm = ROW_TILE
    grid_spec = pltpu.PrefetchScalarGridSpec(
        num_scalar_prefetch=2, grid=(t // tm,),
        in_specs=[pl.BlockSpec((1, 1, 2 * tm), lambda i, pe, pa: (i, 0, 0), memory_space=pltpu.SMEM),
                  pl.BlockSpec((tm, d), lambda i, pe, pa: (i, 0))],
        out_specs=pl.BlockSpec(memory_space=pl.ANY),
        scratch_shapes=[pltpu.VMEM((MOE_BLOCK, d), F32), pltpu.SemaphoreType.DMA])
    return pl.pallas_call(
        _dispatch_body, grid_spec=grid_spec, out_shape=jax.ShapeDtypeStruct((cap, d), F32),
        compiler_params=_params("arbitrary"), name="moe_dispatch",
    )(pend, padded, dest, xn)


def _experts_body(be_ref, nb_ref, xs_ref, wgu_ref, wd_ref, ys_ref):
    del be_ref
    live = pl.program_id(0) < nb_ref[0]

    @pl.when(live)
    def _():
        gu = _dot(xs_ref[...].astype(BF16), wgu_ref[0])
        f = gu.shape[1] // 2
        gt = gu[:, :f]
        hid = gt * _sigmoid(gt) * gu[:, f:]
        ys_ref[...] = _dot(hid.astype(BF16), wd_ref[0])

    @pl.when(jnp.logical_not(live))
    def _():
        ys_ref[...] = jnp.zeros_like(ys_ref)


def _experts(xs, block_expert, n_live, w_gate_up, w_down):
    cap, d = xs.shape
    f2 = w_gate_up.shape[2]
    blk = MOE_BLOCK
    grid_spec = pltpu.PrefetchScalarGridSpec(
        num_scalar_prefetch=2, grid=(cap // blk,),
        in_specs=[pl.BlockSpec((blk, d), lambda i, be, nb: (jnp.minimum(i, nb[0] - 1), 0)),
                  pl.BlockSpec((1, d, f2), lambda i, be, nb: (be[i], 0, 0)),
                  pl.BlockSpec((1, f2 // 2, d), lambda i, be, nb: (be[i], 0, 0))],
        out_specs=pl.BlockSpec((blk, d), lambda i, be, nb: (i, 0)))
    return pl.pallas_call(
        _experts_body, grid_spec=grid_spec, out_shape=jax.ShapeDtypeStruct((cap, d), F32),
        compiler_params=_params("arbitrary"), name="moe_experts",
    )(block_expert, n_live, xs, w_gate_up, w_down)


def _combine_body(dest_ref, x_ref, gate_ref, ys_ref, *rest, n_proj, final_norm):
    norm_refs = rest[:n_proj + (1 if final_norm else 0)]
    w_refs = rest[len(norm_refs):len(norm_refs) + n_proj]
    outs = rest[len(norm_refs) + n_proj:-3]
    buf1, buf2, sem = rest[-3:]
    tm = x_ref.shape[0]

    def issue(r, c):
        _row_copy(ys_ref, dest_ref[0, 0, r], buf1, r, sem).start()
        _row_copy(ys_ref, dest_ref[0, 0, tm + r], buf2, r, sem).start()
        return c

    lax.fori_loop(0, tm, issue, 0, unroll=DMA_UNROLL)
    for buf in (buf1, buf2):
        pltpu.make_async_copy(ys_ref.at[pl.ds(0, tm)], buf, sem).wait()
    gate = gate_ref[...]
    x = x_ref[...] + gate[:, 0:1] * buf1[...] + gate[:, 1:2] * buf2[...]
    if final_norm:
        outs[0][...] = _rms(x, norm_refs[0][...])
        return
    outs[0][...] = x
    for j in range(n_proj):
        outs[1 + j][...] = _dot(_rms(x, norm_refs[j][...]).astype(BF16), w_refs[j][...]).astype(outs[1 + j].dtype)


def _combine(x, ys, dest, gates, norms, weights, final_norm=False):
    t, d = x.shape
    tm = ROW_TILE
    row = pl.BlockSpec((tm, d), lambda i: (i, 0))
    in_specs = [pl.BlockSpec((1, 1, 2 * tm), lambda i: (i, 0, 0), memory_space=pltpu.SMEM),
                row, pl.BlockSpec((tm, LANES), lambda i: (i, 0)), pl.BlockSpec(memory_space=pl.ANY)]
    in_specs += [_full((1, d))] * len(norms) + [_full(w.shape) for w in weights]
    out_specs = [row] + [pl.BlockSpec((tm, w.shape[1]), lambda i: (i, 0)) for w in weights]
    out_shape = [jax.ShapeDtypeStruct((t, d), F32)]
    out_shape += [jax.ShapeDtypeStruct((t, w.shape[1]), BF16) for w in weights]
    return pl.pallas_call(
        functools.partial(_combine_body, n_proj=len(weights), final_norm=final_norm),
        grid=(t // tm,), in_specs=in_specs, out_specs=out_specs, out_shape=out_shape,
        scratch_shapes=[pltpu.VMEM((tm, d), F32), pltpu.VMEM((tm, d), F32), pltpu.SemaphoreType.DMA],
        compiler_params=_params("arbitrary"), name="moe_combine",
    )(dest, x, gates, ys, *[g.reshape(1, d) for g in norms], *[w.astype(BF16) for w in weights])


def _moe(x, xn, routing, w_gate, w_up, w_down, norms, weights, final_norm=False):
    t, d = x.shape
    blk = MOE_BLOCK
    tm = ROW_TILE
    idx, gates, counts = routing
    counts = counts[0, N_GROUPS:N_GROUPS + N_EXPERTS].astype(I32)
    padded = (counts + blk - 1) // blk * blk
    pend = jnp.cumsum(padded)
    pstart = pend - padded
    cap = 2 * t + N_EXPERTS * blk
    n_blocks = cap // blk
    starts = jnp.arange(n_blocks, dtype=I32) * blk
    block_expert = jnp.minimum(jnp.sum(pend[None, :] <= starts[:, None], axis=1), N_EXPERTS - 1).astype(I32)
    n_live = (pend[-1:] // blk).astype(I32)
    chosen = idx[:, 0:2, None] == jnp.arange(N_EXPERTS, dtype=I32)
    dest = jnp.sum(jnp.where(chosen, pstart, 0), axis=-1) + idx[:, 2:4]
    dest = dest.reshape(t // tm, tm, 2).transpose(0, 2, 1).reshape(t // tm, 1, 2 * tm)
    xs = _dispatch(xn, dest, pend.astype(I32), padded.astype(I32), cap)
    w_gu = jnp.concatenate([w_gate, w_up], axis=2).astype(BF16)
    ys = _experts(xs, block_expert, n_live, w_gu, w_down.astype(BF16))
    return _combine(x, ys, dest, gates, norms, weights, final_norm)


def _sb_masks():
    row = lax.broadcasted_iota(I32, (SB_BLOCK, SB_WINDOW), 0)
    col = lax.broadcasted_iota(I32, (SB_BLOCK, SB_WINDOW), 1)
    visible = jnp.stack([col < row, col < row + SB_BLOCK, col >= 0, col < SB_BLOCK])
    return jnp.where(visible, 0.0, SB_HIDDEN).astype(F32)


def _sb_attn_body(q_ref, k_ref, v_ref, tri_ref, bias_ref, o_ref):
    i = pl.program_id(2)
    blk, win = SB_BLOCK, SB_WINDOW
    nh = q_ref.shape[1] // HEAD
    neg_suffix = tri_ref[...]
    lanes = [slice(h * HEAD, (h + 1) * HEAD) for h in range(nh)]
    qs = [(q_ref[:, sl].astype(F32) * (HEAD ** -0.5 * LOG2E)).astype(BF16) for sl in lanes]
    u32 = jnp.uint32

    def more(carry):
        hi, go = carry[0], carry[1]
        return jnp.logical_and(hi > 0, go)

    def window(carry):
        hi, _, accs, rests = carry
        lo = jnp.maximum(hi - win, 0)
        rows = pl.ds(pl.multiple_of(lo, blk), win)
        first = hi == (i + 1) * blk
        which = jnp.where(first, jnp.where(i == 0, 0, 1), jnp.where(hi < win, 3, 2))
        bias = bias_ref[which]
        zs, parts = [], []
        for h in range(nh):
            z = _dot_nt(qs[h], k_ref[rows, lanes[h]]) + bias
            neg_abs = lax.bitcast_convert_type(lax.bitcast_convert_type(z, u32) | u32(0x80000000), F32)
            sp = jnp.maximum(z, 0.0) + jnp.log(1.0 + jnp.exp2(neg_abs)) * LOG2E
            top = lax.bitcast_convert_type(lax.bitcast_convert_type(sp, u32) & u32(0xFFFF0000), F32)
            zs.append(z)
            parts += [top.astype(BF16), (sp - top).astype(BF16)]
        sums = _dot(jnp.concatenate(parts, axis=0), neg_suffix)
        new_accs, new_rests = [], []
        for h in range(nh):
            sfx = sums[2 * h * blk:(2 * h + 1) * blk] + sums[(2 * h + 1) * blk:(2 * h + 2) * blk]
            attn = jnp.exp2(zs[h] + sfx + rests[h])
            new_accs.append(accs[h] + _dot(attn.astype(BF16), v_ref[rows, lanes[h]]))
            new_rests.append(rests[h] + sfx[:, 0:1])
        worst = functools.reduce(jnp.maximum, new_rests)
        return lo, jnp.max(worst) > SB_ZERO_LOG2, tuple(new_accs), tuple(new_rests)

    init = ((i + 1) * blk, True, tuple(jnp.zeros((blk, HEAD), F32) for _ in range(nh)),
            tuple(jnp.zeros((blk, 1), F32) for _ in range(nh)))
    accs = lax.while_loop(more, window, window(init))[2]
    o_ref[...] = jnp.concatenate(accs, axis=1).astype(o_ref.dtype)


def _sb_attn(q, kv, batch, seq):
    t, d = q.shape
    nq = seq // SB_BLOCK
    w = SB_LANES
    hp = d // w
    win = SB_WINDOW
    assert d % w == 0 and seq % SB_BLOCK == 0 and seq >= win and win == 2 * SB_BLOCK
    tri = -jnp.tril(jnp.ones((win, win), F32)).astype(BF16)
    once = pl.Buffered(1)
    return pl.pallas_call(
        _sb_attn_body, grid=(batch, hp, nq),
        in_specs=[pl.BlockSpec((SB_BLOCK, w), lambda b, h, i: (b * nq + i, h)),
                  pl.BlockSpec((seq, w), lambda b, h, i: (b, h), pipeline_mode=once),
                  pl.BlockSpec((seq, w), lambda b, h, i: (b, hp + h), pipeline_mode=once),
                  pl.BlockSpec((win, win), lambda b, h, i: (0, 0)),
                  pl.BlockSpec((4, SB_BLOCK, win), lambda b, h, i: (0, 0, 0))],
        out_specs=pl.BlockSpec((SB_BLOCK, w), lambda b, h, i: (b * nq + i, h)),
        out_shape=jax.ShapeDtypeStruct((t, d), BF16),
        compiler_params=_params("parallel", "parallel", "arbitrary"), name="sb_attn",
    )(q, kv, kv, tri, _sb_masks())


def _router_weights(w_group, b_group, w_expert, b_expert):
    d = w_group.shape[0]
    pad = LANES - N_GROUPS - N_EXPERTS
    w = jnp.concatenate([w_group, w_expert, jnp.zeros((d, pad), F32)], axis=1)
    b = jnp.concatenate([b_group, b_expert, jnp.zeros((pad,), F32)]).reshape(1, LANES)
    return w, b


def kernel(x, mem, norm_mix, norm_mem, norm_ffn, norm_final, rwkv_mu, rwkv_w_rkv, rwkv_w0, rwkv_w1, rwkv_w2, rwkv_a0, rwkv_a1, rwkv_a2, rwkv_g1, rwkv_g2, rwkv_k_k, rwkv_k_a, rwkv_r_k, rwkv_ln_g, rwkv_ln_b, rwkv_w_o, kv_norm, w_kv_shared, sb_w_q, sb_w_o, mem_norm_kv, mem_w_q, mem_w_kv, mem_w_o, moe_w_group, moe_b_group, moe_w_expert, moe_b_expert, moe_w_gate, moe_w_up, moe_w_down):
    batch, seq, d = x.shape
    depth = norm_mix.shape[0]
    n_a = rwkv_mu.shape[0]
    mem_tokens = mem.shape[1]
    xs = x.reshape(batch * seq, d)
    memf = mem.reshape(batch * mem_tokens, d)
    q = kv = None
    for l in range(depth):
        if l < n_a:
            r, lw, k, v, kk, asg, g = _rwkv_proj(
                xs, seq, norm_mix[l], rwkv_mu[l], rwkv_w_rkv[l], rwkv_w0[l], rwkv_w1[l], rwkv_w2[l],
                rwkv_a0[l], rwkv_a1[l], rwkv_a2[l], rwkv_g1[l], rwkv_g2[l], rwkv_k_k[l], rwkv_k_a[l])
            mix = _wide_scan(r, lw, k, v, kk, asg, rwkv_r_k[l], rwkv_ln_g[l], rwkv_ln_b[l], batch, seq)
            gate, w_mix = g, rwkv_w_o[l]
        else:
            mix = _sb_attn(q, kv, batch, seq)
            gate, w_mix = None, sb_w_o[l - n_a]
        mkv = _norm_linear(memf, mem_norm_kv[l], mem_w_kv[l], BF16)
        w_router, b_router = _router_weights(moe_w_group[l], moe_b_group[l], moe_w_expert[l], moe_b_expert[l])
        xs, xn, *routing = _mem_attn(xs, mix, gate, w_mix, mkv, seq, mem_tokens, norm_mem[l], mem_w_q[l],
                                     mem_w_o[l], norm_ffn[l], w_router, b_router)
        last = l == depth - 1
        if last:
            norms, weights = [norm_final], []
        elif l + 1 < n_a:
            norms, weights = [], []
        else:
            norms, weights = [norm_mix[l + 1]], [sb_w_q[l + 1 - n_a]]
            if l == n_a - 1:
                norms, weights = norms + [kv_norm], weights + [w_kv_shared]
        res = _moe(xs, xn, routing, moe_w_gate[l], moe_w_up[l], moe_w_down[l], norms, weights, final_norm=last)
        if last:
            return res[0].reshape(batch, seq, d)
        xs = res[0]
        if weights:
            q = res[1]
            if l == n_a - 1:
                kv = res[2]
```

```python
import functools

import jax
import jax.numpy as jnp
from jax import lax
from jax.experimental import pallas as pl
from jax.experimental.pallas import tpu as pltpu

F32 = jnp.float32
BF16 = jnp.bfloat16
I32 = jnp.int32

HEAD = 64
MEM_HEADS = 4
N_GROUPS = 4
EXPERTS_PER_GROUP = 8
N_EXPERTS = N_GROUPS * EXPERTS_PER_GROUP
NORM_EPS = 1e-6
RWKV_LN_EPS = 64e-5

LANES = 128
CHUNK = 64
SB_BLOCK = 128
SB_WINDOW = 256
SB_LANES = 1024
SB_ZERO_LOG2 = -160.0
SB_HIDDEN = -1e30
LOG2E = 1.4426950408889634
MOE_BLOCK = 512
DMA_UNROLL = 8
ROW_TILE = 512
PROJ_TILE = 512
SCAN_ROWS = 512
SCAN_LANES = 256
SCAN_GROUP = 8
VMEM_LIMIT = 56 * 1024 * 1024


def _params(*sem):
    return pltpu.CompilerParams(dimension_semantics=sem, vmem_limit_bytes=VMEM_LIMIT)


def _dot(a, b):
    return jnp.dot(a, b, preferred_element_type=F32)


def _dot_nt(a, b):
    return lax.dot_general(a, b, (((1,), (1,)), ((), ())), preferred_element_type=F32)


def _dot_tn(a, b):
    return lax.dot_general(a, b, (((0,), (0,)), ((), ())), preferred_element_type=F32)


def _rms(x, g):
    ms = jnp.mean(x * x, axis=-1, keepdims=True)
    return x * lax.rsqrt(ms + NORM_EPS) * g


def _sigmoid(x):
    return 1.0 / (1.0 + jnp.exp(-x))


def _softplus(x):
    return jnp.maximum(x, 0.0) + jnp.log(1.0 + jnp.exp(-jnp.abs(x)))


def _full(shape):
    return pl.BlockSpec(shape, lambda *_: (0,) * len(shape))


def _rwkv_proj_body(x_ref, xp_ref, gn_ref, mu_ref, wr_ref, wk_ref, wv_ref, w1_ref, a1_ref, g1_ref,
                    w2_ref, a2_ref, g2_ref, vec_ref,
                    r_o, lw_o, k_o, v_o, kk_o, as_o, g_o, *, tiles_per_seq):
    i = pl.program_id(0)
    gn = gn_ref[...]
    h = _rms(x_ref[...], gn)
    hp = _rms(xp_ref[...], gn)[7:8, :]
    hp = jnp.where(lax.rem(i, tiles_per_seq) == 0, 0.0, hp)
    row = lax.broadcasted_iota(I32, (h.shape[0], 1), 0)
    hprev = jnp.where(row == 0, hp, pltpu.roll(h, 1, 0))
    xx = hprev - h
    mu = mu_ref[...]

    def lerp(j):
        return (h + xx * mu[j:j + 1, :]).astype(BF16)

    vec = vec_ref[...]
    r = _dot(lerp(0), wr_ref[...])
    k = _dot(lerp(1), wk_ref[...])
    v = _dot(lerp(2), wv_ref[...])
    w = vec[0:1] + _dot(jnp.tanh(_dot(lerp(3), w1_ref[...])).astype(BF16), w2_ref[...])
    w = -_softplus(-w) - 0.5
    a = _sigmoid(vec[1:2] + _dot(_dot(lerp(4), a1_ref[...]).astype(BF16), a2_ref[...]))
    g = _dot(_sigmoid(_dot(lerp(5), g1_ref[...])).astype(BF16), g2_ref[...])
    r_o[...] = r.astype(BF16)
    lw_o[...] = -jnp.exp(w)
    k_o[...] = (k * (1.0 + (a - 1.0) * vec[3:4])).astype(BF16)
    v_o[...] = v.astype(BF16)
    kk_o[...] = (k * vec[2:3]).astype(BF16)
    as_o[...] = a.astype(BF16)
    g_o[...] = g.astype(BF16)


def _rwkv_proj(x, seq, gn, mu, w_rkv, w0, w1, w2, a0, a1, a2, g1, g2, k_k, k_a):
    t, d = x.shape
    tm = PROJ_TILE
    pad8 = lambda rows: jnp.concatenate([rows, jnp.zeros((8 - rows.shape[0], d), F32)], axis=0)
    vec = pad8(jnp.stack([w0, a0, k_k, k_a]))
    mu8 = pad8(mu)
    wts = [w_rkv[0], w_rkv[1], w_rkv[2], w1, a1, g1, w2, a2, g2]
    wts = [w.astype(BF16) for w in wts]
    row_spec = pl.BlockSpec((tm, d), lambda i: (i, 0))
    prev_spec = pl.BlockSpec((8, d), lambda i: (jnp.maximum(i * (tm // 8) - 1, 0), 0))
    in_specs = [row_spec, prev_spec, _full((1, d)), _full((8, d))]
    in_specs += [_full(w.shape) for w in wts] + [_full((8, d))]
    outs = [jax.ShapeDtypeStruct((t, d), dt) for dt in (BF16, F32, BF16, BF16, BF16, BF16, BF16)]
    return pl.pallas_call(
        functools.partial(_rwkv_proj_body, tiles_per_seq=seq // tm),
        grid=(t // tm,), in_specs=in_specs, out_specs=[row_spec] * 7, out_shape=outs,
        compiler_params=_params("parallel"), name="rwkv_proj",
    )(x, x, gn.reshape(1, d), mu8, *wts, vec)


def _cumsum_rows(x):
    n = x.shape[0]
    row = lax.broadcasted_iota(I32, (n, 1), 0)
    s = 1
    while s < n:
        x = x + jnp.where(row >= s, pltpu.roll(x, s, 0), 0.0)
        s *= 2
    return x


def _split_bf16(x):
    top = lax.bitcast_convert_type(lax.bitcast_convert_type(x, jnp.uint32) & jnp.uint32(0xFFFF0000), F32)
    return top.astype(BF16), (x - top).astype(BF16)


def _wide_scan_body(r_ref, lw_ref, k_ref, v_ref, kk_ref, as_ref, rk_ref, lg_ref, lb_ref, y_ref,
                    state, o0_s, n_s, gc_s, g_s):
    c = CHUNK
    n = HEAD
    w = r_ref.shape[1]
    nh = w // n
    n_chunks = r_ref.shape[0] // c

    @pl.when(pl.program_id(2) == 0)
    def _():
        state[...] = jnp.zeros_like(state)

    row = lax.broadcasted_iota(I32, (c, w), 0)
    col = lax.broadcasted_iota(I32, (c, w), 1) & (n - 1)
    strict = row > col
    incl = row >= col
    eye = (row == col).astype(F32)
    same_head = (lax.broadcasted_iota(I32, (w, w), 0) // n) == (lax.broadcasted_iota(I32, (w, w), 1) // n)
    head_ones = jnp.where(same_head, 1.0, 0.0).astype(BF16)
    eye_c = (lax.broadcasted_iota(I32, (c, c), 0) == lax.broadcasted_iota(I32, (c, c), 1)).astype(BF16)

    def bd(x):
        return jnp.concatenate([x] * nh, axis=0) * head_ones

    def head_sums(*xs):
        parts = [p for x in xs for p in _split_bf16(x)]
        sums = _dot(jnp.concatenate(parts, axis=0), head_ones)
        return [sums[2 * j * c:(2 * j + 1) * c] + sums[(2 * j + 1) * c:(2 * j + 2) * c] for j in range(len(xs))]

    def local(cj, carry):
        cis, rowss, a_alls, r_alls, lhss, rhss, g_lasts = [], [], [], [], [], [], []
        for sub in range(SCAN_GROUP):
            ci = cj * SCAN_GROUP + sub
            rows = pl.ds(pl.multiple_of(ci * c, c), c)
            r = r_ref[rows, :].astype(F32)
            lw = lw_ref[rows, :]
            k = k_ref[rows, :].astype(F32)
            kkr = kk_ref[rows, :].astype(F32)
            asg = as_ref[rows, :].astype(F32)
            cum = _cumsum_rows(lw)
            g_in = jnp.exp(cum)
            g_ex = jnp.exp(cum - lw)
            g_inv = jnp.exp(-cum)
            g_s[ci] = g_in[c - 1:c, :]
            g_lasts.append(g_in[c - 1:c, :])
            kkn = kkr * lax.rsqrt(jnp.maximum(head_sums(kkr * kkr)[0], 1e-24))
            a_all = (-kkn * g_ex).astype(BF16)
            b_all = (kkn * asg * g_inv).astype(BF16)
            k_all = (k * g_inv).astype(BF16)
            r_all = (r * g_in).astype(BF16)
            cis.append(ci)
            rowss.append(rows)
            a_alls.append(a_all)
            r_alls.append(r_all)
            lhss.append(jnp.concatenate([a_all, r_all], axis=0))
            rhss.append((b_all, k_all))
        pbs = [_dot_nt(lhs, bd(b)) for lhs, (b, _) in zip(lhss, rhss)]
        pks = [_dot_nt(lhs, bd(k)) for lhs, (_, k) in zip(lhss, rhss)]
        lps = [jnp.where(strict, pb[:c], 0.0) for pb in pbs]
        xos = [_dot(jnp.concatenate([jnp.where(strict, pk[:c], 0.0), jnp.where(incl, pk[c:], 0.0)],
                                    axis=0).astype(BF16), bd(v_ref[rows, :])) for pk, rows in zip(pks, rowss)]
        arbs = [jnp.where(incl, pb[c:], 0.0).astype(BF16) for pb in pbs]
        tinvs = [eye + lp for lp in lps]
        lpbs = [lp.astype(BF16) for lp in lps]
        lpbs = [_dot(lpb, bd(lpb)).astype(BF16) for lpb in lpbs]
        s = 4
        while s < c:
            both = [_dot(jnp.concatenate([t.astype(BF16), lpb], axis=0), bd(lpb)) for t, lpb in zip(tinvs, lpbs)]
            tinvs = [t + bo[:c] for t, bo in zip(tinvs, both)]
            lpbs = [bo[c:].astype(BF16) for bo in both]
            s *= 2
        tinvs = [t + _dot(t.astype(BF16), bd(lpb)) for t, lpb in zip(tinvs, lpbs)]
        tbs = [t.astype(BF16) for t in tinvs]
        wbs = [_dot(tb, bd(a)).astype(BF16) for tb, a in zip(tbs, a_alls)]
        u0bs = [_dot(tb, bd(xo[:c].astype(BF16))).astype(BF16) for tb, xo in zip(tbs, xos)]
        qs = [r_all.astype(F32) + _dot(arb, bd(wb)) for r_all, arb, wb in zip(r_alls, arbs, wbs)]
        o0s = [xo[c:] + _dot(arb, bd(u0b)) for xo, arb, u0b in zip(xos, arbs, u0bs)]
        ncs = [jnp.where(same_head, _dot_tn(wb, b), 0.0) * g for wb, (b, _), g in zip(wbs, rhss, g_lasts)]
        gcs = [jnp.where(same_head, _dot_tn(jnp.concatenate([u0b, v_ref[rows, :]], axis=0),
                                            jnp.concatenate([b, k], axis=0)), 0.0) * g
               for u0b, rows, (b, k), g in zip(u0bs, rowss, rhss, g_lasts)]
        qts = [_dot_tn(q.astype(BF16), eye_c) for q in qs]
        for ci, qt, o0, nc, gc in zip(cis, qts, o0s, ncs, gcs):
            o0_s[ci] = o0
            n_s[ci] = jnp.concatenate([nc, qt, jnp.zeros((w, LANES - c), F32)], axis=1).astype(BF16)
            gc_s[ci] = gc
        return carry

    lax.fori_loop(0, n_chunks // SCAN_GROUP, local, 0)

    rk = rk_ref[...]
    lng = lg_ref[...]
    lnb = lb_ref[...]

    def sweep(ci, carry):
        s = state[...]
        both = _dot(s.astype(BF16), n_s[ci])
        state[...] = s * g_s[ci] + both[:, :w] + gc_s[ci]
        o0_s[ci] = jnp.transpose(both[:, w:])[:c] + o0_s[ci]
        return carry

    lax.fori_loop(0, n_chunks, sweep, 0)

    rowss = [pl.ds(ci * c, c) for ci in range(n_chunks)]
    os_ = [o0_s[ci] for ci in range(n_chunks)]
    sums = [head_sums(o, r_ref[rows, :].astype(F32) * k_ref[rows, :].astype(F32) * rk) for o, rows in zip(os_, rowss)]
    devs = [o - o_sum * (1.0 / n) for o, (o_sum, _) in zip(os_, sums)]
    vars_ = [head_sums(dev * dev)[0] * (1.0 / n) for dev in devs]
    for rows, dev, var, (_, bonus) in zip(rowss, devs, vars_, sums):
        y = dev * lax.rsqrt(var + RWKV_LN_EPS) * lng + lnb
        y_ref[rows, :] = (y + bonus * v_ref[rows, :].astype(F32)).astype(y_ref.dtype)


def _wide_scan(r, lw, k, v, kk, asg, r_k, ln_g, ln_b, batch, seq):
    t, d = r.shape
    rb = SCAN_ROWS
    w = SCAN_LANES
    nc = seq // rb
    n_chunks = rb // CHUNK
    assert d % w == 0 and seq % rb == 0 and n_chunks % SCAN_GROUP == 0
    blk = pl.BlockSpec((rb, w), lambda b, hp, ci: (b * nc + ci, hp))
    vec = pl.BlockSpec((1, w), lambda b, hp, ci: (0, hp))
    per_chunk = lambda rows, cols, dt: pltpu.VMEM((n_chunks, rows, cols), dt)
    scratch = [pltpu.VMEM((w, w), F32),
               per_chunk(CHUNK, w, F32),
               per_chunk(w, w + LANES, BF16),
               per_chunk(w, w, F32),
               per_chunk(1, w, F32)]
    return pl.pallas_call(
        _wide_scan_body,
        grid=(batch, d // w, nc),
        in_specs=[blk] * 6 + [vec] * 3, out_specs=blk,
        out_shape=jax.ShapeDtypeStruct((t, d), BF16),
        scratch_shapes=scratch,
        compiler_params=_params("parallel", "parallel", "arbitrary"), name="rwkv_scan",
    )(r, lw, k, v, kk, asg, r_k.reshape(1, d), ln_g.reshape(1, d), ln_b.reshape(1, d))


def _norm_linear_body(x_ref, g_ref, w_ref, o_ref):
    o_ref[...] = _dot(_rms(x_ref[...], g_ref[...]).astype(BF16), w_ref[...]).astype(o_ref.dtype)


def _norm_linear(x, g, w, out_dtype):
    t, d = x.shape
    tm = ROW_TILE
    return pl.pallas_call(
        _norm_linear_body, grid=(t // tm,),
        in_specs=[pl.BlockSpec((tm, d), lambda i: (i, 0)), _full((1, d)), _full(w.shape)],
        out_specs=pl.BlockSpec((tm, w.shape[1]), lambda i: (i, 0)),
        out_shape=jax.ShapeDtypeStruct((t, w.shape[1]), out_dtype),
        compiler_params=_params("parallel"), name="norm_linear",
    )(x, g.reshape(1, d), w.astype(BF16))


def _mem_attn_body(x_ref, a_ref, *rest, gated):
    if gated:
        mix = a_ref[...] * rest[0][...]
        rest = rest[1:]
    else:
        mix = a_ref[...]
    (wm_ref, kv_ref, gn_ref, wq_ref, wo_ref, gf_ref, wrh_ref, wrl_ref, br_ref,
     x_o, xn_o, idx_o, gate_o, cnt_o, carry) = rest
    x = x_ref[...] + _dot(mix, wm_ref[...])
    d = x.shape[1]
    hd = d // MEM_HEADS
    q = _dot(_rms(x, gn_ref[...]).astype(BF16), wq_ref[...])
    outs = []
    for hh in range(MEM_HEADS):
        qh = q[:, hh * hd:(hh + 1) * hd].astype(BF16)
        kh = kv_ref[:, hh * hd:(hh + 1) * hd]
        vh = kv_ref[:, d + hh * hd:d + (hh + 1) * hd]
        s = _dot_nt(qh, kh)
        p = jnp.exp(s - jnp.max(s, axis=-1, keepdims=True))
        o = _dot(p.astype(BF16), vh) / jnp.sum(p, axis=-1, keepdims=True)
        outs.append(o.astype(BF16))
    xnew = x + _dot(jnp.concatenate(outs, axis=1), wo_ref[...])
    x_o[...] = xnew
    xn = _rms(xnew, gf_ref[...])
    xn_o[...] = xn
    xh = xn.astype(BF16)
    xl = (xn - xh.astype(F32)).astype(BF16)
    tm = x.shape[0]
    hw = _dot(jnp.concatenate([xh, xl], axis=0), wrh_ref[...])
    logits = hw[:tm] + hw[tm:] + _dot(xh, wrl_ref[...]) + br_ref[...]
    _route_tile(logits, idx_o, gate_o, cnt_o, carry)


def _mem_attn(x, mix, gate, w_mix, kv, seq, mem_tokens, gn, w_q, w_o, g_ffn, w_router, b_router):
    t, d = x.shape
    tm = ROW_TILE
    per_seq = seq // tm
    row = pl.BlockSpec((tm, d), lambda i: (i, 0))
    lane_row = pl.BlockSpec((tm, LANES), lambda i: (i, 0))
    w_hi = w_router.astype(BF16)
    acts = [x, mix] + ([gate] if gate is not None else [])
    return pl.pallas_call(
        functools.partial(_mem_attn_body, gated=gate is not None), grid=(t // tm,),
        in_specs=[row] * len(acts) + [_full((d, d)), pl.BlockSpec((mem_tokens, 2 * d), lambda i: (i // per_seq, 0)),
                                      _full((1, d)), _full((d, d)), _full((d, d)), _full((1, d)),
                                      _full((d, LANES)), _full((d, LANES)), _full((1, LANES))],
        out_specs=[row, row, lane_row, lane_row, _full((1, LANES))],
        out_shape=[jax.ShapeDtypeStruct((t, d), F32), jax.ShapeDtypeStruct((t, d), F32),
                   jax.ShapeDtypeStruct((t, LANES), I32), jax.ShapeDtypeStruct((t, LANES), F32),
                   jax.ShapeDtypeStruct((1, LANES), F32)],
        scratch_shapes=[pltpu.VMEM((1, LANES), F32)],
        compiler_params=_params("arbitrary"), name="mem_attn",
    )(*acts, w_mix.astype(BF16), kv, gn.reshape(1, d), (w_q * (d // MEM_HEADS) ** -0.5).astype(BF16),
      w_o.astype(BF16), g_ffn.reshape(1, d), w_hi, (w_router - w_hi.astype(F32)).astype(BF16), b_router)


def _route_tile(lg, idx_o, gate_o, cnt_o, carry):
    @pl.when(pl.program_id(0) == 0)
    def _():
        carry[...] = jnp.zeros_like(carry)

    tm = lg.shape[0]
    lane = lax.broadcasted_iota(I32, lg.shape, 1)
    lanef = lane.astype(F32)
    big = 1e9
    ninf = -jnp.inf
    gl = jnp.where(lane < N_GROUPS, lg, ninf)
    gmax = jnp.max(gl, axis=-1, keepdims=True)
    g_w = 1.0 / jnp.sum(jnp.exp(gl - gmax), axis=-1, keepdims=True)
    grp = jnp.min(jnp.where(gl == gmax, lanef, big), axis=-1, keepdims=True).astype(I32)
    in_grp = (lane >= N_GROUPS) & (lane < N_GROUPS + N_EXPERTS) & (((lane - N_GROUPS) >> 3) == grp)
    el = jnp.where(in_grp, lg, ninf)
    m1 = jnp.max(el, axis=-1, keepdims=True)
    i1 = jnp.min(jnp.where(el == m1, lanef, big), axis=-1, keepdims=True)
    el2 = jnp.where(lanef == i1, ninf, el)
    m2 = jnp.max(el2, axis=-1, keepdims=True)
    i2 = jnp.min(jnp.where(el2 == m2, lanef, big), axis=-1, keepdims=True)
    tt = jnp.exp(m2 - m1)
    w1 = 1.0 / (1.0 + tt)
    w2 = tt / (1.0 + tt)
    hit1 = lanef == i1
    hit2 = lanef == i2
    oh = jnp.where(hit1 | hit2, 1.0, 0.0)
    r_i = lax.broadcasted_iota(I32, (tm, tm), 0)
    c_i = lax.broadcasted_iota(I32, (tm, tm), 1)
    tri = jnp.where(r_i > c_i, 1.0, 0.0).astype(BF16)
    pos = _dot(tri, oh.astype(BF16)) + carry[...]
    pos1 = jnp.sum(jnp.where(hit1, pos, 0.0), axis=-1, keepdims=True)
    pos2 = jnp.sum(jnp.where(hit2, pos, 0.0), axis=-1, keepdims=True)
    carry[...] = carry[...] + jnp.sum(oh, axis=0, keepdims=True)
    cnt_o[...] = carry[...]
    idx = jnp.where(lane == 0, i1 - N_GROUPS, jnp.where(lane == 1, i2 - N_GROUPS,
          jnp.where(lane == 2, pos1, jnp.where(lane == 3, pos2, 0.0))))
    idx_o[...] = idx.astype(I32)
    gate_o[...] = jnp.where(lane == 0, g_w * w1, jnp.where(lane == 1, g_w * w2, 0.0))


def _row_copy(src, s, dst, d, sem):
    return pltpu.make_async_copy(src.at[pl.ds(s, 1)], dst.at[pl.ds(d, 1)], sem)


def _dispatch_body(pend_ref, pad_ref, dest_ref, xn_ref, xs_out, zero_buf, sem):
    tm = xn_ref.shape[0]
    blk = zero_buf.shape[0]

    @pl.when(pl.program_id(0) == 0)
    def _():
        zero_buf[...] = jnp.zeros_like(zero_buf)

        def clear_block(start):
            return pltpu.make_async_copy(zero_buf, xs_out.at[pl.ds(pl.multiple_of(start, blk), blk)], sem)

        def each_clear(act):
            def expert_block(e, c):
                @pl.when(pad_ref[e] > 0)
                def _():
                    act(clear_block(pend_ref[e] - blk))
                return c

            def tail_block(j, c):
                act(clear_block(j * blk))
                return c

            lax.fori_loop(0, N_EXPERTS, expert_block, 0)
            lax.fori_loop(pend_ref[N_EXPERTS - 1] // blk, xs_out.shape[0] // blk, tail_block, 0)

        each_clear(lambda cp: cp.start())
        each_clear(lambda cp: cp.wait())

    def issue(r, c):
        _row_copy(xn_ref, r, xs_out, dest_ref[0, 0, r], sem).start()
        _row_copy(xn_ref, r, xs_out, dest_ref[0, 0, tm + r], sem).start()
        return c

    lax.fori_loop(0, tm, issue, 0, unroll=DMA_UNROLL)
    for _ in range(2):
        pltpu.make_async_copy(xn_ref, xs_out.at[pl.ds(0, tm)], sem).wait()


def _dispatch(xn, dest, pend, padded, cap):
    t, d = xn.shape
    tm = ROW_TILE
    grid_spec = pltpu.PrefetchScalarGridSpec(
        num_scalar_prefetch=2, grid=(t // tm,),
        in_specs=[pl.BlockSpec((1, 1, 2 * tm), lambda i, pe, pa: (i, 0, 0), memory_space=pltpu.SMEM),
                  pl.BlockSpec((tm, d), lambda i, pe, pa: (i, 0))],
        out_specs=pl.BlockSpec(memory_space=pl.ANY),
        scratch_shapes=[pltpu.VMEM((MOE_BLOCK, d), F32), pltpu.SemaphoreType.DMA])
    return pl.pallas_call(
        _dispatch_body, grid_spec=grid_spec, out_shape=jax.ShapeDtypeStruct((cap, d), F32),
        compiler_params=_params("arbitrary"), name="moe_dispatch",
    )(pend, padded, dest, xn)


def _experts_body(be_ref, nb_ref, xs_ref, wg_ref, wu_ref, wd_ref, ys_ref):
    del be_ref
    live = pl.program_id(0) < nb_ref[0]

    @pl.when(live)
    def _():
        xb = xs_ref[...].astype(BF16)
        gt = _dot(xb, wg_ref[0])
        hid = gt * _sigmoid(gt) * _dot(xb, wu_ref[0])
        ys_ref[...] = _dot(hid.astype(BF16), wd_ref[0])

    @pl.when(jnp.logical_not(live))
    def _():
        ys_ref[...] = jnp.zeros_like(ys_ref)


def _experts(xs, block_expert, n_live, w_gate, w_up, w_down):
    cap, d = xs.shape
    f = w_gate.shape[2]
    blk = MOE_BLOCK
    w_in = pl.BlockSpec((1, d, f), lambda i, be, nb: (be[i], 0, 0))
    grid_spec = pltpu.PrefetchScalarGridSpec(
        num_scalar_prefetch=2, grid=(cap // blk,),
        in_specs=[pl.BlockSpec((blk, d), lambda i, be, nb: (jnp.minimum(i, nb[0] - 1), 0)),
                  w_in, w_in, pl.BlockSpec((1, f, d), lambda i, be, nb: (be[i], 0, 0))],
        out_specs=pl.BlockSpec((blk, d), lambda i, be, nb: (i, 0)))
    return pl.pallas_call(
        _experts_body, grid_spec=grid_spec, out_shape=jax.ShapeDtypeStruct((cap, d), F32),
        compiler_params=_params("arbitrary"), name="moe_experts",
    )(block_expert, n_live, xs, w_gate.astype(BF16), w_up.astype(BF16), w_down.astype(BF16))


def _combine_body(dest_ref, x_ref, gate_ref, ys_ref, *rest, n_proj, final_norm):
    norm_refs = rest[:n_proj + (1 if final_norm else 0)]
    w_refs = rest[len(norm_refs):len(norm_refs) + n_proj]
    outs = rest[len(norm_refs) + n_proj:-3]
    buf1, buf2, sem = rest[-3:]
    tm = x_ref.shape[0]

    def issue(r, c):
        _row_copy(ys_ref, dest_ref[0, 0, r], buf1, r, sem).start()
        _row_copy(ys_ref, dest_ref[0, 0, tm + r], buf2, r, sem).start()
        return c

    lax.fori_loop(0, tm, issue, 0, unroll=DMA_UNROLL)
    for buf in (buf1, buf2):
        pltpu.make_async_copy(ys_ref.at[pl.ds(0, tm)], buf, sem).wait()
    gate = gate_ref[...]
    x = x_ref[...] + gate[:, 0:1] * buf1[...] + gate[:, 1:2] * buf2[...]
    if final_norm:
        outs[0][...] = _rms(x, norm_refs[0][...])
        return
    outs[0][...] = x
    for j in range(n_proj):
        outs[1 + j][...] = _dot(_rms(x, norm_refs[j][...]).astype(BF16), w_refs[j][...]).astype(outs[1 + j].dtype)


def _combine(x, ys, dest, gates, norms, weights, final_norm=False):
    t, d = x.shape
    tm = ROW_TILE
    row = pl.BlockSpec((tm, d), lambda i: (i, 0))
    in_specs = [pl.BlockSpec((1, 1, 2 * tm), lambda i: (i, 0, 0), memory_space=pltpu.SMEM),
                row, pl.BlockSpec((tm, LANES), lambda i: (i, 0)), pl.BlockSpec(memory_space=pl.ANY)]
    in_specs += [_full((1, d))] * len(norms) + [_full(w.shape) for w in weights]
    out_specs = [row] + [pl.BlockSpec((tm, w.shape[1]), lambda i: (i, 0)) for w in weights]
    out_shape = [jax.ShapeDtypeStruct((t, d), F32)]
    out_shape += [jax.ShapeDtypeStruct((t, w.shape[1]), BF16) for w in weights]
    return pl.pallas_call(
        functools.partial(_combine_body, n_proj=len(weights), final_norm=final_norm),
        grid=(t // tm,), in_specs=in_specs, out_specs=out_specs, out_shape=out_shape,
        scratch_shapes=[pltpu.VMEM((tm, d), F32), pltpu.VMEM((tm, d), F32), pltpu.SemaphoreType.DMA],
        compiler_params=_params("arbitrary"), name="moe_combine",
    )(dest, x, gates, ys, *[g.reshape(1, d) for g in norms], *[w.astype(BF16) for w in weights])


def _moe(x, xn, routing, w_gate, w_up, w_down, norms, weights, final_norm=False):
    t, d = x.shape
    blk = MOE_BLOCK
    tm = ROW_TILE
    idx, gates, counts = routing
    counts = counts[0, N_GROUPS:N_GROUPS + N_EXPERTS].astype(I32)
    padded = (counts + blk - 1) // blk * blk
    pend = jnp.cumsum(padded)
    pstart = pend - padded
    cap = 2 * t + N_EXPERTS * blk
    n_blocks = cap // blk
    starts = jnp.arange(n_blocks, dtype=I32) * blk
    block_expert = jnp.minimum(jnp.sum(pend[None, :] <= starts[:, None], axis=1), N_EXPERTS - 1).astype(I32)
    n_live = (pend[-1:] // blk).astype(I32)
    chosen = idx[:, 0:2, None] == jnp.arange(N_EXPERTS, dtype=I32)
    dest = jnp.sum(jnp.where(chosen, pstart, 0), axis=-1) + idx[:, 2:4]
    dest = dest.reshape(t // tm, tm, 2).transpose(0, 2, 1).reshape(t // tm, 1, 2 * tm)
    xs = _dispatch(xn, dest, pend.astype(I32), padded.astype(I32), cap)
    ys = _experts(xs, block_expert, n_live, w_gate, w_up, w_down)
    return _combine(x, ys, dest, gates, norms, weights, final_norm)


def _sb_masks():
    row = lax.broadcasted_iota(I32, (SB_BLOCK, SB_WINDOW), 0)
    col = lax.broadcasted_iota(I32, (SB_BLOCK, SB_WINDOW), 1)
    visible = jnp.stack([col < row, col < row + SB_BLOCK, col >= 0, col < SB_BLOCK])
    return jnp.where(visible, 0.0, SB_HIDDEN).astype(F32)


def _sb_attn_body(q_ref, k_ref, v_ref, tri_ref, bias_ref, o_ref):
    i = pl.program_id(2)
    blk, win = SB_BLOCK, SB_WINDOW
    nh = q_ref.shape[1] // HEAD
    neg_suffix = tri_ref[...]
    lanes = [slice(h * HEAD, (h + 1) * HEAD) for h in range(nh)]
    qs = [(q_ref[:, sl].astype(F32) * (HEAD ** -0.5 * LOG2E)).astype(BF16) for sl in lanes]
    u32 = jnp.uint32

    def more(carry):
        hi, go = carry[0], carry[1]
        return jnp.logical_and(hi > 0, go)

    def window(carry):
        hi, _, accs, rests = carry
        lo = jnp.maximum(hi - win, 0)
        rows = pl.ds(pl.multiple_of(lo, blk), win)
        first = hi == (i + 1) * blk
        which = jnp.where(first, jnp.where(i == 0, 0, 1), jnp.where(hi < win, 3, 2))
        bias = bias_ref[which]
        zs, parts = [], []
        for h in range(nh):
            z = _dot_nt(qs[h], k_ref[rows, lanes[h]]) + bias
            neg_abs = lax.bitcast_convert_type(lax.bitcast_convert_type(z, u32) | u32(0x80000000), F32)
            sp = jnp.maximum(z, 0.0) + jnp.log(1.0 + jnp.exp2(neg_abs)) * LOG2E
            top = lax.bitcast_convert_type(lax.bitcast_convert_type(sp, u32) & u32(0xFFFF0000), F32)
            zs.append(z)
            parts += [top.astype(BF16), (sp - top).astype(BF16)]
        sums = _dot(jnp.concatenate(parts, axis=0), neg_suffix)
        new_accs, new_rests = [], []
        for h in range(nh):
            sfx = sums[2 * h * blk:(2 * h + 1) * blk] + sums[(2 * h + 1) * blk:(2 * h + 2) * blk]
            attn = jnp.exp2(zs[h] + sfx + rests[h])
            new_accs.append(accs[h] + _dot(attn.astype(BF16), v_ref[rows, lanes[h]]))
            new_rests.append(rests[h] + sfx[:, 0:1])
        worst = functools.reduce(jnp.maximum, new_rests)
        return lo, jnp.max(worst) > SB_ZERO_LOG2, tuple(new_accs), tuple(new_rests)

    init = ((i + 1) * blk, True, tuple(jnp.zeros((blk, HEAD), F32) for _ in range(nh)),
            tuple(jnp.zeros((blk, 1), F32) for _ in range(nh)))
    accs = lax.while_loop(more, window, window(init))[2]
    o_ref[...] = jnp.concatenate(accs, axis=1).astype(o_ref.dtype)


def _sb_attn(q, kv, batch, seq):
    t, d = q.shape
    nq = seq // SB_BLOCK
    w = SB_LANES
    hp = d // w
    win = SB_WINDOW
    assert d % w == 0 and seq % SB_BLOCK == 0 and seq >= win and win == 2 * SB_BLOCK
    tri = -jnp.tril(jnp.ones((win, win), F32)).astype(BF16)
    once = pl.Buffered(1)
    return pl.pallas_call(
        _sb_attn_body, grid=(batch, hp, nq),
        in_specs=[pl.BlockSpec((SB_BLOCK, w), lambda b, h, i: (b * nq + i, h)),
                  pl.BlockSpec((seq, w), lambda b, h, i: (b, h), pipeline_mode=once),
                  pl.BlockSpec((seq, w), lambda b, h, i: (b, hp + h), pipeline_mode=once),
                  pl.BlockSpec((win, win), lambda b, h, i: (0, 0)),
                  pl.BlockSpec((4, SB_BLOCK, win), lambda b, h, i: (0, 0, 0))],
        out_specs=pl.BlockSpec((SB_BLOCK, w), lambda b, h, i: (b * nq + i, h)),
        out_shape=jax.ShapeDtypeStruct((t, d), BF16),
        compiler_params=_params("parallel", "parallel", "arbitrary"), name="sb_attn",
    )(q, kv, kv, tri, _sb_masks())


def _router_weights(w_group, b_group, w_expert, b_expert):
    d = w_group.shape[0]
    pad = LANES - N_GROUPS - N_EXPERTS
    w = jnp.concatenate([w_group, w_expert, jnp.zeros((d, pad), F32)], axis=1)
    b = jnp.concatenate([b_group, b_expert, jnp.zeros((pad,), F32)]).reshape(1, LANES)
    return w, b


def kernel(x, mem, norm_mix, norm_mem, norm_ffn, norm_final, rwkv_mu, rwkv_w_rkv, rwkv_w0, rwkv_w1, rwkv_w2, rwkv_a0, rwkv_a1, rwkv_a2, rwkv_g1, rwkv_g2, rwkv_k_k, rwkv_k_a, rwkv_r_k, rwkv_ln_g, rwkv_ln_b, rwkv_w_o, kv_norm, w_kv_shared, sb_w_q, sb_w_o, mem_norm_kv, mem_w_q, mem_w_kv, mem_w_o, moe_w_group, moe_b_group, moe_w_expert, moe_b_expert, moe_w_gate, moe_w_up, moe_w_down):
    batch, seq, d = x.shape
    depth = norm_mix.shape[0]
    n_a = rwkv_mu.shape[0]
    mem_tokens = mem.shape[1]
    xs = x.reshape(batch * seq, d)
    memf = mem.reshape(batch * mem_tokens, d)
    q = kv = None
    for l in range(depth):
        if l < n_a:
            r, lw, k, v, kk, asg, g = _rwkv_proj(
                xs, seq, norm_mix[l], rwkv_mu[l], rwkv_w_rkv[l], rwkv_w0[l], rwkv_w1[l], rwkv_w2[l],
                rwkv_a0[l], rwkv_a1[l], rwkv_a2[l], rwkv_g1[l], rwkv_g2[l], rwkv_k_k[l], rwkv_k_a[l])
            mix = _wide_scan(r, lw, k, v, kk, asg, rwkv_r_k[l], rwkv_ln_g[l], rwkv_ln_b[l], batch, seq)
            gate, w_mix = g, rwkv_w_o[l]
        else:
            mix = _sb_attn(q, kv, batch, seq)
            gate, w_mix = None, sb_w_o[l - n_a]
        mkv = _norm_linear(memf, mem_norm_kv[l], mem_w_kv[l], BF16)
        w_router, b_router = _router_weights(moe_w_group[l], moe_b_group[l], moe_w_expert[l], moe_b_expert[l])
        xs, xn, *routing = _mem_attn(xs, mix, gate, w_mix, mkv, seq, mem_tokens, norm_mem[l], mem_w_q[l],
                                     mem_w_o[l], norm_ffn[l], w_router, b_router)
        last = l == depth - 1
        if last:
            norms, weights = [norm_final], []
        elif l + 1 < n_a:
            norms, weights = [], []
        else:
            norms, weights = [norm_mix[l + 1]], [sb_w_q[l + 1 - n_a]]
            if l == n_a - 1:
                norms, weights = norms + [kv_norm], weights + [w_kv_shared]
        res = _moe(xs, xn, routing, moe_w_gate[l], moe_w_up[l], moe_w_down[l], norms, weights, final_norm=last)
        if last:
            return res[0].reshape(batch, seq, d)
        xs = res[0]
        if weights:
            q = res[1]
            if l == n_a - 1:
                kv = res[2]
```

```python
import functools

import jax
import jax.numpy as jnp
from jax import lax
from jax.experimental import pallas as pl
from jax.experimental.pallas import tpu as pltpu

F32 = jnp.float32
BF16 = jnp.bfloat16
I32 = jnp.int32

HEAD = 64
MEM_HEADS = 4
N_GROUPS = 4
EXPERTS_PER_GROUP = 8
N_EXPERTS = N_GROUPS * EXPERTS_PER_GROUP
NORM_EPS = 1e-6
RWKV_LN_EPS = 64e-5

LANES = 128
CHUNK = 64
SB_BLOCK = 128
SB_WINDOW = 256
SB_LANES = 1024
SB_ZERO_LOG2 = -160.0
SB_HIDDEN = -1e30
LOG2E = 1.4426950408889634
MOE_BLOCK = 512
DMA_UNROLL = 8
ROW_TILE = 512
PROJ_TILE = 512
SCAN_ROWS = 512
SCAN_LANES = 256
SCAN_GROUP = 8
VMEM_LIMIT = 56 * 1024 * 1024


def _params(*sem):
    return pltpu.CompilerParams(dimension_semantics=sem, vmem_limit_bytes=VMEM_LIMIT)


def _dot(a, b):
    return jnp.dot(a, b, preferred_element_type=F32)


def _dot_nt(a, b):
    return lax.dot_general(a, b, (((1,), (1,)), ((), ())), preferred_element_type=F32)


def _dot_tn(a, b):
    return lax.dot_general(a, b, (((0,), (0,)), ((), ())), preferred_element_type=F32)


def _rms(x, g):
    ms = jnp.mean(x * x, axis=-1, keepdims=True)
    return x * lax.rsqrt(ms + NORM_EPS) * g


def _sigmoid(x):
    return 1.0 / (1.0 + jnp.exp(-x))


def _softplus(x):
    return jnp.maximum(x, 0.0) + jnp.log(1.0 + jnp.exp(-jnp.abs(x)))


def _full(shape):
    return pl.BlockSpec(shape, lambda *_: (0,) * len(shape))


def _rwkv_proj_body(x_ref, xp_ref, gn_ref, mu_ref, wr_ref, wk_ref, wv_ref, w1_ref, a1_ref, g1_ref,
                    w2_ref, a2_ref, g2_ref, vec_ref,
                    r_o, lw_o, k_o, v_o, kk_o, as_o, g_o, *, tiles_per_seq):
    i = pl.program_id(0)
    gn = gn_ref[...]
    h = _rms(x_ref[...], gn)
    hp = _rms(xp_ref[...], gn)[7:8, :]
    hp = jnp.where(lax.rem(i, tiles_per_seq) == 0, 0.0, hp)
    row = lax.broadcasted_iota(I32, (h.shape[0], 1), 0)
    hprev = jnp.where(row == 0, hp, pltpu.roll(h, 1, 0))
    xx = hprev - h
    mu = mu_ref[...]

    def lerp(j):
        return (h + xx * mu[j:j + 1, :]).astype(BF16)

    vec = vec_ref[...]
    r = _dot(lerp(0), wr_ref[...])
    k = _dot(lerp(1), wk_ref[...])
    v = _dot(lerp(2), wv_ref[...])
    w = vec[0:1] + _dot(jnp.tanh(_dot(lerp(3), w1_ref[...])).astype(BF16), w2_ref[...])
    w = -_softplus(-w) - 0.5
    a = _sigmoid(vec[1:2] + _dot(_dot(lerp(4), a1_ref[...]).astype(BF16), a2_ref[...]))
    g = _dot(_sigmoid(_dot(lerp(5), g1_ref[...])).astype(BF16), g2_ref[...])
    r_o[...] = r.astype(BF16)
    lw_o[...] = -jnp.exp(w)
    k_o[...] = (k * (1.0 + (a - 1.0) * vec[3:4])).astype(BF16)
    v_o[...] = v.astype(BF16)
    kk_o[...] = (k * vec[2:3]).astype(BF16)
    as_o[...] = a.astype(BF16)
    g_o[...] = g.astype(BF16)


def _rwkv_proj(x, seq, gn, mu, w_rkv, w0, w1, w2, a0, a1, a2, g1, g2, k_k, k_a):
    t, d = x.shape
    tm = PROJ_TILE
    pad8 = lambda rows: jnp.concatenate([rows, jnp.zeros((8 - rows.shape[0], d), F32)], axis=0)
    vec = pad8(jnp.stack([w0, a0, k_k, k_a]))
    mu8 = pad8(mu)
    wts = [w_rkv[0], w_rkv[1], w_rkv[2], w1, a1, g1, w2, a2, g2]
    wts = [w.astype(BF16) for w in wts]
    row_spec = pl.BlockSpec((tm, d), lambda i: (i, 0))
    prev_spec = pl.BlockSpec((8, d), lambda i: (jnp.maximum(i * (tm // 8) - 1, 0), 0))
    in_specs = [row_spec, prev_spec, _full((1, d)), _full((8, d))]
    in_specs += [_full(w.shape) for w in wts] + [_full((8, d))]
    outs = [jax.ShapeDtypeStruct((t, d), dt) for dt in (BF16, F32, BF16, BF16, BF16, BF16, BF16)]
    return pl.pallas_call(
        functools.partial(_rwkv_proj_body, tiles_per_seq=seq // tm),
        grid=(t // tm,), in_specs=in_specs, out_specs=[row_spec] * 7, out_shape=outs,
        compiler_params=_params("parallel"), name="rwkv_proj",
    )(x, x, gn.reshape(1, d), mu8, *wts, vec)


def _cumsum_rows(x):
    n = x.shape[0]
    row = lax.broadcasted_iota(I32, (n, 1), 0)
    s = 1
    while s < n:
        x = x + jnp.where(row >= s, pltpu.roll(x, s, 0), 0.0)
        s *= 2
    return x


def _split_bf16(x):
    top = lax.bitcast_convert_type(lax.bitcast_convert_type(x, jnp.uint32) & jnp.uint32(0xFFFF0000), F32)
    return top.astype(BF16), (x - top).astype(BF16)


def _wide_scan_body(r_ref, lw_ref, k_ref, v_ref, kk_ref, as_ref, rk_ref, lg_ref, lb_ref, y_ref,
                    state, o0_s, n_s, gc_s, g_s):
    c = CHUNK
    n = HEAD
    w = r_ref.shape[1]
    nh = w // n
    n_chunks = r_ref.shape[0] // c

    @pl.when(pl.program_id(2) == 0)
    def _():
        state[...] = jnp.zeros_like(state)

    row = lax.broadcasted_iota(I32, (c, w), 0)
    col = lax.broadcasted_iota(I32, (c, w), 1) & (n - 1)
    strict = row > col
    incl = row >= col
    eye = (row == col).astype(F32)
    same_head = (lax.broadcasted_iota(I32, (w, w), 0) // n) == (lax.broadcasted_iota(I32, (w, w), 1) // n)
    head_ones = jnp.where(same_head, 1.0, 0.0).astype(BF16)
    eye_c = (lax.broadcasted_iota(I32, (c, c), 0) == lax.broadcasted_iota(I32, (c, c), 1)).astype(BF16)

    def bd(x):
        return jnp.concatenate([x] * nh, axis=0) * head_ones

    def head_sums(*xs):
        parts = [p for x in xs for p in _split_bf16(x)]
        sums = _dot(jnp.concatenate(parts, axis=0), head_ones)
        return [sums[2 * j * c:(2 * j + 1) * c] + sums[(2 * j + 1) * c:(2 * j + 2) * c] for j in range(len(xs))]

    def local(cj, carry):
        cis, rowss, a_alls, r_alls, lhss, rhss, g_lasts = [], [], [], [], [], [], []
        for sub in range(SCAN_GROUP):
            ci = cj * SCAN_GROUP + sub
            rows = pl.ds(pl.multiple_of(ci * c, c), c)
            r = r_ref[rows, :].astype(F32)
            lw = lw_ref[rows, :]
            k = k_ref[rows, :].astype(F32)
            kkr = kk_ref[rows, :].astype(F32)
            asg = as_ref[rows, :].astype(F32)
            cum = _cumsum_rows(lw)
            g_in = jnp.exp(cum)
            g_ex = jnp.exp(cum - lw)
            g_inv = jnp.exp(-cum)
            g_s[ci] = g_in[c - 1:c, :]
            g_lasts.append(g_in[c - 1:c, :])
            kkn = kkr * lax.rsqrt(jnp.maximum(head_sums(kkr * kkr)[0], 1e-24))
            a_all = (-kkn * g_ex).astype(BF16)
            b_all = (kkn * asg * g_inv).astype(BF16)
            k_all = (k * g_inv).astype(BF16)
            r_all = (r * g_in).astype(BF16)
            cis.append(ci)
            rowss.append(rows)
            a_alls.append(a_all)
            r_alls.append(r_all)
            lhss.append(jnp.concatenate([a_all, r_all], axis=0))
            rhss.append((b_all, k_all))
        pbs = [_dot_nt(lhs, bd(b)) for lhs, (b, _) in zip(lhss, rhss)]
        pks = [_dot_nt(lhs, bd(k)) for lhs, (_, k) in zip(lhss, rhss)]
        lps = [jnp.where(strict, pb[:c], 0.0) for pb in pbs]
        xos = [_dot(jnp.concatenate([jnp.where(strict, pk[:c], 0.0), jnp.where(incl, pk[c:], 0.0)],
                                    axis=0).astype(BF16), bd(v_ref[rows, :])) for pk, rows in zip(pks, rowss)]
        arbs = [jnp.where(incl, pb[c:], 0.0).astype(BF16) for pb in pbs]
        tinvs = [eye + lp for lp in lps]
        lpbs = [lp.astype(BF16) for lp in lps]
        lpbs = [_dot(lpb, bd(lpb)).astype(BF16) for lpb in lpbs]
        s = 4
        while s < c:
            both = [_dot(jnp.concatenate([t.astype(BF16), lpb], axis=0), bd(lpb)) for t, lpb in zip(tinvs, lpbs)]
            tinvs = [t + bo[:c] for t, bo in zip(tinvs, both)]
            lpbs = [bo[c:].astype(BF16) for bo in both]
            s *= 2
        tinvs = [t + _dot(t.astype(BF16), bd(lpb)) for t, lpb in zip(tinvs, lpbs)]
        tbs = [t.astype(BF16) for t in tinvs]
        wbs = [_dot(tb, bd(a)).astype(BF16) for tb, a in zip(tbs, a_alls)]
        u0bs = [_dot(tb, bd(xo[:c].astype(BF16))).astype(BF16) for tb, xo in zip(tbs, xos)]
        qs = [r_all.astype(F32) + _dot(arb, bd(wb)) for r_all, arb, wb in zip(r_alls, arbs, wbs)]
        o0s = [xo[c:] + _dot(arb, bd(u0b)) for xo, arb, u0b in zip(xos, arbs, u0bs)]
        ncs = [jnp.where(same_head, _dot_tn(wb, b), 0.0) * g for wb, (b, _), g in zip(wbs, rhss, g_lasts)]
        gcs = [jnp.where(same_head, _dot_tn(jnp.concatenate([u0b, v_ref[rows, :]], axis=0),
                                            jnp.concatenate([b, k], axis=0)), 0.0) * g
               for u0b, rows, (b, k), g in zip(u0bs, rowss, rhss, g_lasts)]
        qts = [_dot_tn(q.astype(BF16), eye_c) for q in qs]
        for ci, qt, o0, nc, gc in zip(cis, qts, o0s, ncs, gcs):
            o0_s[ci] = o0
            n_s[ci] = jnp.concatenate([nc, qt, jnp.zeros((w, LANES - c), F32)], axis=1).astype(BF16)
            gc_s[ci] = gc
        return carry

    lax.fori_loop(0, n_chunks // SCAN_GROUP, local, 0)

    rk = rk_ref[...]
    lng = lg_ref[...]
    lnb = lb_ref[...]

    def sweep(ci, carry):
        s = state[...]
        both = _dot(s.astype(BF16), n_s[ci])
        state[...] = s * g_s[ci] + both[:, :w] + gc_s[ci]
        o0_s[ci] = jnp.transpose(both[:, w:])[:c] + o0_s[ci]
        return carry

    lax.fori_loop(0, n_chunks, sweep, 0)

    rowss = [pl.ds(ci * c, c) for ci in range(n_chunks)]
    os_ = [o0_s[ci] for ci in range(n_chunks)]
    sums = [head_sums(o, r_ref[rows, :].astype(F32) * k_ref[rows, :].astype(F32) * rk) for o, rows in zip(os_, rowss)]
    devs = [o - o_sum * (1.0 / n) for o, (o_sum, _) in zip(os_, sums)]
    vars_ = [head_sums(dev * dev)[0] * (1.0 / n) for dev in devs]
    for rows, dev, var, (_, bonus) in zip(rowss, devs, vars_, sums):
        y = dev * lax.rsqrt(var + RWKV_LN_EPS) * lng + lnb
        y_ref[rows, :] = (y + bonus * v_ref[rows, :].astype(F32)).astype(y_ref.dtype)


def _wide_scan(r, lw, k, v, kk, asg, r_k, ln_g, ln_b, batch, seq):
    t, d = r.shape
    rb = SCAN_ROWS
    w = SCAN_LANES
    nc = seq // rb
    n_chunks = rb // CHUNK
    assert d % w == 0 and seq % rb == 0 and n_chunks % SCAN_GROUP == 0
    blk = pl.BlockSpec((rb, w), lambda b, hp, ci: (b * nc + ci, hp))
    vec = pl.BlockSpec((1, w), lambda b, hp, ci: (0, hp))
    per_chunk = lambda rows, cols, dt: pltpu.VMEM((n_chunks, rows, cols), dt)
    scratch = [pltpu.VMEM((w, w), F32),
               per_chunk(CHUNK, w, F32),
               per_chunk(w, w + LANES, BF16),
               per_chunk(w, w, F32),
               per_chunk(1, w, F32)]
    return pl.pallas_call(
        _wide_scan_body,
        grid=(batch, d // w, nc),
        in_specs=[blk] * 6 + [vec] * 3, out_specs=blk,
        out_shape=jax.ShapeDtypeStruct((t, d), BF16),
        scratch_shapes=scratch,
        compiler_params=_params("parallel", "parallel", "arbitrary"), name="rwkv_scan",
    )(r, lw, k, v, kk, asg, r_k.reshape(1, d), ln_g.reshape(1, d), ln_b.reshape(1, d))


def _norm_linear_body(x_ref, g_ref, w_ref, o_ref):
    o_ref[...] = _dot(_rms(x_ref[...], g_ref[...]).astype(BF16), w_ref[...]).astype(o_ref.dtype)


def _norm_linear(x, g, w, out_dtype):
    t, d = x.shape
    tm = ROW_TILE
    return pl.pallas_call(
        _norm_linear_body, grid=(t // tm,),
        in_specs=[pl.BlockSpec((tm, d), lambda i: (i, 0)), _full((1, d)), _full(w.shape)],
        out_specs=pl.BlockSpec((tm, w.shape[1]), lambda i: (i, 0)),
        out_shape=jax.ShapeDtypeStruct((t, w.shape[1]), out_dtype),
        compiler_params=_params("parallel"), name="norm_linear",
    )(x, g.reshape(1, d), w.astype(BF16))


def _mem_attn_body(x_ref, a_ref, *rest, gated):
    if gated:
        mix = a_ref[...] * rest[0][...]
        rest = rest[1:]
    else:
        mix = a_ref[...]
    (wm_ref, kv_ref, gn_ref, wq_ref, wo_ref, gf_ref, wrh_ref, wrl_ref, br_ref,
     x_o, xn_o, idx_o, gate_o, cnt_o, carry) = rest
    x = x_ref[...] + _dot(mix, wm_ref[...])
    d = x.shape[1]
    hd = d // MEM_HEADS
    q = _dot(_rms(x, gn_ref[...]).astype(BF16), wq_ref[...])
    outs = []
    for hh in range(MEM_HEADS):
        qh = q[:, hh * hd:(hh + 1) * hd].astype(BF16)
        kh = kv_ref[:, hh * hd:(hh + 1) * hd]
        vh = kv_ref[:, d + hh * hd:d + (hh + 1) * hd]
        s = _dot_nt(qh, kh)
        p = jnp.exp(s - jnp.max(s, axis=-1, keepdims=True))
        o = _dot(p.astype(BF16), vh) / jnp.sum(p, axis=-1, keepdims=True)
        outs.append(o.astype(BF16))
    xnew = x + _dot(jnp.concatenate(outs, axis=1), wo_ref[...])
    x_o[...] = xnew
    xn = _rms(xnew, gf_ref[...])
    xn_o[...] = xn
    xh = xn.astype(BF16)
    xl = (xn - xh.astype(F32)).astype(BF16)
    tm = x.shape[0]
    hw = _dot(jnp.concatenate([xh, xl], axis=0), wrh_ref[...])
    logits = hw[:tm] + hw[tm:] + _dot(xh, wrl_ref[...]) + br_ref[...]
    _route_tile(logits, idx_o, gate_o, cnt_o, carry)


def _mem_attn(x, mix, gate, w_mix, kv, seq, mem_tokens, gn, w_q, w_o, g_ffn, w_router, b_router):
    t, d = x.shape
    tm = ROW_TILE
    per_seq = seq // tm
    row = pl.BlockSpec((tm, d), lambda i: (i, 0))
    lane_row = pl.BlockSpec((tm, LANES), lambda i: (i, 0))
    w_hi = w_router.astype(BF16)
    acts = [x, mix] + ([gate] if gate is not None else [])
    return pl.pallas_call(
        functools.partial(_mem_attn_body, gated=gate is not None), grid=(t // tm,),
        in_specs=[row] * len(acts) + [_full((d, d)), pl.BlockSpec((mem_tokens, 2 * d), lambda i: (i // per_seq, 0)),
                                      _full((1, d)), _full((d, d)), _full((d, d)), _full((1, d)),
                                      _full((d, LANES)), _full((d, LANES)), _full((1, LANES))],
        out_specs=[row, row, lane_row, lane_row, _full((1, LANES))],
        out_shape=[jax.ShapeDtypeStruct((t, d), F32), jax.ShapeDtypeStruct((t, d), F32),
                   jax.ShapeDtypeStruct((t, LANES), I32), jax.ShapeDtypeStruct((t, LANES), F32),
                   jax.ShapeDtypeStruct((1, LANES), F32)],
        scratch_shapes=[pltpu.VMEM((1, LANES), F32)],
        compiler_params=_params("arbitrary"), name="mem_attn",
    )(*acts, w_mix.astype(BF16), kv, gn.reshape(1, d), (w_q * (d // MEM_HEADS) ** -0.5).astype(BF16),
      w_o.astype(BF16), g_ffn.reshape(1, d), w_hi, (w_router - w_hi.astype(F32)).astype(BF16), b_router)


def _route_tile(lg, idx_o, gate_o, cnt_o, carry):
    @pl.when(pl.program_id(0) == 0)
    def _():
        carry[...] = jnp.zeros_like(carry)

    tm = lg.shape[0]
    lane = lax.broadcasted_iota(I32, lg.shape, 1)
    lanef = lane.astype(F32)
    big = 1e9
    ninf = -jnp.inf
    gl = jnp.where(lane < N_GROUPS, lg, ninf)
    gmax = jnp.max(gl, axis=-1, keepdims=True)
    g_w = 1.0 / jnp.sum(jnp.exp(gl - gmax), axis=-1, keepdims=True)
    grp = jnp.min(jnp.where(gl == gmax, lanef, big), axis=-1, keepdims=True).astype(I32)
    in_grp = (lane >= N_GROUPS) & (lane < N_GROUPS + N_EXPERTS) & (((lane - N_GROUPS) >> 3) == grp)
    el = jnp.where(in_grp, lg, ninf)
    m1 = jnp.max(el, axis=-1, keepdims=True)
    i1 = jnp.min(jnp.where(el == m1, lanef, big), axis=-1, keepdims=True)
    el2 = jnp.where(lanef == i1, ninf, el)
    m2 = jnp.max(el2, axis=-1, keepdims=True)
    i2 = jnp.min(jnp.where(el2 == m2, lanef, big), axis=-1, keepdims=True)
    tt = jnp.exp(m2 - m1)
    w1 = 1.0 / (1.0 + tt)
    w2 = tt / (1.0 + tt)
    hit1 = lanef == i1
    hit2 = lanef == i2
    oh = jnp.where(hit1 | hit2, 1.0, 0.0)
    r_i = lax.broadcasted_iota(I32, (tm, tm), 0)
    c_i = lax.broadcasted_iota(I32, (tm, tm), 1)
    tri = jnp.where(r_i > c_i, 1.0, 0.0).astype(BF16)
    pos = _dot(tri, oh.astype(BF16)) + carry[...]
    pos1 = jnp.sum(jnp.where(hit1, pos, 0.0), axis=-1, keepdims=True)
    pos2 = jnp.sum(jnp.where(hit2, pos, 0.0), axis=-1, keepdims=True)
    carry[...] = carry[...] + jnp.sum(oh, axis=0, keepdims=True)
    cnt_o[...] = carry[...]
    idx = jnp.where(lane == 0, i1 - N_GROUPS, jnp.where(lane == 1, i2 - N_GROUPS,
          jnp.where(lane == 2, pos1, jnp.where(lane == 3, pos2, 0.0))))
    idx_o[...] = idx.astype(I32)
    gate_o[...] = jnp.where(lane == 0, g_w * w1, jnp.where(lane == 1, g_w * w2, 0.0))


def _row_copy(src, s, dst, d, sem):
    return pltpu.make_async_copy(src.at[pl.ds(s, 1)], dst.at[pl.ds(d, 1)], sem)


def _dispatch_body(pend_ref, pad_ref, dest_ref, xn_ref, xs_out, zero_buf, sem):
    tm = xn_ref.shape[0]
    blk = zero_buf.shape[0]

    @pl.when(pl.program_id(0) == 0)
    def _():
        zero_buf[...] = jnp.zeros_like(zero_buf)

        def clear_block(start):
            return pltpu.make_async_copy(zero_buf, xs_out.at[pl.ds(pl.multiple_of(start, blk), blk)], sem)

        def each_clear(act):
            def expert_block(e, c):
                @pl.when(pad_ref[e] > 0)
                def _():
                    act(clear_block(pend_ref[e] - blk))
                return c

            def tail_block(j, c):
                act(clear_block(j * blk))
                return c

            lax.fori_loop(0, N_EXPERTS, expert_block, 0)
            lax.fori_loop(pend_ref[N_EXPERTS - 1] // blk, xs_out.shape[0] // blk, tail_block, 0)

        each_clear(lambda cp: cp.start())
        each_clear(lambda cp: cp.wait())

    def issue(r, c):
        _row_copy(xn_ref, r, xs_out, dest_ref[0, 0, r], sem).start(priority=0)
        _row_copy(xn_ref, r, xs_out, dest_ref[0, 0, tm + r], sem).start(priority=1)
        return c

    lax.fori_loop(0, tm, issue, 0, unroll=DMA_UNROLL)
    for _ in range(2):
        pltpu.make_async_copy(xn_ref, xs_out.at[pl.ds(0, tm)], sem).wait()


def _dispatch(xn, dest, pend, padded, cap):
    t, d = xn.shape
    tm = ROW_TILE
    grid_spec = pltpu.PrefetchScalarGridSpec(
        num_scalar_prefetch=2, grid=(t // tm,),
        in_specs=[pl.BlockSpec((1, 1, 2 * tm), lambda i, pe, pa: (i, 0, 0), memory_space=pltpu.SMEM),
                  pl.BlockSpec((tm, d), lambda i, pe, pa: (i, 0))],
        out_specs=pl.BlockSpec(memory_space=pl.ANY),
        scratch_shapes=[pltpu.VMEM((MOE_BLOCK, d), F32), pltpu.SemaphoreType.DMA])
    return pl.pallas_call(
        _dispatch_body, grid_spec=grid_spec, out_shape=jax.ShapeDtypeStruct((cap, d), F32),
        compiler_params=_params("arbitrary"), name="moe_dispatch",
    )(pend, padded, dest, xn)


def _experts_body(be_ref, nb_ref, xs_ref, wg_ref, wu_ref, wd_ref, ys_ref):
    del be_ref
    live = pl.program_id(0) < nb_ref[0]

    @pl.when(live)
    def _():
        xb = xs_ref[...].astype(BF16)
        gt = _dot(xb, wg_ref[0])
        hid = gt * _sigmoid(gt) * _dot(xb, wu_ref[0])
        ys_ref[...] = _dot(hid.astype(BF16), wd_ref[0])

    @pl.when(jnp.logical_not(live))
    def _():
        ys_ref[...] = jnp.zeros_like(ys_ref)


def _experts(xs, block_expert, n_live, w_gate, w_up, w_down):
    cap, d = xs.shape
    f = w_gate.shape[2]
    blk = MOE_BLOCK
    w_in = pl.BlockSpec((1, d, f), lambda i, be, nb: (be[i], 0, 0))
    grid_spec = pltpu.PrefetchScalarGridSpec(
        num_scalar_prefetch=2, grid=(cap // blk,),
        in_specs=[pl.BlockSpec((blk, d), lambda i, be, nb: (jnp.minimum(i, nb[0] - 1), 0)),
                  w_in, w_in, pl.BlockSpec((1, f, d), lambda i, be, nb: (be[i], 0, 0))],
        out_specs=pl.BlockSpec((blk, d), lambda i, be, nb: (i, 0)))
    return pl.pallas_call(
        _experts_body, grid_spec=grid_spec, out_shape=jax.ShapeDtypeStruct((cap, d), F32),
        compiler_params=_params("arbitrary"), name="moe_experts",
    )(block_expert, n_live, xs, w_gate.astype(BF16), w_up.astype(BF16), w_down.astype(BF16))


def _combine_body(dest_ref, x_ref, gate_ref, ys_ref, *rest, n_proj, final_norm):
    norm_refs = rest[:n_proj + (1 if final_norm else 0)]
    w_refs = rest[len(norm_refs):len(norm_refs) + n_proj]
    outs = rest[len(norm_refs) + n_proj:-3]
    buf1, buf2, sem = rest[-3:]
    tm = x_ref.shape[0]

    def issue(r, c):
        _row_copy(ys_ref, dest_ref[0, 0, r], buf1, r, sem).start(priority=0)
        _row_copy(ys_ref, dest_ref[0, 0, tm + r], buf2, r, sem).start(priority=1)
        return c

    lax.fori_loop(0, tm, issue, 0, unroll=DMA_UNROLL)
    for buf in (buf1, buf2):
        pltpu.make_async_copy(ys_ref.at[pl.ds(0, tm)], buf, sem).wait()
    gate = gate_ref[...]
    x = x_ref[...] + gate[:, 0:1] * buf1[...] + gate[:, 1:2] * buf2[...]
    if final_norm:
        outs[0][...] = _rms(x, norm_refs[0][...])
        return
    outs[0][...] = x
    for j in range(n_proj):
        outs[1 + j][...] = _dot(_rms(x, norm_refs[j][...]).astype(BF16), w_refs[j][...]).astype(outs[1 + j].dtype)


def _combine(x, ys, dest, gates, norms, weights, final_norm=False):
    t, d = x.shape
    tm = ROW_TILE
    row = pl.BlockSpec((tm, d), lambda i: (i, 0))
    in_specs = [pl.BlockSpec((1, 1, 2 * tm), lambda i: (i, 0, 0), memory_space=pltpu.SMEM),
                row, pl.BlockSpec((tm, LANES), lambda i: (i, 0)), pl.BlockSpec(memory_space=pl.ANY)]
    in_specs += [_full((1, d))] * len(norms) + [_full(w.shape) for w in weights]
    out_specs = [row] + [pl.BlockSpec((tm, w.shape[1]), lambda i: (i, 0)) for w in weights]
    out_shape = [jax.ShapeDtypeStruct((t, d), F32)]
    out_shape += [jax.ShapeDtypeStruct((t, w.shape[1]), BF16) for w in weights]
    return pl.pallas_call(
        functools.partial(_combine_body, n_proj=len(weights), final_norm=final_norm),
        grid=(t // tm,), in_specs=in_specs, out_specs=out_specs, out_shape=out_shape,
        scratch_shapes=[pltpu.VMEM((tm, d), F32), pltpu.VMEM((tm, d), F32), pltpu.SemaphoreType.DMA],
        compiler_params=_params("arbitrary"), name="moe_combine",
    )(dest, x, gates, ys, *[g.reshape(1, d) for g in norms], *[w.astype(BF16) for w in weights])


def _moe(x, xn, routing, w_gate, w_up, w_down, norms, weights, final_norm=False):
    t, d = x.shape
    blk = MOE_BLOCK
    tm = ROW_TILE
    idx, gates, counts = routing
    counts = counts[0, N_GROUPS:N_GROUPS + N_EXPERTS].astype(I32)
    padded = (counts + blk - 1) // blk * blk
    pend = jnp.cumsum(padded)
    pstart = pend - padded
    cap = 2 * t + N_EXPERTS * blk
    n_blocks = cap // blk
    starts = jnp.arange(n_blocks, dtype=I32) * blk
    block_expert = jnp.minimum(jnp.sum(pend[None, :] <= starts[:, None], axis=1), N_EXPERTS - 1).astype(I32)
    n_live = (pend[-1:] // blk).astype(I32)
    chosen = idx[:, 0:2, None] == jnp.arange(N_EXPERTS, dtype=I32)
    dest = jnp.sum(jnp.where(chosen, pstart, 0), axis=-1) + idx[:, 2:4]
    dest = dest.reshape(t // tm, tm, 2).transpose(0, 2, 1).reshape(t // tm, 1, 2 * tm)
    xs = _dispatch(xn, dest, pend.astype(I32), padded.astype(I32), cap)
    ys = _experts(xs, block_expert, n_live, w_gate, w_up, w_down)
    return _combine(x, ys, dest, gates, norms, weights, final_norm)


def _sb_masks():
    row = lax.broadcasted_iota(I32, (SB_BLOCK, SB_WINDOW), 0)
    col = lax.broadcasted_iota(I32, (SB_BLOCK, SB_WINDOW), 1)
    visible = jnp.stack([col < row, col < row + SB_BLOCK, col >= 0, col < SB_BLOCK])
    return jnp.where(visible, 0.0, SB_HIDDEN).astype(F32)


def _sb_attn_body(q_ref, k_ref, v_ref, tri_ref, bias_ref, o_ref):
    i = pl.program_id(2)
    blk, win = SB_BLOCK, SB_WINDOW
    nh = q_ref.shape[1] // HEAD
    neg_suffix = tri_ref[...]
    lanes = [slice(h * HEAD, (h + 1) * HEAD) for h in range(nh)]
    qs = [(q_ref[:, sl].astype(F32) * (HEAD ** -0.5 * LOG2E)).astype(BF16) for sl in lanes]
    u32 = jnp.uint32

    def more(carry):
        hi, go = carry[0], carry[1]
        return jnp.logical_and(hi > 0, go)

    def window(carry):
        hi, _, accs, rests = carry
        lo = jnp.maximum(hi - win, 0)
        rows = pl.ds(pl.multiple_of(lo, blk), win)
        first = hi == (i + 1) * blk
        which = jnp.where(first, jnp.where(i == 0, 0, 1), jnp.where(hi < win, 3, 2))
        bias = bias_ref[which]
        zs, parts = [], []
        for h in range(nh):
            z = _dot_nt(qs[h], k_ref[rows, lanes[h]]) + bias
            neg_abs = lax.bitcast_convert_type(lax.bitcast_convert_type(z, u32) | u32(0x80000000), F32)
            sp = jnp.maximum(z, 0.0) + jnp.log(1.0 + jnp.exp2(neg_abs)) * LOG2E
            top = lax.bitcast_convert_type(lax.bitcast_convert_type(sp, u32) & u32(0xFFFF0000), F32)
            zs.append(z)
            parts += [top.astype(BF16), (sp - top).astype(BF16)]
        sums = _dot(jnp.concatenate(parts, axis=0), neg_suffix)
        new_accs, new_rests = [], []
        for h in range(nh):
            sfx = sums[2 * h * blk:(2 * h + 1) * blk] + sums[(2 * h + 1) * blk:(2 * h + 2) * blk]
            attn = jnp.exp2(zs[h] + sfx + rests[h])
            new_accs.append(accs[h] + _dot(attn.astype(BF16), v_ref[rows, lanes[h]]))
            new_rests.append(rests[h] + sfx[:, 0:1])
        worst = functools.reduce(jnp.maximum, new_rests)
        return lo, jnp.max(worst) > SB_ZERO_LOG2, tuple(new_accs), tuple(new_rests)

    init = ((i + 1) * blk, True, tuple(jnp.zeros((blk, HEAD), F32) for _ in range(nh)),
            tuple(jnp.zeros((blk, 1), F32) for _ in range(nh)))
    accs = lax.while_loop(more, window, window(init))[2]
    o_ref[...] = jnp.concatenate(accs, axis=1).astype(o_ref.dtype)


def _sb_attn(q, kv, batch, seq):
    t, d = q.shape
    nq = seq // SB_BLOCK
    w = SB_LANES
    hp = d // w
    win = SB_WINDOW
    assert d % w == 0 and seq % SB_BLOCK == 0 and seq >= win and win == 2 * SB_BLOCK
    tri = -jnp.tril(jnp.ones((win, win), F32)).astype(BF16)
    once = pl.Buffered(1)
    return pl.pallas_call(
        _sb_attn_body, grid=(batch, hp, nq),
        in_specs=[pl.BlockSpec((SB_BLOCK, w), lambda b, h, i: (b * nq + i, h)),
                  pl.BlockSpec((seq, w), lambda b, h, i: (b, h), pipeline_mode=once),
                  pl.BlockSpec((seq, w), lambda b, h, i: (b, hp + h), pipeline_mode=once),
                  pl.BlockSpec((win, win), lambda b, h, i: (0, 0)),
                  pl.BlockSpec((4, SB_BLOCK, win), lambda b, h, i: (0, 0, 0))],
        out_specs=pl.BlockSpec((SB_BLOCK, w), lambda b, h, i: (b * nq + i, h)),
        out_shape=jax.ShapeDtypeStruct((t, d), BF16),
        compiler_params=_params("parallel", "parallel", "arbitrary"), name="sb_attn",
    )(q, kv, kv, tri, _sb_masks())


def _router_weights(w_group, b_group, w_expert, b_expert):
    d = w_group.shape[0]
    pad = LANES - N_GROUPS - N_EXPERTS
    w = jnp.concatenate([w_group, w_expert, jnp.zeros((d, pad), F32)], axis=1)
    b = jnp.concatenate([b_group, b_expert, jnp.zeros((pad,), F32)]).reshape(1, LANES)
    return w, b


def kernel(x, mem, norm_mix, norm_mem, norm_ffn, norm_final, rwkv_mu, rwkv_w_rkv, rwkv_w0, rwkv_w1, rwkv_w2, rwkv_a0, rwkv_a1, rwkv_a2, rwkv_g1, rwkv_g2, rwkv_k_k, rwkv_k_a, rwkv_r_k, rwkv_ln_g, rwkv_ln_b, rwkv_w_o, kv_norm, w_kv_shared, sb_w_q, sb_w_o, mem_norm_kv, mem_w_q, mem_w_kv, mem_w_o, moe_w_group, moe_b_group, moe_w_expert, moe_b_expert, moe_w_gate, moe_w_up, moe_w_down):
    batch, seq, d = x.shape
    depth = norm_mix.shape[0]
    n_a = rwkv_mu.shape[0]
    mem_tokens = mem.shape[1]
    xs = x.reshape(batch * seq, d)
    memf = mem.reshape(batch * mem_tokens, d)
    q = kv = None
    for l in range(depth):
        if l < n_a:
            r, lw, k, v, kk, asg, g = _rwkv_proj(
                xs, seq, norm_mix[l], rwkv_mu[l], rwkv_w_rkv[l], rwkv_w0[l], rwkv_w1[l], rwkv_w2[l],
                rwkv_a0[l], rwkv_a1[l], rwkv_a2[l], rwkv_g1[l], rwkv_g2[l], rwkv_k_k[l], rwkv_k_a[l])
            mix = _wide_scan(r, lw, k, v, kk, asg, rwkv_r_k[l], rwkv_ln_g[l], rwkv_ln_b[l], batch, seq)
            gate, w_mix = g, rwkv_w_o[l]
        else:
            mix = _sb_attn(q, kv, batch, seq)
            gate, w_mix = None, sb_w_o[l - n_a]
        mkv = _norm_linear(memf, mem_norm_kv[l], mem_w_kv[l], BF16)
        w_router, b_router = _router_weights(moe_w_group[l], moe_b_group[l], moe_w_expert[l], moe_b_expert[l])
        xs, xn, *routing = _mem_attn(xs, mix, gate, w_mix, mkv, seq, mem_tokens, norm_mem[l], mem_w_q[l],
                                     mem_w_o[l], norm_ffn[l], w_router, b_router)
        last = l == depth - 1
        if last:
            norms, weights = [norm_final], []
        elif l + 1 < n_a:
            norms, weights = [], []
        else:
            norms, weights = [norm_mix[l + 1]], [sb_w_q[l + 1 - n_a]]
            if l == n_a - 1:
                norms, weights = norms + [kv_norm], weights + [w_kv_shared]
        res = _moe(xs, xn, routing, moe_w_gate[l], moe_w_up[l], moe_w_down[l], norms, weights, final_norm=last)
        if last:
            return res[0].reshape(batch, seq, d)
        xs = res[0]
        if weights:
            q = res[1]
            if l == n_a - 1:
                kv = res[2]
```
